```python
import math
import jax, jax.numpy as jnp
from jax import lax
import numpy as np

D_MODEL = 1024
BATCH = 32
SEQ = 256
DEPTH = 4
DEC_BATCH = 8
DEC_SEQ = 2048
PAST_LEN = 512

GRID_W = 64
EPS = 1e-6
N_MOD = 6
N_BRANCH = 4
BR_W = D_MODEL // N_BRANCH
GLA_HEADS = 4
GLA_DV = BR_W // GLA_HEADS
GLA_DK = GLA_DV // 2
GLA_RANK = 16
GLA_TAU = 16.0
GLA_CHUNK = 32
RET_HEADS = 4
RET_DV = BR_W // RET_HEADS
RET_DK = RET_DV // 2
RET_CHUNK = 64
ROPE_BASE = 10000.0
S5_GROUP = 16
S5_GROUPS = BR_W // S5_GROUP
S5_STATE = 64
S5_RE_MAX = -1e-4
LRU_BLOCKS = 4
LRU_BW = BR_W // LRU_BLOCKS
LRU_CONV = 4
LRU_C = 8.0
PEER_HEADS = 8
PEER_KEYS = 128
PEER_EXPERTS = PEER_KEYS * PEER_KEYS
PEER_TOPK = 16
PEER_QDIM = 128
PEER_BLOCK = 128
SPLITS = (GLA_HEADS * GLA_DK, GLA_HEADS * GLA_DK, BR_W, BR_W, 2 * GLA_RANK,
          RET_HEADS * RET_DK, RET_HEADS * RET_DK, BR_W, BR_W,
          BR_W,
          BR_W, BR_W,
          N_BRANCH * D_MODEL)
N_IN = sum(SPLITS)

kernel_name = "hybrid_prefix_diffusion_step"

F32 = jnp.float32


def rms_norm(x, gain):
    xf = x.astype(F32)
    y = xf * lax.rsqrt(jnp.mean(xf * xf, axis=-1, keepdims=True) + EPS) * gain.astype(F32)
    return y.astype(x.dtype)


def head_rms(o):
    return o * lax.rsqrt(jnp.mean(o * o, axis=-1, keepdims=True) + EPS)


def adaln(cond, w_mod, b_mod):
    m = jax.nn.silu(cond) @ w_mod + b_mod
    return tuple(t[:, None, :] for t in jnp.split(m, N_MOD, axis=-1))


def to_heads(t, n_heads):
    b, s, _ = t.shape
    return t.reshape(b, s, n_heads, -1).transpose(0, 2, 1, 3)


def from_heads(t):
    b, h, s, v = t.shape
    return t.transpose(0, 2, 1, 3).reshape(b, s, h * v)


def _linear_combine(e1, e2):
    a1, b1 = e1
    a2, b2 = e2
    return a1 * a2, a2 * b1 + b2


def grid_rope(x):
    t_len, k_dim = x.shape[2], x.shape[3]
    rows = t_len // GRID_W
    row = jnp.repeat(jnp.arange(rows), GRID_W).astype(F32)
    col = jnp.tile(jnp.arange(GRID_W), rows).astype(F32)
    n_freq = k_dim // 4
    inv_freq = ROPE_BASE ** (-jnp.arange(n_freq, dtype=F32) / n_freq)
    ang = jnp.concatenate([row[:, None] * inv_freq, col[:, None] * inv_freq], axis=-1)
    cos, sin = jnp.cos(ang), jnp.sin(ang)
    x1, x2 = x[..., : k_dim // 2], x[..., k_dim // 2:]
    return jnp.concatenate([x1 * cos - x2 * sin, x1 * sin + x2 * cos], axis=-1)


def chunk_state_scan(q_in, k_in, v, a_chunk, s0):
    def step(s, inp):
        qc, kc, vc, ac = inp
        o = jnp.einsum('bhck,bhkv->bhcv', qc, s)
        s = ac[..., None] * s + jnp.einsum('bhck,bhcv->bhkv', kc, vc)
        return s, o
    xs = tuple(jnp.moveaxis(t, 2, 0) for t in (q_in, k_in, v, a_chunk))
    s_fin, o = lax.scan(step, s0, xs)
    return jnp.moveaxis(o, 0, 2), s_fin


def gla_direction(q, k, v, log_a, s0, reverse):
    if reverse:
        q, k, v, log_a = (jnp.flip(t, axis=2) for t in (q, k, v, log_a))
    b, h, t_len, _ = q.shape
    n = t_len // GLA_CHUNK
    q, k, v, log_a = (t.astype(F32).reshape(b, h, n, GLA_CHUNK, -1) for t in (q, k, v, log_a))
    cum = jnp.cumsum(log_a, axis=3)
    causal = jnp.tril(jnp.ones((GLA_CHUNK, GLA_CHUNK), dtype=bool))
    diff = cum[:, :, :, :, None, :] - cum[:, :, :, None, :, :]
    decay = jnp.exp(jnp.where(causal[:, :, None], diff, -jnp.inf))
    scores = jnp.einsum('bhnik,bhnjk,bhnijk->bhnij', q, k, decay)
    o_intra = jnp.einsum('bhnij,bhnjv->bhniv', scores, v)
    last = cum[:, :, :, -1:, :]
    o_inter, s_fin = chunk_state_scan(q * jnp.exp(cum), k * jnp.exp(last - cum), v,
                                      jnp.exp(last[:, :, :, 0]), s0)
    o = (o_intra + o_inter).reshape(b, h, t_len, -1)
    if reverse:
        o = jnp.flip(o, axis=2)
    return o, s_fin


def retention_direction(q, k, v, log_g, s0, reverse):
    if reverse:
        q, k, v = (jnp.flip(t, axis=2) for t in (q, k, v))
    b, h, t_len, k_dim = q.shape
    n = t_len // RET_CHUNK
    q, k, v = (t.astype(F32).reshape(b, h, n, RET_CHUNK, -1) for t in (q, k, v))
    pos = jnp.arange(RET_CHUNK, dtype=F32)
    rel = pos[:, None] - pos[None, :]
    d_intra = jnp.where(rel >= 0, jnp.exp(log_g[:, None, None] * jnp.maximum(rel, 0.0)), 0.0)
    scores = jnp.einsum('bhnik,bhnjk->bhnij', q, k) * d_intra[None, :, None]
    o_intra = jnp.einsum('bhnij,bhnjv->bhniv', scores, v)
    wq = jnp.exp(log_g[:, None] * (pos + 1.0))[None, :, None, :, None]
    wk = jnp.exp(log_g[:, None] * (RET_CHUNK - 1.0 - pos))[None, :, None, :, None]
    a_c = jnp.broadcast_to(jnp.exp(log_g * RET_CHUNK)[None, :, None, None], (b, h, n, k_dim))
    o_inter, s_fin = chunk_state_scan(q * wq, k * wk, v, a_c, s0)
    o = (o_intra + o_inter).reshape(b, h, t_len, -1)
    if reverse:
        o = jnp.flip(o, axis=2)
    return o, s_fin


def s5_mixer(u, a_re, a_im, log_dt, b_re, b_im, c_re, c_im, d_skip, w_glu, h0):
    b, t_len, _ = u.shape
    uf = u.astype(F32)
    ug = uf.reshape(b, t_len, S5_GROUPS, S5_GROUP).astype(jnp.complex64)
    bmat = lax.complex(b_re.astype(F32), b_im.astype(F32))
    cmat = lax.complex(c_re.astype(F32), c_im.astype(F32))
    bu = jnp.einsum('gpc,btgc->btgp', bmat, ug)
    states, finals = [], []
    for dr, reverse in ((0, False), (1, True)):
        lam = lax.complex(jnp.minimum(a_re[dr].astype(F32), S5_RE_MAX), a_im[dr].astype(F32))
        dt = jnp.exp(log_dt[dr].astype(F32))[:, None]
        lam_bar = jnp.exp(lam * dt)
        drive = ((lam_bar - 1.0) / lam) * bu
        edge = -1 if reverse else 0
        drive = drive.at[:, edge].add(lam_bar * h0[:, dr])
        decay = jnp.broadcast_to(lam_bar, drive.shape)
        _, hs = lax.associative_scan(_linear_combine, (decay, drive), reverse=reverse, axis=1)
        states.append(hs)
        finals.append(hs[:, 0] if reverse else hs[:, -1])
    y = jnp.real(jnp.einsum('gcp,btgp->btgc', cmat, states[0] + states[1])).reshape(b, t_len, BR_W)
    y = jax.nn.gelu(y + d_skip.astype(F32) * uf)
    ga, gb = jnp.split(y @ w_glu.astype(F32), 2, axis=-1)
    return ga * jax.nn.sigmoid(gb), jnp.stack(finals, axis=1)


def rglru_mixer(x, gate_in, conv_w, conv_b, w_a, b_a, w_x, b_x, lam, h0):
    b, t_len, _ = x.shape
    xc = lax.conv_general_dilated(
        x.astype(F32), conv_w.astype(F32)[:, None, :], window_strides=(1,),
        padding=[(LRU_CONV // 2 - 1, LRU_CONV // 2)],
        dimension_numbers=('NWC', 'WIO', 'NWC'), feature_group_count=BR_W) + conv_b.astype(F32)
    xb = xc.reshape(b, t_len, LRU_BLOCKS, LRU_BW)
    states, finals = [], []
    for dr, reverse in ((0, False), (1, True)):
        r = jax.nn.sigmoid(jnp.einsum('btni,nij->btnj', xb, w_a[dr].astype(F32)).reshape(b, t_len, BR_W)
                           + b_a[dr].astype(F32))
        i = jax.nn.sigmoid(jnp.einsum('btni,nij->btnj', xb, w_x[dr].astype(F32)).reshape(b, t_len, BR_W)
                           + b_x[dr].astype(F32))
        log_a = -LRU_C * r * jax.nn.softplus(-lam[dr].astype(F32))
        a = jnp.exp(log_a)
        drive = jnp.sqrt(-jnp.expm1(2.0 * log_a)) * (i * xc)
        edge = -1 if reverse else 0
        drive = drive.at[:, edge].add(a[:, edge] * h0[:, dr])
        _, hs = lax.associative_scan(_linear_combine, (a, drive), reverse=reverse, axis=1)
        states.append(hs)
        finals.append(hs[:, 0] if reverse else hs[:, -1])
    y = (states[0] + states[1]) * jax.nn.gelu(gate_in.astype(F32))
    return y, jnp.stack(finals, axis=1)


def token_mixers(h, p, states0, grid_pos):
    b, t_len, _ = h.shape
    z = h @ p['w_in']
    (ga_q, ga_k, ga_v, ga_g, ga_lr, rb_q, rb_k, rb_v, rb_g,
     s5_u, lru_x, lru_y, merge) = jnp.split(z, np.cumsum(SPLITS)[:-1].tolist(), axis=-1)

    gq = to_heads(ga_q, GLA_HEADS).astype(F32) * GLA_DK ** -0.5
    gk = to_heads(ga_k, GLA_HEADS).astype(F32)
    gv = to_heads(ga_v, GLA_HEADS).astype(F32)
    lr = ga_lr.astype(F32).reshape(b, t_len, 2, GLA_RANK)
    s_gla0 = states0[0].astype(F32)
    outs, fins = [], []
    for dr, rev in ((0, False), (1, True)):
        log_a = jax.nn.log_sigmoid(lr[:, :, dr] @ p['gla_w_decay'][dr].astype(F32)
                                   + p['gla_b_decay'][dr].astype(F32)) / GLA_TAU
        o, s = gla_direction(gq, gk, gv, to_heads(log_a, GLA_HEADS), s_gla0[:, dr], rev)
        outs.append(o)
        fins.append(s)
    m_a = from_heads(head_rms(outs[0] + outs[1])) * jax.nn.silu(ga_g.astype(F32))
    st_gla = jnp.stack(fins, axis=1)

    rq = to_heads(rb_q, RET_HEADS).astype(F32) * RET_DK ** -0.5
    rk = to_heads(rb_k, RET_HEADS).astype(F32)
    rv = to_heads(rb_v, RET_HEADS).astype(F32)
    if grid_pos:
        rq, rk = grid_rope(rq), grid_rope(rk)
    s_ret0 = states0[1].astype(F32)
    outs, fins = [], []
    for dr, rev in ((0, False), (1, True)):
        log_g = jax.nn.log_sigmoid(p['ret_decay_logit'][dr].astype(F32))
        o, s = retention_direction(rq, rk, rv, log_g, s_ret0[:, dr], rev)
        outs.append(o)
        fins.append(s)
    m_b = from_heads(head_rms(outs[0] + outs[1])) * jax.nn.silu(rb_g.astype(F32))
    st_ret = jnp.stack(fins, axis=1)

    s5_0 = states0[2].astype(F32)
    m_c, s5_fin = s5_mixer(s5_u, p['s5_a_re'], p['s5_a_im'], p['s5_log_dt'], p['s5_b_re'], p['s5_b_im'],
                           p['s5_c_re'], p['s5_c_im'], p['s5_d'], p['s5_w_glu'],
                           lax.complex(s5_0[:, :, 0], s5_0[:, :, 1]))
    st_s5 = jnp.stack([jnp.real(s5_fin), jnp.imag(s5_fin)], axis=2)

    m_d, st_lru = rglru_mixer(lru_x, lru_y, p['lru_conv_w'], p['lru_conv_b'], p['lru_w_a'], p['lru_b_a'],
                              p['lru_w_x'], p['lru_b_x'], p['lru_lambda'], states0[3].astype(F32))

    branches = jnp.stack([m_a, m_b, m_c, m_d], axis=2)
    proj = jnp.einsum('btnw,nwd->btnd', branches, p['w_branch'].astype(F32))
    gates = jax.nn.sigmoid(merge.astype(F32).reshape(b, t_len, N_BRANCH, D_MODEL))
    merged = jnp.sum(gates * proj, axis=2)
    out = merged.astype(h.dtype) @ p['w_out']
    return out, (st_gla, st_ret, st_s5, st_lru)


def peer_ffn(h, w_q, sub_keys, u_tab, v_tab):
    b, t_len, d = h.shape
    tok = h.reshape(b * t_len, d)
    n = tok.shape[0]
    q = (tok @ w_q).astype(F32).reshape(n, PEER_HEADS, 2, PEER_QDIM // 2)
    scores = jnp.einsum('nhsq,hskq->nhsk', q, sub_keys.astype(F32))
    half_s, half_i = lax.top_k(scores, PEER_TOPK)
    cand_s = (half_s[:, :, 0, :, None] + half_s[:, :, 1, None, :]).reshape(n, PEER_HEADS, PEER_TOPK * PEER_TOPK)
    cand_i = (half_i[:, :, 0, :, None] * PEER_KEYS + half_i[:, :, 1, None, :]).reshape(n, PEER_HEADS, PEER_TOPK * PEER_TOPK)
    top_s, top_pos = lax.top_k(cand_s, PEER_TOPK)
    expert = jnp.take_along_axis(cand_i, top_pos, axis=-1)
    gate = jax.nn.softmax(top_s, axis=-1)
    nb = n // PEER_BLOCK
    xs = (tok.reshape(nb, PEER_BLOCK, d),
          expert.reshape(nb, PEER_BLOCK, PEER_HEADS * PEER_TOPK),
          gate.reshape(nb, PEER_BLOCK, PEER_HEADS * PEER_TOPK))

    def block(args):
        xb, eb, gb = args
        u = jnp.take(u_tab, eb, axis=0)
        act = jax.nn.gelu(jnp.einsum('td,ted->te', xb, u).astype(F32))
        v = jnp.take(v_tab, eb, axis=0)
        return jnp.einsum('te,ted->td', (gb * act).astype(v.dtype), v)

    out = lax.map(block, xs)
    return out.reshape(b, t_len, d).astype(h.dtype)


def trunk_layer(x, mod, p, states0, grid_pos):
    sh1, sc1, g1, sh2, sc2, g2 = mod
    h = (rms_norm(x, p['norm_mix']) * (1.0 + sc1) + sh1).astype(x.dtype)
    mix, states = token_mixers(h, p, states0, grid_pos)
    x = x + (g1 * mix).astype(x.dtype)
    h = (rms_norm(x, p['norm_ffn']) * (1.0 + sc2) + sh2).astype(x.dtype)
    x = x + (g2 * peer_ffn(h, p['peer_w_q'], p['peer_keys'], p['peer_u'], p['peer_v'])).astype(x.dtype)
    return x, states


def setup_inputs(seed: int = 0) -> dict:
    key = jax.random.key(seed)
    ks = iter(jax.random.split(key, 48))

    def nrm(shape, scale):
        return jax.random.normal(next(ks), shape, F32) * scale

    ret_base = jnp.log(2.0 ** (5.0 + jnp.arange(RET_HEADS, dtype=F32)) - 1.0)
    a0 = jax.random.uniform(next(ks), (DEPTH, 2, BR_W), F32, minval=0.9, maxval=0.999)
    p_lru = a0 ** (1.0 / LRU_C)
    return {
        "x_prompt": nrm((BATCH, SEQ, D_MODEL), 1.0),
        "x_sample": nrm((DEC_BATCH, DEC_SEQ, D_MODEL), 1.0),
        "c": nrm((DEC_BATCH, D_MODEL), 1.0),
        "state_gla": nrm((DEC_BATCH, DEPTH, 2, GLA_HEADS, GLA_DK, GLA_DV), 0.5),
        "state_ret": nrm((DEC_BATCH, DEPTH, 2, RET_HEADS, RET_DK, RET_DV), 1.0),
        "state_s5": nrm((DEC_BATCH, DEPTH, 2, 2, S5_GROUPS, S5_STATE), 0.5),
        "state_lru": nrm((DEC_BATCH, DEPTH, 2, BR_W), 0.5),
        "c_ctx": nrm((D_MODEL,), 1.0),
        "w_mod": nrm((DEPTH, D_MODEL, N_MOD * D_MODEL), D_MODEL ** -0.5),
        "b_mod": nrm((DEPTH, N_MOD * D_MODEL), 0.02),
        "norm_mix": 1.0 + nrm((DEPTH, D_MODEL), 0.02),
        "norm_ffn": 1.0 + nrm((DEPTH, D_MODEL), 0.02),
        "norm_final": 1.0 + nrm((D_MODEL,), 0.02),
        "w_in": nrm((DEPTH, D_MODEL, N_IN), D_MODEL ** -0.5),
        "gla_w_decay": nrm((DEPTH, 2, GLA_RANK, GLA_HEADS * GLA_DK), GLA_RANK ** -0.5),
        "gla_b_decay": nrm((DEPTH, 2, GLA_HEADS * GLA_DK), 0.1),
        "ret_decay_logit": ret_base + nrm((DEPTH, 2, RET_HEADS), 0.05),
        "s5_a_re": -0.5 + nrm((DEPTH, 2, S5_GROUPS, S5_STATE), 0.01),
        "s5_a_im": jnp.pi * jnp.arange(S5_STATE, dtype=F32) + nrm((DEPTH, 2, S5_GROUPS, S5_STATE), 0.01),
        "s5_log_dt": jax.random.uniform(next(ks), (DEPTH, 2, S5_GROUPS), F32,
                                        minval=math.log(1e-3), maxval=math.log(1e-1)),
        "s5_b_re": nrm((DEPTH, S5_GROUPS, S5_STATE, S5_GROUP), (2 * S5_GROUP) ** -0.5),
        "s5_b_im": nrm((DEPTH, S5_GROUPS, S5_STATE, S5_GROUP), (2 * S5_GROUP) ** -0.5),
        "s5_c_re": nrm((DEPTH, S5_GROUPS, S5_GROUP, S5_STATE), S5_STATE ** -0.5),
        "s5_c_im": nrm((DEPTH, S5_GROUPS, S5_GROUP, S5_STATE), S5_STATE ** -0.5),
        "s5_d": nrm((DEPTH, BR_W), 1.0),
        "s5_w_glu": nrm((DEPTH, BR_W, 2 * BR_W), BR_W ** -0.5),
        "lru_conv_w": nrm((DEPTH, LRU_CONV, BR_W), LRU_CONV ** -0.5),
        "lru_conv_b": nrm((DEPTH, BR_W), 0.02),
        "lru_w_a": nrm((DEPTH, 2, LRU_BLOCKS, LRU_BW, LRU_BW), LRU_BW ** -0.5),
        "lru_b_a": nrm((DEPTH, 2, BR_W), 0.02),
        "lru_w_x": nrm((DEPTH, 2, LRU_BLOCKS, LRU_BW, LRU_BW), LRU_BW ** -0.5),
        "lru_b_x": nrm((DEPTH, 2, BR_W), 0.02),
        "lru_lambda": jnp.log(p_lru) - jnp.log1p(-p_lru),
        "w_branch": nrm((DEPTH, N_BRANCH, BR_W, D_MODEL), BR_W ** -0.5),
        "w_out": nrm((DEPTH, D_MODEL, D_MODEL), D_MODEL ** -0.5),
        "peer_w_q": nrm((DEPTH, D_MODEL, PEER_HEADS * PEER_QDIM), D_MODEL ** -0.5),
        "peer_keys": nrm((DEPTH, PEER_HEADS, 2, PEER_KEYS, PEER_QDIM // 2), (PEER_QDIM // 2) ** -0.5),
        "peer_u": nrm((DEPTH, PEER_EXPERTS, D_MODEL), D_MODEL ** -0.5),
        "peer_v": nrm((DEPTH, PEER_EXPERTS, D_MODEL), PEER_HEADS ** -0.5),
    }


def reference(x_prompt, x_sample, c, state_gla, state_ret, state_s5, state_lru, c_ctx,
              w_mod, b_mod, norm_mix, norm_ffn, norm_final, w_in, gla_w_decay, gla_b_decay,
              ret_decay_logit, s5_a_re, s5_a_im, s5_log_dt, s5_b_re, s5_b_im, s5_c_re, s5_c_im,
              s5_d, s5_w_glu, lru_conv_w, lru_conv_b, lru_w_a, lru_b_a, lru_w_x, lru_b_x, lru_lambda,
              w_branch, w_out, peer_w_q, peer_keys, peer_u, peer_v):
    nb = x_prompt.shape[0]
    zero_states = (jnp.zeros((nb, 2, GLA_HEADS, GLA_DK, GLA_DV), F32),
                   jnp.zeros((nb, 2, RET_HEADS, RET_DK, RET_DV), F32),
                   jnp.zeros((nb, 2, 2, S5_GROUPS, S5_STATE), F32),
                   jnp.zeros((nb, 2, BR_W), F32))
    y_p, y_s = x_prompt, x_sample
    gla_l, ret_l, s5_l, lru_l = [], [], [], []
    for l in range(DEPTH):
        p = dict(norm_mix=norm_mix[l], norm_ffn=norm_ffn[l], w_in=w_in[l],
                 gla_w_decay=gla_w_decay[l], gla_b_decay=gla_b_decay[l],
                 ret_decay_logit=ret_decay_logit[l],
                 s5_a_re=s5_a_re[l], s5_a_im=s5_a_im[l], s5_log_dt=s5_log_dt[l],
                 s5_b_re=s5_b_re[l], s5_b_im=s5_b_im[l], s5_c_re=s5_c_re[l], s5_c_im=s5_c_im[l],
                 s5_d=s5_d[l], s5_w_glu=s5_w_glu[l],
                 lru_conv_w=lru_conv_w[l], lru_conv_b=lru_conv_b[l], lru_w_a=lru_w_a[l], lru_b_a=lru_b_a[l],
                 lru_w_x=lru_w_x[l], lru_b_x=lru_b_x[l], lru_lambda=lru_lambda[l],
                 w_branch=w_branch[l], w_out=w_out[l],
                 peer_w_q=peer_w_q[l], peer_keys=peer_keys[l], peer_u=peer_u[l], peer_v=peer_v[l])
        mod_ctx = adaln(c_ctx[None, :], w_mod[l], b_mod[l])
        y_p, (sg, sr, ss, sl) = trunk_layer(y_p, mod_ctx, p, zero_states, False)
        gla_l.append(sg)
        ret_l.append(sr)
        s5_l.append(ss)
        lru_l.append(sl)
        mod_lat = adaln(c, w_mod[l], b_mod[l])
        y_s, _ = trunk_layer(y_s, mod_lat, p,
                             (state_gla[:, l], state_ret[:, l], state_s5[:, l], state_lru[:, l]), True)
    y_p = rms_norm(y_p, norm_final)
    y_s = rms_norm(y_s, norm_final)
    new_state_gla = jnp.stack(gla_l, axis=1)
    new_state_ret = jnp.stack(ret_l, axis=1)
    new_state_s5 = jnp.stack(s5_l, axis=1)
    new_state_lru = jnp.stack(lru_l, axis=1)
    return (y_p, y_s, new_state_gla, new_state_ret, new_state_s5, new_state_lru)
```

```python
import functools
import math

import jax
import jax.numpy as jnp
from jax import lax
from jax.experimental import pallas as pl
from jax.experimental.pallas import tpu as pltpu

F32 = jnp.float32
BF16 = jnp.bfloat16
HI = lax.Precision.HIGHEST

D_MODEL = 1024
N_MOD = 6
EPS = 1e-6
BR_W = 256
GLA_HEADS, GLA_DK, GLA_DV, GLA_RANK, GLA_TAU, GLA_CHUNK = 4, 32, 64, 16, 16.0, 32
RET_HEADS, RET_DK, RET_DV, RET_CHUNK = 4, 32, 64, 64
ROPE_BASE = 10000.0
GRID_W = 64
S5_GROUP, S5_GROUPS, S5_STATE, S5_RE_MAX = 16, 16, 64, -1e-4
S5_CH = S5_GROUPS * S5_STATE
LRU_BLOCKS, LRU_BW, LRU_C = 4, 64, 8.0
PEER_HEADS, PEER_KEYS, PEER_TOPK, PEER_QDIM = 8, 128, 16, 128
PEER_EXPERTS = PEER_KEYS * PEER_KEYS

LANES = 128
VMEM_LIMIT = 56 * 1024 * 1024

Z_MERGE, Z_GLA, Z_RET, Z_LRU, Z_S5 = 0, 4096, 5120, 6144, 6656
Z_W = 6912

SCAN_TB = 128
NEG_INF = float("-inf")


def _cparams(sem):
    return pltpu.CompilerParams(dimension_semantics=sem, vmem_limit_bytes=VMEM_LIMIT)


def _nt(a, b):
    return lax.dot_general(a, b, (((1,), (1,)), ((), ())), preferred_element_type=F32)


def _tn(a, b):
    return lax.dot_general(a, b, (((0,), (0,)), ((), ())), preferred_element_type=F32)


def _mm(a, b):
    return jnp.dot(a, b, preferred_element_type=F32)


def _mm_hi(a, b):
    return jnp.dot(a, b, preferred_element_type=F32, precision=HI)


def _sigmoid(x):
    return jax.nn.sigmoid(x)


def _silu(x):
    return x * jax.nn.sigmoid(x)


def _gelu(x):
    return jax.nn.gelu(x)


def _log_sigmoid(x):
    return jnp.minimum(x, 0.0) - jnp.log(1.0 + jnp.exp(-jnp.abs(x)))


def _softplus(x):
    return jnp.maximum(x, 0.0) + jnp.log(1.0 + jnp.exp(-jnp.abs(x)))


def _rms_mod(x, gain, sc, sh):
    ms = jnp.mean(x * x, axis=-1, keepdims=True)
    return x * lax.rsqrt(ms + EPS) * gain * (1.0 + sc) + sh


def _adaln_kernel(c_ref, w_ref, b_ref, o_ref):
    o_ref[...] = _mm_hi(_silu(c_ref[...]), w_ref[...]) + b_ref[...]


def _adaln(cond, w_mod, b_mod):
    n_l = w_mod.shape[0]
    rows = cond.shape[0]
    tn = 1536
    return pl.pallas_call(
        _adaln_kernel,
        grid=(n_l, N_MOD * D_MODEL // tn),
        in_specs=[
            pl.BlockSpec((rows, D_MODEL), lambda l, j: (0, 0)),
            pl.BlockSpec((None, D_MODEL, tn), lambda l, j: (l, 0, j)),
            pl.BlockSpec((None, 1, tn), lambda l, j: (l, 0, j)),
        ],
        out_specs=pl.BlockSpec((None, rows, tn), lambda l, j: (l, 0, j)),
        out_shape=jax.ShapeDtypeStruct((n_l, rows, N_MOD * D_MODEL), F32),
        compiler_params=_cparams(("parallel", "parallel")),
        name="adaln",
    )(cond, w_mod, b_mod.reshape(n_l, 1, N_MOD * D_MODEL))


def _premix_kernel(x_ref, sh_ref, sc_ref, g_ref, w_ref, o_ref, h_scr):
    @pl.when(pl.program_id(1) == 0)
    def _():
        h_scr[...] = _rms_mod(x_ref[...], g_ref[...], sc_ref[...], sh_ref[...]).astype(BF16)

    o_ref[...] = _mm(h_scr[...], w_ref[...])


def _premix(x, mod3, gain, w, row_of_tile, tm):
    nt = x.shape[0]
    tn = 1152
    return pl.pallas_call(
        _premix_kernel,
        grid=(nt // tm, Z_W // tn),
        in_specs=[
            pl.BlockSpec((tm, D_MODEL), lambda i, j: (i, 0)),
            pl.BlockSpec((None, 1, D_MODEL), lambda i, j: (row_of_tile(i), 0, 0)),
            pl.BlockSpec((None, 1, D_MODEL), lambda i, j: (row_of_tile(i), 0, 1)),
            pl.BlockSpec((1, D_MODEL), lambda i, j: (0, 0)),
            pl.BlockSpec((D_MODEL, tn), lambda i, j: (0, j)),
        ],
        out_specs=pl.BlockSpec((tm, tn), lambda i, j: (i, j)),
        out_shape=jax.ShapeDtypeStruct((nt, Z_W), F32),
        scratch_shapes=[pltpu.VMEM((tm, D_MODEL), BF16)],
        compiler_params=_cparams(("parallel", "arbitrary")),
        name="premix",
    )(x, mod3, mod3, gain, w)


def _gla_kernel(z_ref, wd_ref, bd_ref, s0_ref, e_ref, ind_ref, m_ref, sfin_ref,
                la_scr, of_scr, st_scr, p_scr, cum_scr, k_scr, v_scr):
    t_len = z_ref.shape[0]
    c = GLA_CHUNK
    n_chunks = t_len // c
    hk = GLA_HEADS * GLA_DK
    hv = GLA_HEADS * GLA_DV
    scale = GLA_DK ** -0.5

    pre = _mm_hi(z_ref[:, 768:896], wd_ref[...]) + bd_ref[...]
    la_scr[...] = _log_sigmoid(pre) * (1.0 / GLA_TAU)

    ri = lax.broadcasted_iota(jnp.int32, (c, c), 0)
    ci = lax.broadcasted_iota(jnp.int32, (c, c), 1)
    tri_lo = (ri >= ci).astype(F32)
    tri_up = (ri <= ci).astype(F32)
    row = lax.broadcasted_iota(jnp.int32, (c, hk), 0)
    bd_mask = (lax.broadcasted_iota(jnp.int32, (hv, hk), 0) // GLA_DV
               == lax.broadcasted_iota(jnp.int32, (hv, hk), 1) // GLA_DK).astype(F32)

    def chunk(base, la, tri, reverse):
        cum = _mm_hi(tri, la)
        q = z_ref[pl.ds(base, c), 0:128] * scale
        k = z_ref[pl.ds(base, c), 128:256]
        v = z_ref[pl.ds(base, c), 256:512]
        edge = cum[0:1, :] if reverse else cum[c - 1:c, :]
        st = st_scr[...]
        o = _nt((q * jnp.exp(cum)).astype(BF16), st.astype(BF16))
        ke = k * jnp.exp(edge - cum)
        cum_scr[...] = cum
        k_scr[...] = k
        v_scr[...] = v
        for j in range(c):
            d = jnp.minimum(cum - cum_scr[j:j + 1, :], 0.0)
            p = q * k_scr[j:j + 1, :] * jnp.exp(d)
            keep = (row <= j) if reverse else (row >= j)
            p_scr[j * c:(j + 1) * c, :] = jnp.where(keep, p, 0.0).astype(BF16)
        pe = _mm(p_scr[...], e_ref[...])
        for j in range(c):
            o = o + pe[j * c:(j + 1) * c, :] * v_scr[j:j + 1, :]
        st_scr[...] = st * jnp.exp(edge) + bd_mask * _tn(v.astype(BF16), ke.astype(BF16))
        return o

    st_scr[...] = s0_ref[0]

    def fwd(i, carry):
        base = pl.multiple_of(i * c, c)
        of_scr[pl.ds(base, c), :] = chunk(base, la_scr[pl.ds(base, c), 0:128], tri_lo, False)
        return carry

    lax.fori_loop(0, n_chunks, fwd, 0)
    sfin_ref[0] = st_scr[...]
    st_scr[...] = s0_ref[1]

    def bwd(i, carry):
        base = pl.multiple_of((n_chunks - 1 - i) * c, c)
        o = of_scr[pl.ds(base, c), :] + chunk(base, la_scr[pl.ds(base, c), 128:256], tri_up, True)
        ms = _mm_hi(o * o, ind_ref[...])
        g = z_ref[pl.ds(base, c), 512:768]
        m_ref[pl.ds(base, c), :] = o * lax.rsqrt(ms + EPS) * _silu(g)
        return carry

    lax.fori_loop(0, n_chunks, bwd, 0)
    sfin_ref[1] = st_scr[...]


def _gla(z, row_off, n_b, t_len, wd, bd, s0t, e_mat, ind):
    hk, hv = GLA_HEADS * GLA_DK, GLA_HEADS * GLA_DV
    c = GLA_CHUNK
    return pl.pallas_call(
        _gla_kernel,
        grid=(n_b,),
        in_specs=[
            pl.BlockSpec((t_len, 1024), lambda b: (row_off + b, Z_GLA // 1024)),
            pl.BlockSpec((128, 256), lambda b: (0, 0)),
            pl.BlockSpec((1, 256), lambda b: (0, 0)),
            pl.BlockSpec((None, 2, hv, hk), lambda b: (b, 0, 0, 0)),
            pl.BlockSpec((hk, hv), lambda b: (0, 0)),
            pl.BlockSpec((hv, hv), lambda b: (0, 0)),
        ],
        out_specs=[
            pl.BlockSpec((t_len, BR_W), lambda b: (b, 0)),
            pl.BlockSpec((None, 2, hv, hk), lambda b: (b, 0, 0, 0)),
        ],
        out_shape=[
            jax.ShapeDtypeStruct((n_b * t_len, BR_W), F32),
            jax.ShapeDtypeStruct((n_b, 2, hv, hk), F32),
        ],
        scratch_shapes=[
            pltpu.VMEM((t_len, 256), F32),
            pltpu.VMEM((t_len, hv), F32),
            pltpu.VMEM((hv, hk), F32),
            pltpu.VMEM((c * c, hk), BF16),
            pltpu.VMEM((c, hk), F32),
            pltpu.VMEM((c, hk), F32),
            pltpu.VMEM((c, hv), F32),
        ],
        compiler_params=_cparams(("parallel",)),
        name="gla",
    )(z, wd, bd, s0t, e_mat, ind)


def _ret_kernel(z_ref, cos_ref, sin_ref, swap_ref, lgl_ref, lgs_ref, s0_ref, ind_ref, m_ref, sfin_ref,
                qk_scr, of_scr, st_scr, *, rope):
    t_len = z_ref.shape[0]
    c = RET_CHUNK
    n_chunks = t_len // c
    hk = RET_HEADS * RET_DK
    hv = RET_HEADS * RET_DV
    scale = RET_DK ** -0.5

    lgf = lgl_ref[0:1, :]
    lgb = lgl_ref[1:2, :]
    pos = lax.broadcasted_iota(jnp.int32, (c, hk), 0).astype(F32)
    wq_f = jnp.exp(lgf * (pos + 1.0))
    wk_f = jnp.exp(lgf * (c - 1.0 - pos))
    wq_b = jnp.exp(lgb * (c - pos))
    wk_b = jnp.exp(lgb * pos)
    dec_f = jnp.exp(lgf * float(c))
    dec_b = jnp.exp(lgb * float(c))

    ii = lax.broadcasted_iota(jnp.int32, (c, c), 0)
    jj = lax.broadcasted_iota(jnp.int32, (c, c), 1)
    rel = (ii - jj).astype(F32)
    dms = []
    for h in range(RET_HEADS):
        d_f = jnp.where(ii >= jj, jnp.exp(lgs_ref[0, h] * jnp.maximum(rel, 0.0)), 0.0)
        d_b = jnp.where(jj >= ii, jnp.exp(lgs_ref[1, h] * jnp.maximum(-rel, 0.0)), 0.0)
        dms.append(d_f + d_b)
    dmat = jnp.concatenate(dms, axis=1)

    ek_mask = (lax.broadcasted_iota(jnp.int32, (RET_HEADS * c, hk), 0) // c
               == lax.broadcasted_iota(jnp.int32, (RET_HEADS * c, hk), 1) // RET_DK).astype(F32)
    ev_mask = (lax.broadcasted_iota(jnp.int32, (RET_HEADS * c, hv), 0) // c
               == lax.broadcasted_iota(jnp.int32, (RET_HEADS * c, hv), 1) // RET_DV).astype(F32)
    bd_mask = (lax.broadcasted_iota(jnp.int32, (hv, hk), 0) // RET_DV
               == lax.broadcasted_iota(jnp.int32, (hv, hk), 1) // RET_DK).astype(F32)

    st_scr[...] = s0_ref[0]

    def fwd(i, carry):
        base = pl.multiple_of(i * c, c)
        q = z_ref[pl.ds(base, c), 0:128] * scale
        k = z_ref[pl.ds(base, c), 128:256]
        v = z_ref[pl.ds(base, c), 256:512]
        if rope:
            cs = cos_ref[pl.ds(base, c), :]
            sn = sin_ref[pl.ds(base, c), :]
            q = q * cs + _mm_hi(q, swap_ref[...]) * sn
            k = k * cs + _mm_hi(k, swap_ref[...]) * sn
        qk_scr[pl.ds(base, c), 0:128] = q
        qk_scr[pl.ds(base, c), 128:256] = k
        kexp = (jnp.concatenate([k] * RET_HEADS, axis=0) * ek_mask).astype(BF16)
        vexp = (jnp.concatenate([v] * RET_HEADS, axis=0) * ev_mask).astype(BF16)
        sc = _nt(q.astype(BF16), kexp) * dmat
        o = _mm(sc.astype(BF16), vexp)
        st = st_scr[...]
        o = o + _nt((q * wq_f).astype(BF16), st.astype(BF16))
        of_scr[pl.ds(base, c), :] = o
        st_scr[...] = st * dec_f + bd_mask * _tn(v.astype(BF16), (k * wk_f).astype(BF16))
        return carry

    lax.fori_loop(0, n_chunks, fwd, 0)
    sfin_ref[0] = st_scr[...]
    st_scr[...] = s0_ref[1]

    def bwd(i, carry):
        base = pl.multiple_of((n_chunks - 1 - i) * c, c)
        q = qk_scr[pl.ds(base, c), 0:128]
        k = qk_scr[pl.ds(base, c), 128:256]
        v = z_ref[pl.ds(base, c), 256:512]
        st = st_scr[...]
        o = of_scr[pl.ds(base, c), :] + _nt((q * wq_b).astype(BF16), st.astype(BF16))
        st_scr[...] = st * dec_b + bd_mask * _tn(v.astype(BF16), (k * wk_b).astype(BF16))
        ms = _mm_hi(o * o, ind_ref[...])
        g = z_ref[pl.ds(base, c), 512:768]
        m_ref[pl.ds(base, c), :] = o * lax.rsqrt(ms + EPS) * _silu(g)
        return carry

    lax.fori_loop(0, n_chunks, bwd, 0)
    sfin_ref[1] = st_scr[...]


def _ret(z, row_off, n_b, t_len, cos_t, sin_t, swap, lgl, lgs, s0t, ind, rope):
    hk, hv = RET_HEADS * RET_DK, RET_HEADS * RET_DV
    return pl.pallas_call(
        functools.partial(_ret_kernel, rope=rope),
        grid=(n_b,),
        in_specs=[
            pl.BlockSpec((t_len, 1024), lambda b: (row_off + b, Z_RET // 1024)),
            pl.BlockSpec((t_len, hk), lambda b: (0, 0)),
            pl.BlockSpec((t_len, hk), lambda b: (0, 0)),
            pl.BlockSpec((hk, hk), lambda b: (0, 0)),
            pl.BlockSpec((2, hk), lambda b: (0, 0)),
            pl.BlockSpec(memory_space=pltpu.SMEM),
            pl.BlockSpec((None, 2, hv, hk), lambda b: (b, 0, 0, 0)),
            pl.BlockSpec((hv, hv), lambda b: (0, 0)),
        ],
        out_specs=[
            pl.BlockSpec((t_len, BR_W), lambda b: (b, 0)),
            pl.BlockSpec((None, 2, hv, hk), lambda b: (b, 0, 0, 0)),
        ],
        out_shape=[
            jax.ShapeDtypeStruct((n_b * t_len, BR_W), F32),
            jax.ShapeDtypeStruct((n_b, 2, hv, hk), F32),
        ],
        scratch_shapes=[
            pltpu.VMEM((t_len, 2 * hk), F32),
            pltpu.VMEM((t_len, hv), F32),
            pltpu.VMEM((hv, hk), F32),
        ],
        compiler_params=_cparams(("parallel",)),
        name="ret",
    )(z, cos_t, sin_t, swap, lgl, lgs, s0t, ind)


def _s5_disc_kernel(are_ref, aim_ref, ldt_ref, coef_ref, pf_ref, pb_ref):
    tb = pf_ref.shape[0]
    re = jnp.minimum(are_ref[...], S5_RE_MAX)
    im = aim_ref[...]
    dt = jnp.exp(ldt_ref[...])
    er = jnp.exp(re * dt)
    lbr = er * jnp.cos(im * dt)
    lbi = er * jnp.sin(im * dt)
    den = re * re + im * im
    nr = lbr - 1.0
    coef_ref[:, 0:S5_CH] = (nr * re + lbi * im) / den
    coef_ref[:, S5_CH:] = (lbi * re - nr * im) / den
    t = lax.broadcasted_iota(jnp.int32, (tb, S5_CH), 0).astype(F32)
    nf = t + 1.0
    nb = float(tb) - t
    mf = jnp.exp(nf * (re[0:1] * dt[0:1]))
    pf_ref[:, 0:S5_CH] = mf * jnp.cos(nf * (im[0:1] * dt[0:1]))
    pf_ref[:, S5_CH:] = mf * jnp.sin(nf * (im[0:1] * dt[0:1]))
    mb = jnp.exp(nb * (re[1:2] * dt[1:2]))
    pb_ref[:, 0:S5_CH] = mb * jnp.cos(nb * (im[1:2] * dt[1:2]))
    pb_ref[:, S5_CH:] = mb * jnp.sin(nb * (im[1:2] * dt[1:2]))


def _s5_disc(are, aim, ldt, tb):
    return pl.pallas_call(
        _s5_disc_kernel,
        out_shape=[
            jax.ShapeDtypeStruct((2, 2 * S5_CH), F32),
            jax.ShapeDtypeStruct((tb, 2 * S5_CH), F32),
            jax.ShapeDtypeStruct((tb, 2 * S5_CH), F32),
        ],
        compiler_params=pltpu.CompilerParams(vmem_limit_bytes=VMEM_LIMIT),
        name="s5_disc",
    )(are, aim, ldt)


def _s5_kernel(u_ref, bbd_ref, cre_ref, cim_ref, coef_ref, pf_ref, pb_ref, dsk_ref, wglu_ref, h0_ref,
               m_ref, hfin_ref, y_scr, bu_scr, hr_scr, hi_scr, car_scr):
    t_len = u_ref.shape[0]
    tb = pf_ref.shape[0]
    n_blocks = t_len // tb
    row = lax.broadcasted_iota(jnp.int32, (tb, LANES), 0)
    steps = [1 << s for s in range(int(math.log2(tb)))]

    def block(base, dr_i, reverse):
        p_ref = pb_ref if reverse else pf_ref
        u = u_ref[pl.ds(base, tb), :]
        bu_scr[...] = _mm(u.astype(BF16), bbd_ref[...])
        for g in range(S5_CH // LANES):
            lo, hi = g * LANES, (g + 1) * LANES
            br = bu_scr[:, lo:hi]
            bi = bu_scr[:, S5_CH + lo:S5_CH + hi]
            cr = coef_ref[dr_i:dr_i + 1, lo:hi]
            ci = coef_ref[dr_i:dr_i + 1, S5_CH + lo:S5_CH + hi]
            hr = cr * br - ci * bi
            hi_ = cr * bi + ci * br
            for d in steps:
                if reverse:
                    pr = p_ref[tb - d:tb - d + 1, lo:hi]
                    pi = p_ref[tb - d:tb - d + 1, S5_CH + lo:S5_CH + hi]
                    keep = row < tb - d
                    sr = jnp.where(keep, pltpu.roll(hr, tb - d, 0), 0.0)
                    si = jnp.where(keep, pltpu.roll(hi_, tb - d, 0), 0.0)
                else:
                    pr = p_ref[d - 1:d, lo:hi]
                    pi = p_ref[d - 1:d, S5_CH + lo:S5_CH + hi]
                    keep = row >= d
                    sr = jnp.where(keep, pltpu.roll(hr, d, 0), 0.0)
                    si = jnp.where(keep, pltpu.roll(hi_, d, 0), 0.0)
                hr, hi_ = hr + pr * sr - pi * si, hi_ + pr * si + pi * sr
            car = car_scr[0:1, lo:hi]
            cai = car_scr[0:1, S5_CH + lo:S5_CH + hi]
            pwr = p_ref[:, lo:hi]
            pwi = p_ref[:, S5_CH + lo:S5_CH + hi]
            hr, hi_ = hr + pwr * car - pwi * cai, hi_ + pwr * cai + pwi * car
            hr_scr[:, lo:hi] = hr
            hi_scr[:, lo:hi] = hi_
            edge = 0 if reverse else tb - 1
            car_scr[0:1, lo:hi] = hr[edge:edge + 1, :]
            car_scr[0:1, S5_CH + lo:S5_CH + hi] = hi_[edge:edge + 1, :]
        y = _mm(hr_scr[...].astype(BF16), cre_ref[...]) - _mm(hi_scr[...].astype(BF16), cim_ref[...])
        return u, y

    car_scr[0:1, :] = h0_ref[0:1, :]

    def fwd(i, carry):
        base = pl.multiple_of(i * tb, tb)
        _, y = block(base, 0, False)
        y_scr[pl.ds(base, tb), :] = y
        return carry

    lax.fori_loop(0, n_blocks, fwd, 0)
    hfin_ref[0:1, :] = car_scr[0:1, :]
    car_scr[0:1, :] = h0_ref[1:2, :]

    def bwd(i, carry):
        base = pl.multiple_of((n_blocks - 1 - i) * tb, tb)
        u, y = block(base, 1, True)
        y = _gelu(y_scr[pl.ds(base, tb), :] + y + dsk_ref[...] * u)
        gg = _mm(y.astype(BF16), wglu_ref[...])
        m_ref[pl.ds(base, tb), :] = gg[:, 0:BR_W] * _sigmoid(gg[:, BR_W:])
        return carry

    lax.fori_loop(0, n_blocks, bwd, 0)
    hfin_ref[1:2, :] = car_scr[0:1, :]


def _s5(z, row_off, n_b, t_len, bbd, cre, cim, coef, pf, pb, dsk, wglu, h0):
    tb = pf.shape[0]
    full = lambda shape: pl.BlockSpec(shape, lambda b: (0,) * len(shape))
    return pl.pallas_call(
        _s5_kernel,
        grid=(n_b,),
        in_specs=[
            pl.BlockSpec((t_len, BR_W), lambda b: (row_off + b, Z_S5 // BR_W)),
            full((BR_W, 2 * S5_CH)),
            full((S5_CH, BR_W)),
            full((S5_CH, BR_W)),
            full((2, 2 * S5_CH)),
            full((tb, 2 * S5_CH)),
            full((tb, 2 * S5_CH)),
            full((1, BR_W)),
            full((BR_W, 2 * BR_W)),
            pl.BlockSpec((None, 2, 2 * S5_CH), lambda b: (b, 0, 0)),
        ],
        out_specs=[
            pl.BlockSpec((t_len, BR_W), lambda b: (b, 0)),
            pl.BlockSpec((None, 2, 2 * S5_CH), lambda b: (b, 0, 0)),
        ],
        out_shape=[
            jax.ShapeDtypeStruct((n_b * t_len, BR_W), F32),
            jax.ShapeDtypeStruct((n_b, 2, 2 * S5_CH), F32),
        ],
        scratch_shapes=[
            pltpu.VMEM((t_len, BR_W), F32),
            pltpu.VMEM((tb, 2 * S5_CH), F32),
            pltpu.VMEM((tb, S5_CH), F32),
            pltpu.VMEM((tb, S5_CH), F32),
            pltpu.VMEM((8, 2 * S5_CH), F32),
        ],
        compiler_params=_cparams(("parallel",)),
        name="s5",
    )(z, bbd, cre, cim, coef, pf, pb, dsk, wglu, h0)


def _lru_kernel(z_ref, cw_ref, cb_ref, wg_ref, bg_ref, lam_ref, h0_ref, m_ref, hfin_ref,
                xc_scr, hf_scr, car_scr):
    t_len = z_ref.shape[0]
    tb = SCAN_TB
    n_blocks = t_len // tb
    steps = [1 << s for s in range(int(math.log2(tb)))]

    x = z_ref[:, 0:BR_W]
    trow = lax.broadcasted_iota(jnp.int32, (t_len, BR_W), 0)
    xm1 = jnp.where(trow >= 1, pltpu.roll(x, 1, 0), 0.0)
    xp1 = jnp.where(trow < t_len - 1, pltpu.roll(x, t_len - 1, 0), 0.0)
    xp2 = jnp.where(trow < t_len - 2, pltpu.roll(x, t_len - 2, 0), 0.0)
    xc_scr[...] = (cw_ref[0:1, :] * xm1 + cw_ref[1:2, :] * x + cw_ref[2:3, :] * xp1
                   + cw_ref[3:4, :] * xp2 + cb_ref[...])

    row = lax.broadcasted_iota(jnp.int32, (tb, BR_W), 0)
    sp = _softplus(-lam_ref[...])

    def block(base, dr_i, reverse):
        xc = xc_scr[pl.ds(base, tb), :]
        off = dr_i * 2 * BR_W
        gates = _mm(xc.astype(BF16), wg_ref[:, off:off + 2 * BR_W]) + bg_ref[:, off:off + 2 * BR_W]
        r = _sigmoid(gates[:, 0:BR_W])
        ig = _sigmoid(gates[:, BR_W:])
        log_a = -LRU_C * r * sp[dr_i:dr_i + 1, :]
        a = jnp.exp(log_a)
        th = jnp.tanh(log_a)
        b = jnp.sqrt(-2.0 * th / (1.0 - th)) * (ig * xc)
        for d in steps:
            if reverse:
                keep = row < tb - d
                a_s = jnp.where(keep, pltpu.roll(a, tb - d, 0), 1.0)
                b_s = jnp.where(keep, pltpu.roll(b, tb - d, 0), 0.0)
            else:
                keep = row >= d
                a_s = jnp.where(keep, pltpu.roll(a, d, 0), 1.0)
                b_s = jnp.where(keep, pltpu.roll(b, d, 0), 0.0)
            b = b + a * b_s
            a = a * a_s
        h = b + a * car_scr[dr_i:dr_i + 1, :]
        edge = 0 if reverse else tb - 1
        car_scr[dr_i:dr_i + 1, :] = h[edge:edge + 1, :]
        return h

    car_scr[0:2, :] = h0_ref[...]

    def fwd(i, carry):
        base = pl.multiple_of(i * tb, tb)
        hf_scr[pl.ds(base, tb), :] = block(base, 0, False)
        return carry

    lax.fori_loop(0, n_blocks, fwd, 0)

    def bwd(i, carry):
        base = pl.multiple_of((n_blocks - 1 - i) * tb, tb)
        h = block(base, 1, True) + hf_scr[pl.ds(base, tb), :]
        m_ref[pl.ds(base, tb), :] = h * _gelu(z_ref[pl.ds(base, tb), BR_W:2 * BR_W])
        return carry

    lax.fori_loop(0, n_blocks, bwd, 0)
    hfin_ref[...] = car_scr[0:2, :]


def _lru(z, row_off, n_b, t_len, cw, cb, wg, bg, lam, h0):
    full = lambda shape: pl.BlockSpec(shape, lambda b: (0,) * len(shape))
    return pl.pallas_call(
        _lru_kernel,
        grid=(n_b,),
        in_specs=[
            pl.BlockSpec((t_len, 2 * BR_W), lambda b: (row_off + b, Z_LRU // (2 * BR_W))),
            full((4, BR_W)),
            full((1, BR_W)),
            full((BR_W, 4 * BR_W)),
            full((1, 4 * BR_W)),
            full((2, BR_W)),
            pl.BlockSpec((None, 2, BR_W), lambda b: (b, 0, 0)),
        ],
        out_specs=[
            pl.BlockSpec((t_len, BR_W), lambda b: (b, 0)),
            pl.BlockSpec((None, 2, BR_W), lambda b: (b, 0, 0)),
        ],
        out_shape=[
            jax.ShapeDtypeStruct((n_b * t_len, BR_W), F32),
            jax.ShapeDtypeStruct((n_b, 2, BR_W), F32),
        ],
        scratch_shapes=[
            pltpu.VMEM((t_len, BR_W), F32),
            pltpu.VMEM((t_len, BR_W), F32),
            pltpu.VMEM((8, BR_W), F32),
        ],
        compiler_params=_cparams(("parallel",)),
        name="lru",
    )(z, cw, cb, wg, bg, lam, h0)


def _merge_kernel(ma_ref, mb_ref, mc_ref, md_ref, zm_ref, x_ref, g1_ref, sh2_ref, sc2_ref, gain_ref,
                  wb_ref, wo_ref, xo_ref, h2_ref):
    acc = None
    for n, m_ref in enumerate((ma_ref, mb_ref, mc_ref, md_ref)):
        proj = _mm(m_ref[...].astype(BF16), wb_ref[n])
        term = _sigmoid(zm_ref[:, n * D_MODEL:(n + 1) * D_MODEL]) * proj
        acc = term if acc is None else acc + term
    xn = x_ref[...] + g1_ref[...] * _mm(acc.astype(BF16), wo_ref[...])
    xo_ref[...] = xn
    h2_ref[...] = _rms_mod(xn, gain_ref[...], sc2_ref[...], sh2_ref[...]).astype(BF16)


def _merge(ms, z, x, mod3, gain, wb, wo, row_of_tile, tm):
    nt = x.shape[0]
    modspec = lambda k: pl.BlockSpec((None, 1, D_MODEL), lambda i: (row_of_tile(i), 0, k))
    return pl.pallas_call(
        _merge_kernel,
        grid=(nt // tm,),
        in_specs=[pl.BlockSpec((tm, BR_W), lambda i: (i, 0))] * 4 + [
            pl.BlockSpec((tm, 4 * D_MODEL), lambda i: (i, 0)),
            pl.BlockSpec((tm, D_MODEL), lambda i: (i, 0)),
            modspec(2), modspec(3), modspec(4),
            pl.BlockSpec((1, D_MODEL), lambda i: (0, 0)),
            pl.BlockSpec((4, BR_W, D_MODEL), lambda i: (0, 0, 0)),
            pl.BlockSpec((D_MODEL, D_MODEL), lambda i: (0, 0)),
        ],
        out_specs=[
            pl.BlockSpec((tm, D_MODEL), lambda i: (i, 0)),
            pl.BlockSpec((tm, D_MODEL), lambda i: (i, 0)),
        ],
        out_shape=[
            jax.ShapeDtypeStruct((nt, D_MODEL), F32),
            jax.ShapeDtypeStruct((nt, D_MODEL), BF16),
        ],
        compiler_params=_cparams(("parallel",)),
        name="merge",
    )(*ms, z, x, mod3, mod3, mod3, gain, wb, wo)


def _peer_kernel(h2_ref, x_ref, g2_ref, wqt_ref, kbt_ref, u_ref, vt_ref, xo_ref,
                 th_scr, s1_scr, a_scr, b_scr, sc_scr, v_scr, at_scr, wa_scr, yt_scr, *, te):
    tm = h2_ref.shape[0]
    n_lt = tm // LANES
    ic = te // PEER_KEYS
    c_idx = pl.program_id(1)
    nk = PEER_KEYS

    @pl.when(c_idx == 0)
    def _route():
        qt = _nt(wqt_ref[...], h2_ref[...])
        sct = _mm(kbt_ref[...], qt.astype(BF16))
        for lt in range(n_lt):
            sc_scr[lt] = sct[:, lt * LANES:(lt + 1) * LANES]
        r8 = lax.broadcasted_iota(jnp.int32, (8, LANES), 0)

        def per_tile(lt, carry):
            def per_head(h, carry2):
                for s in range(2):
                    off = pl.multiple_of(h * (2 * nk) + s * nk, nk)
                    xs = sc_scr[lt, pl.ds(off, nk), :]
                    for r in range(PEER_TOPK):
                        m = jnp.max(xs, axis=0, keepdims=True)
                        v_scr[s * PEER_TOPK + r:s * PEER_TOPK + r + 1, :] = m
                        xs = jnp.where(xs == m, NEG_INF, xs)
                v0 = v_scr[0:PEER_TOPK, :]
                v1 = v_scr[PEER_TOPK:2 * PEER_TOPK, :]
                pieces = [v0[0:1, :] + v1]
                for r0 in range(1, 8):
                    pieces.append(jnp.where(r8 < PEER_TOPK // (r0 + 1), v0[r0:r0 + 1, :] + v1[0:8, :], NEG_INF))
                pieces.append(v0[8:16, :] + v1[0:1, :])
                cand = jnp.concatenate(pieces, axis=0)
                top = v0[0:1, :] + v1[0:1, :]
                zsum = jnp.zeros_like(top)
                tau = top
                for r in range(PEER_TOPK):
                    tau = jnp.max(cand, axis=0, keepdims=True)
                    zsum = zsum + jnp.exp(tau - top)
                    cand = jnp.where(cand == tau, NEG_INF, cand)
                o0 = pl.multiple_of(h * (2 * nk), nk)
                o1 = pl.multiple_of(h * (2 * nk) + nk, nk)
                ho = pl.multiple_of(h * nk, nk)
                s0 = sc_scr[lt, pl.ds(o0, nk), :]
                s1 = sc_scr[lt, pl.ds(o1, nk), :]
                th = jnp.full((nk, LANES), jnp.inf, F32)
                for r0 in range(PEER_TOPK):
                    ok = (v0[r0:r0 + 1, :] + v1) >= tau
                    thr = jnp.min(jnp.where(ok, v1, jnp.inf), axis=0, keepdims=True)
                    th = jnp.where(s0 == v0[r0:r0 + 1, :], thr, th)
                th_scr[lt, pl.ds(ho, nk), :] = th
                s1_scr[lt, pl.ds(ho, nk), :] = s1
                a_scr[lt, pl.ds(ho, nk), :] = jnp.exp(s0 - v0[0:1, :])
                b_scr[lt, pl.ds(ho, nk), :] = jnp.exp(s1 - v1[0:1, :]) / zsum
                return carry2

            return lax.fori_loop(0, PEER_HEADS, per_head, carry)

        lax.fori_loop(0, n_lt, per_tile, 0)
        yt_scr[...] = jnp.zeros_like(yt_scr)

    at_scr[...] = _nt(u_ref[...], h2_ref[...])
    for lt in range(n_lt):
        for ii in range(ic):
            i_glob = c_idx * ic + ii
            acc = jnp.zeros((nk, LANES), F32)
            for h in range(PEER_HEADS):
                th = th_scr[lt, pl.ds(h * nk + i_glob, 1), :]
                ai = a_scr[lt, pl.ds(h * nk + i_glob, 1), :]
                sel = s1_scr[lt, h * nk:(h + 1) * nk, :] >= th
                acc = acc + jnp.where(sel, b_scr[lt, h * nk:(h + 1) * nk, :], 0.0) * ai
            g = _gelu(at_scr[ii * nk:(ii + 1) * nk, lt * LANES:(lt + 1) * LANES])
            wa_scr[ii * nk:(ii + 1) * nk, lt * LANES:(lt + 1) * LANES] = (acc * g).astype(BF16)
    yt_scr[...] += _mm(vt_ref[...], wa_scr[...])

    @pl.when(c_idx == pl.num_programs(1) - 1)
    def _fin():
        xo_ref[...] = x_ref[...] + g2_ref[...] * yt_scr[...].T


def _peer(h2, x, mod3, wqt, kbt, u, vt, row_of_tile, tm, te):
    nt = x.shape[0]
    n_lt = tm // LANES
    n_rt = PEER_HEADS * PEER_KEYS
    return pl.pallas_call(
        functools.partial(_peer_kernel, te=te),
        grid=(nt // tm, PEER_EXPERTS // te),
        in_specs=[
            pl.BlockSpec((tm, D_MODEL), lambda i, c: (i, 0)),
            pl.BlockSpec((tm, D_MODEL), lambda i, c: (i, 0)),
            pl.BlockSpec((None, 1, D_MODEL), lambda i, c: (row_of_tile(i), 0, 5)),
            pl.BlockSpec((D_MODEL, D_MODEL), lambda i, c: (0, 0)),
            pl.BlockSpec((2 * n_rt, D_MODEL), lambda i, c: (0, 0)),
            pl.BlockSpec((te, D_MODEL), lambda i, c: (c, 0)),
            pl.BlockSpec((D_MODEL, te), lambda i, c: (0, c)),
        ],
        out_specs=pl.BlockSpec((tm, D_MODEL), lambda i, c: (i, 0)),
        out_shape=jax.ShapeDtypeStruct((nt, D_MODEL), F32),
        scratch_shapes=[
            pltpu.VMEM((n_lt, n_rt, LANES), F32),
            pltpu.VMEM((n_lt, n_rt, LANES), F32),
            pltpu.VMEM((n_lt, n_rt, LANES), F32),
            pltpu.VMEM((n_lt, n_rt, LANES), F32),
            pltpu.VMEM((n_lt, 2 * n_rt, LANES), F32),
            pltpu.VMEM((2 * PEER_TOPK, LANES), F32),
            pltpu.VMEM((te, tm), F32),
            pltpu.VMEM((te, tm), BF16),
            pltpu.VMEM((D_MODEL, tm), F32),
        ],
        compiler_params=_cparams(("parallel", "arbitrary")),
        name="peer",
    )(h2, x, mod3, wqt, kbt, u, vt)


def _final_kernel(x_ref, g_ref, o_ref):
    x = x_ref[...]
    o_ref[...] = x * lax.rsqrt(jnp.mean(x * x, axis=-1, keepdims=True) + EPS) * g_ref[...]


def _final_norm(x, gain, tm):
    nt = x.shape[0]
    return pl.pallas_call(
        _final_kernel,
        grid=(nt // tm,),
        in_specs=[pl.BlockSpec((tm, D_MODEL), lambda i: (i, 0)), pl.BlockSpec((1, D_MODEL), lambda i: (0, 0))],
        out_specs=pl.BlockSpec((tm, D_MODEL), lambda i: (i, 0)),
        out_shape=jax.ShapeDtypeStruct((nt, D_MODEL), F32),
        compiler_params=_cparams(("parallel",)),
        name="final_norm",
    )(x, gain)


def _block_diag(blocks):
    n, r, c = blocks.shape
    eye = jnp.eye(n, dtype=blocks.dtype)
    return jnp.einsum('nrc,nm->nrmc', blocks, eye).reshape(n * r, n * c)


def _state_to_bd_t(s):
    b, two, h, k, v = s.shape
    eye = jnp.eye(h, dtype=s.dtype)
    return jnp.einsum('bdhkv,hg->bdhvgk', s, eye).reshape(b, two, h * v, h * k)


def _bd_t_to_state(st, h, k, v):
    b = st.shape[0]
    return jnp.einsum('bdhvhk->bdhkv', st.reshape(b, 2, h, v, h, k))


def _rope_tables(t_len):
    rows = t_len // GRID_W
    row = jnp.repeat(jnp.arange(rows), GRID_W).astype(F32)
    col = jnp.tile(jnp.arange(GRID_W), rows).astype(F32)
    n_freq = RET_DK // 4
    inv_freq = ROPE_BASE ** (-jnp.arange(n_freq, dtype=F32) / n_freq)
    ang = jnp.concatenate([row[:, None] * inv_freq, col[:, None] * inv_freq], axis=-1)
    cos, sin = jnp.cos(ang), jnp.sin(ang)
    cos_h = jnp.concatenate([cos, cos], axis=-1)
    sin_h = jnp.concatenate([-sin, sin], axis=-1)
    return jnp.tile(cos_h, (1, RET_HEADS)), jnp.tile(sin_h, (1, RET_HEADS))


def _swap_matrix():
    lane = jnp.arange(RET_HEADS * RET_DK)
    half = RET_DK // 2
    src = jnp.where(lane % RET_DK < half, lane + half, lane - half)
    return (lane[:, None] == src[None, :]).astype(F32)


def kernel(x_prompt, x_sample, c, state_gla, state_ret, state_s5, state_lru, c_ctx, w_mod, b_mod, norm_mix, norm_ffn, norm_final, w_in, gla_w_decay, gla_b_decay, ret_decay_logit, s5_a_re, s5_a_im, s5_log_dt, s5_b_re, s5_b_im, s5_c_re, s5_c_im, s5_d, s5_w_glu, lru_conv_w, lru_conv_b, lru_w_a, lru_b_a, lru_w_x, lru_b_x, lru_lambda, w_branch, w_out, peer_w_q, peer_keys, peer_u, peer_v):
    n_bp, t_p, _ = x_prompt.shape
    n_bs, t_s, _ = x_sample.shape
    depth = w_in.shape[0]
    ntp, nts = n_bp * t_p, n_bs * t_s
    tm = 256
    assert ntp % t_s == 0 and ntp % tm == 0 and t_s % tm == 0 and t_p % tm == 0

    x = jnp.concatenate([x_prompt.reshape(ntp, D_MODEL), x_sample.reshape(nts, D_MODEL)], axis=0)

    n_rows = 8 * ((1 + n_bs + 7) // 8)
    cond = jnp.zeros((n_rows, D_MODEL), F32).at[0].set(c_ctx).at[1:1 + n_bs].set(c)
    mods = _adaln(cond, w_mod, b_mod)

    def make_row_of_tile(tile):
        def row_of_tile(i):
            return jnp.where(i < ntp // tile, 0, 1 + (i - ntp // tile) // (t_s // tile))
        return row_of_tile

    zpad = lambda n: jnp.zeros((depth, D_MODEL, n), F32)
    w_in_p = jnp.concatenate([w_in[:, :, 2336:6432], w_in[:, :, 0:800], zpad(224), w_in[:, :, 800:1568], zpad(256),
                              w_in[:, :, 1824:2336], w_in[:, :, 1568:1824]], axis=2).astype(BF16)

    hk, hv = GLA_HEADS * GLA_DK, GLA_HEADS * GLA_DV
    e_mat = (jnp.arange(hk)[:, None] // GLA_DK == jnp.arange(hv)[None, :] // GLA_DV).astype(BF16)
    ind = (jnp.arange(hv)[:, None] // GLA_DV == jnp.arange(hv)[None, :] // GLA_DV).astype(F32) / GLA_DV
    swap = _swap_matrix()
    cos_s, sin_s = _rope_tables(t_s)
    cos_p, sin_p = jnp.ones((t_p, hk), F32), jnp.zeros((t_p, hk), F32)

    zeros_bd = jnp.zeros((n_bp, 2, hv, hk), F32)
    zeros_s5 = jnp.zeros((n_bp, 2, 2 * S5_CH), F32)
    zeros_lru = jnp.zeros((n_bp, 2, BR_W), F32)

    gla_l, ret_l, s5_l, lru_l = [], [], [], []
    for l in range(depth):
        mod3 = mods[l].reshape(n_rows, 1, N_MOD * D_MODEL)
        z = _premix(x, mod3, norm_mix[l].reshape(1, D_MODEL), w_in_p[l], make_row_of_tile(tm), tm)

        wd = jnp.zeros((128, 256), F32)
        wd = wd.at[0:GLA_RANK, 0:hk].set(gla_w_decay[l, 0]).at[GLA_RANK:2 * GLA_RANK, hk:].set(gla_w_decay[l, 1])
        bd = gla_b_decay[l].reshape(1, 2 * hk)
        lg = jax.nn.log_sigmoid(ret_decay_logit[l].astype(F32))
        lgl = jnp.repeat(lg, RET_DK, axis=1)
        bre = _block_diag(jnp.swapaxes(s5_b_re[l], 1, 2))
        bim = _block_diag(jnp.swapaxes(s5_b_im[l], 1, 2))
        bbd = jnp.concatenate([bre, bim], axis=1).astype(BF16)
        cre = _block_diag(jnp.swapaxes(s5_c_re[l], 1, 2)).astype(BF16)
        cim = _block_diag(jnp.swapaxes(s5_c_im[l], 1, 2)).astype(BF16)
        coef, pf, pb = _s5_disc(s5_a_re[l].reshape(2, S5_CH), s5_a_im[l].reshape(2, S5_CH),
                                jnp.repeat(s5_log_dt[l], S5_STATE, axis=1), SCAN_TB)
        dsk = s5_d[l].reshape(1, BR_W)
        wglu = s5_w_glu[l].astype(BF16)
        wg = jnp.concatenate([_block_diag(lru_w_a[l, 0]), _block_diag(lru_w_x[l, 0]),
                              _block_diag(lru_w_a[l, 1]), _block_diag(lru_w_x[l, 1])], axis=1).astype(BF16)
        bg = jnp.concatenate([lru_b_a[l, 0], lru_b_x[l, 0], lru_b_a[l, 1], lru_b_x[l, 1]]).reshape(1, 4 * BR_W)
        cw = lru_conv_w[l]
        cb = lru_conv_b[l].reshape(1, BR_W)
        lam = lru_lambda[l]

        s5_h0 = state_s5[:, l].reshape(n_bs, 2, 2 * S5_CH)

        outs = []
        for (row_off, n_b, t_len, sg, sr, ss, sl, cs, sn, rope) in (
                (0, n_bp, t_p, zeros_bd, zeros_bd, zeros_s5, zeros_lru, cos_p, sin_p, False),
                (ntp // t_s, n_bs, t_s, _state_to_bd_t(state_gla[:, l]), _state_to_bd_t(state_ret[:, l]),
                 s5_h0, state_lru[:, l], cos_s, sin_s, True)):
            m_a, f_gla = _gla(z, row_off, n_b, t_len, wd, bd, sg, e_mat, ind)
            m_b, f_ret = _ret(z, row_off, n_b, t_len, cs, sn, swap, lgl, lg, sr, ind, rope)
            m_c, f_s5 = _s5(z, row_off, n_b, t_len, bbd, cre, cim, coef, pf, pb, dsk, wglu, ss)
            m_d, f_lru = _lru(z, row_off, n_b, t_len, cw, cb, wg, bg, lam, sl)
            outs.append(((m_a, m_b, m_c, m_d), (f_gla, f_ret, f_s5, f_lru)))

        ms = [jnp.concatenate([outs[0][0][n], outs[1][0][n]], axis=0) for n in range(4)]
        f_gla, f_ret, f_s5, f_lru = outs[0][1]
        gla_l.append(_bd_t_to_state(f_gla, GLA_HEADS, GLA_DK, GLA_DV))
        ret_l.append(_bd_t_to_state(f_ret, RET_HEADS, RET_DK, RET_DV))
        s5_l.append(f_s5.reshape(n_bp, 2, 2, S5_GROUPS, S5_STATE))
        lru_l.append(f_lru)

        x, h2 = _merge(ms, z, x, mod3, norm_ffn[l].reshape(1, D_MODEL), w_branch[l].astype(BF16),
                       w_out[l].astype(BF16), make_row_of_tile(tm), tm)

        tm_p, te = 512, 512
        wqt = peer_w_q[l].T.astype(BF16)
        kbt = _block_diag(peer_keys[l].reshape(2 * PEER_HEADS, PEER_KEYS, PEER_QDIM // 2)).astype(BF16)
        x = _peer(h2, x, mod3, wqt, kbt, peer_u[l].astype(BF16), peer_v[l].T.astype(BF16),
                  make_row_of_tile(tm_p), tm_p, te)

    y = _final_norm(x, norm_final.reshape(1, D_MODEL), tm)
    y_p = y[:ntp].reshape(n_bp, t_p, D_MODEL)
    y_s = y[ntp:].reshape(n_bs, t_s, D_MODEL)
    return (y_p, y_s, jnp.stack(gla_l, axis=1), jnp.stack(ret_l, axis=1),
            jnp.stack(s5_l, axis=1), jnp.stack(lru_l, axis=1))
```

```python
import functools
import math

import jax
import jax.numpy as jnp
from jax import lax
from jax.experimental import pallas as pl
from jax.experimental.pallas import tpu as pltpu

F32 = jnp.float32
BF16 = jnp.bfloat16
HI = lax.Precision.HIGHEST

D_MODEL = 1024
N_MOD = 6
EPS = 1e-6
BR_W = 256
GLA_HEADS, GLA_DK, GLA_DV, GLA_RANK, GLA_TAU, GLA_CHUNK = 4, 32, 64, 16, 16.0, 32
RET_HEADS, RET_DK, RET_DV, RET_CHUNK = 4, 32, 64, 64
ROPE_BASE = 10000.0
GRID_W = 64
S5_GROUP, S5_GROUPS, S5_STATE, S5_RE_MAX = 16, 16, 64, -1e-4
S5_CH = S5_GROUPS * S5_STATE
LRU_BLOCKS, LRU_BW, LRU_C = 4, 64, 8.0
PEER_HEADS, PEER_KEYS, PEER_TOPK, PEER_QDIM = 8, 128, 16, 128
PEER_EXPERTS = PEER_KEYS * PEER_KEYS

LANES = 128
VMEM_LIMIT = 56 * 1024 * 1024

Z_MERGE, Z_GLA, Z_RET, Z_LRU, Z_S5 = 0, 4096, 5120, 6144, 6656
Z_W = 6912

SCAN_TB = 128
TOK_TM = 256
PRE_TM = 512
PEER_TM, PEER_TE, PEER_TS = 512, 2048, 512
NEG_INF = float("-inf")


def _cparams(sem):
    return pltpu.CompilerParams(dimension_semantics=sem, vmem_limit_bytes=VMEM_LIMIT)


def _nt(a, b):
    return lax.dot_general(a, b, (((1,), (1,)), ((), ())), preferred_element_type=F32)


def _tn(a, b):
    return lax.dot_general(a, b, (((0,), (0,)), ((), ())), preferred_element_type=F32)


def _mm(a, b):
    return jnp.dot(a, b, preferred_element_type=F32)


def _mm_hi(a, b):
    return jnp.dot(a, b, preferred_element_type=F32, precision=HI)


def _sigmoid(x):
    return jax.nn.sigmoid(x)


def _silu(x):
    return x * jax.nn.sigmoid(x)


def _gelu(x):
    return jax.nn.gelu(x)


GELU_C0 = math.sqrt(2.0 / math.pi)
GELU_C1 = GELU_C0 * 0.044715
BF16_ROWS = 16


def _gelu_bf16(x):
    hx = 0.5 * x
    return hx + hx * jnp.tanh(x * (GELU_C0 + GELU_C1 * (x * x)))


def _row_tile_bf16(row, n_rows):
    one = jnp.broadcast_to(row, (BF16_ROWS, row.shape[1])).astype(BF16)
    return jnp.concatenate([one] * (n_rows // BF16_ROWS), axis=0)


def _log_sigmoid(x):
    return jnp.minimum(x, 0.0) - jnp.log(1.0 + jnp.exp(-jnp.abs(x)))


def _softplus(x):
    return jnp.maximum(x, 0.0) + jnp.log(1.0 + jnp.exp(-jnp.abs(x)))


def _rms_mod(x, gain, sc, sh):
    ms = jnp.mean(x * x, axis=-1, keepdims=True)
    return x * lax.rsqrt(ms + EPS) * gain * (1.0 + sc) + sh


def _adaln_kernel(c_ref, w_ref, b_ref, o_ref):
    o_ref[...] = _mm_hi(_silu(c_ref[...]), w_ref[...]) + b_ref[...]


def _adaln(cond, w_mod, b_mod):
    n_l = w_mod.shape[0]
    rows = cond.shape[0]
    tn = 1536
    return pl.pallas_call(
        _adaln_kernel,
        grid=(n_l, N_MOD * D_MODEL // tn),
        in_specs=[
            pl.BlockSpec((rows, D_MODEL), lambda l, j: (0, 0)),
            pl.BlockSpec((None, D_MODEL, tn), lambda l, j: (l, 0, j)),
            pl.BlockSpec((None, 1, tn), lambda l, j: (l, 0, j)),
        ],
        out_specs=pl.BlockSpec((None, rows, tn), lambda l, j: (l, 0, j)),
        out_shape=jax.ShapeDtypeStruct((n_l, rows, N_MOD * D_MODEL), F32),
        compiler_params=_cparams(("parallel", "parallel")),
        name="adaln",
    )(cond, w_mod, b_mod.reshape(n_l, 1, N_MOD * D_MODEL))


def _premix_kernel(x_ref, sh_ref, sc_ref, g_ref, w_ref, o_ref, h_scr):
    @pl.when(pl.program_id(1) == 0)
    def _():
        h_scr[...] = _rms_mod(x_ref[...], g_ref[...], sc_ref[...], sh_ref[...]).astype(BF16)

    o_ref[...] = _mm(h_scr[...], w_ref[...])


def _premix(x, mod3, gain, w, row_of_tile, tm):
    nt = x.shape[0]
    tn = 1152
    return pl.pallas_call(
        _premix_kernel,
        grid=(nt // tm, Z_W // tn),
        in_specs=[
            pl.BlockSpec((tm, D_MODEL), lambda i, j: (i, 0)),
            pl.BlockSpec((None, 1, D_MODEL), lambda i, j: (row_of_tile(i), 0, 0)),
            pl.BlockSpec((None, 1, D_MODEL), lambda i, j: (row_of_tile(i), 0, 1)),
            pl.BlockSpec((1, D_MODEL), lambda i, j: (0, 0)),
            pl.BlockSpec((D_MODEL, tn), lambda i, j: (0, j)),
        ],
        out_specs=pl.BlockSpec((tm, tn), lambda i, j: (i, j)),
        out_shape=jax.ShapeDtypeStruct((nt, Z_W), F32),
        scratch_shapes=[pltpu.VMEM((tm, D_MODEL), BF16)],
        compiler_params=_cparams(("parallel", "arbitrary")),
        name="premix",
    )(x, mod3, mod3, gain, w)


def _gla_kernel(z_ref, wd_ref, bd_ref, s0_ref, e_ref, ind_ref, m_ref, sfin_ref,
                la_scr, of_scr, st_scr, p_scr, cum_scr, k_scr, v_scr):
    t_len = z_ref.shape[0]
    c = GLA_CHUNK
    n_chunks = t_len // c
    hk = GLA_HEADS * GLA_DK
    hv = GLA_HEADS * GLA_DV
    scale = GLA_DK ** -0.5

    pre = _mm_hi(z_ref[:, 768:896], wd_ref[...]) + bd_ref[...]
    la_scr[...] = _log_sigmoid(pre) * (1.0 / GLA_TAU)

    ri = lax.broadcasted_iota(jnp.int32, (c, c), 0)
    ci = lax.broadcasted_iota(jnp.int32, (c, c), 1)
    tri_lo = (ri >= ci).astype(F32)
    tri_up = (ri <= ci).astype(F32)
    row = lax.broadcasted_iota(jnp.int32, (c, hk), 0)
    bd_mask = (lax.broadcasted_iota(jnp.int32, (hv, hk), 0) // GLA_DV
               == lax.broadcasted_iota(jnp.int32, (hv, hk), 1) // GLA_DK).astype(F32)

    def chunk(base, la, tri, reverse):
        cum = _mm_hi(tri, la)
        q = z_ref[pl.ds(base, c), 0:128] * scale
        k = z_ref[pl.ds(base, c), 128:256]
        v = z_ref[pl.ds(base, c), 256:512]
        edge = cum[0:1, :] if reverse else cum[c - 1:c, :]
        st = st_scr[...]
        o = _nt((q * jnp.exp(cum)).astype(BF16), st.astype(BF16))
        ke = k * jnp.exp(edge - cum)
        cum_scr[...] = cum
        k_scr[...] = k
        v_scr[...] = v
        for j in range(c):
            d = jnp.minimum(cum - cum_scr[j:j + 1, :], 0.0)
            p = q * k_scr[j:j + 1, :] * jnp.exp(d)
            keep = (row <= j) if reverse else (row >= j)
            p_scr[j * c:(j + 1) * c, :] = jnp.where(keep, p, 0.0).astype(BF16)
        pe = _mm(p_scr[...], e_ref[...])
        for j in range(c):
            o = o + pe[j * c:(j + 1) * c, :] * v_scr[j:j + 1, :]
        st_scr[...] = st * jnp.exp(edge) + bd_mask * _tn(v.astype(BF16), ke.astype(BF16))
        return o

    st_scr[...] = s0_ref[0]

    def fwd(i, carry):
        base = pl.multiple_of(i * c, c)
        of_scr[pl.ds(base, c), :] = chunk(base, la_scr[pl.ds(base, c), 0:128], tri_lo, False)
        return carry

    lax.fori_loop(0, n_chunks, fwd, 0)
    sfin_ref[0] = st_scr[...]
    st_scr[...] = s0_ref[1]

    def bwd(i, carry):
        base = pl.multiple_of((n_chunks - 1 - i) * c, c)
        o = of_scr[pl.ds(base, c), :] + chunk(base, la_scr[pl.ds(base, c), 128:256], tri_up, True)
        ms = _mm_hi(o * o, ind_ref[...])
        g = z_ref[pl.ds(base, c), 512:768]
        m_ref[pl.ds(base, c), :] = o * lax.rsqrt(ms + EPS) * _silu(g)
        return carry

    lax.fori_loop(0, n_chunks, bwd, 0)
    sfin_ref[1] = st_scr[...]


def _gla(z, row_off, n_b, t_len, wd, bd, s0t, e_mat, ind):
    hk, hv = GLA_HEADS * GLA_DK, GLA_HEADS * GLA_DV
    c = GLA_CHUNK
    return pl.pallas_call(
        _gla_kernel,
        grid=(n_b,),
        in_specs=[
            pl.BlockSpec((t_len, 1024), lambda b: (row_off + b, Z_GLA // 1024)),
            pl.BlockSpec((128, 256), lambda b: (0, 0)),
            pl.BlockSpec((1, 256), lambda b: (0, 0)),
            pl.BlockSpec((None, 2, hv, hk), lambda b: (b, 0, 0, 0)),
            pl.BlockSpec((hk, hv), lambda b: (0, 0)),
            pl.BlockSpec((hv, hv), lambda b: (0, 0)),
        ],
        out_specs=[
            pl.BlockSpec((t_len, BR_W), lambda b: (b, 0)),
            pl.BlockSpec((None, 2, hv, hk), lambda b: (b, 0, 0, 0)),
        ],
        out_shape=[
            jax.ShapeDtypeStruct((n_b * t_len, BR_W), F32),
            jax.ShapeDtypeStruct((n_b, 2, hv, hk), F32),
        ],
        scratch_shapes=[
            pltpu.VMEM((t_len, 256), F32),
            pltpu.VMEM((t_len, hv), F32),
            pltpu.VMEM((hv, hk), F32),
            pltpu.VMEM((c * c, hk), BF16),
            pltpu.VMEM((c, hk), F32),
            pltpu.VMEM((c, hk), F32),
            pltpu.VMEM((c, hv), F32),
        ],
        compiler_params=_cparams(("parallel",)),
        name="gla",
    )(z, wd, bd, s0t, e_mat, ind)


def _ret_kernel(z_ref, cos_ref, sin_ref, swap_ref, lgl_ref, lgs_ref, s0_ref, ind_ref, m_ref, sfin_ref,
                qk_scr, of_scr, st_scr, *, rope):
    t_len = z_ref.shape[0]
    c = RET_CHUNK
    n_chunks = t_len // c
    hk = RET_HEADS * RET_DK
    hv = RET_HEADS * RET_DV
    scale = RET_DK ** -0.5

    lgf = lgl_ref[0:1, :]
    lgb = lgl_ref[1:2, :]
    pos = lax.broadcasted_iota(jnp.int32, (c, hk), 0).astype(F32)
    wq_f = jnp.exp(lgf * (pos + 1.0))
    wk_f = jnp.exp(lgf * (c - 1.0 - pos))
    wq_b = jnp.exp(lgb * (c - pos))
    wk_b = jnp.exp(lgb * pos)
    dec_f = jnp.exp(lgf * float(c))
    dec_b = jnp.exp(lgb * float(c))

    ii = lax.broadcasted_iota(jnp.int32, (c, c), 0)
    jj = lax.broadcasted_iota(jnp.int32, (c, c), 1)
    rel = (ii - jj).astype(F32)
    dms = []
    for h in range(RET_HEADS):
        d_f = jnp.where(ii >= jj, jnp.exp(lgs_ref[0, h] * jnp.maximum(rel, 0.0)), 0.0)
        d_b = jnp.where(jj >= ii, jnp.exp(lgs_ref[1, h] * jnp.maximum(-rel, 0.0)), 0.0)
        dms.append(d_f + d_b)
    dmat = jnp.concatenate(dms, axis=1)

    ek_mask = (lax.broadcasted_iota(jnp.int32, (RET_HEADS * c, hk), 0) // c
               == lax.broadcasted_iota(jnp.int32, (RET_HEADS * c, hk), 1) // RET_DK).astype(F32)
    ev_mask = (lax.broadcasted_iota(jnp.int32, (RET_HEADS * c, hv), 0) // c
               == lax.broadcasted_iota(jnp.int32, (RET_HEADS * c, hv), 1) // RET_DV).astype(F32)
    bd_mask = (lax.broadcasted_iota(jnp.int32, (hv, hk), 0) // RET_DV
               == lax.broadcasted_iota(jnp.int32, (hv, hk), 1) // RET_DK).astype(F32)

    st_scr[...] = s0_ref[0]

    def fwd(i, carry):
        base = pl.multiple_of(i * c, c)
        q = z_ref[pl.ds(base, c), 0:128] * scale
        k = z_ref[pl.ds(base, c), 128:256]
        v = z_ref[pl.ds(base, c), 256:512]
        if rope:
            cs = cos_ref[pl.ds(base, c), :]
            sn = sin_ref[pl.ds(base, c), :]
            q = q * cs + _mm_hi(q, swap_ref[...]) * sn
            k = k * cs + _mm_hi(k, swap_ref[...]) * sn
        qk_scr[pl.ds(base, c), 0:128] = q
        qk_scr[pl.ds(base, c), 128:256] = k
        kexp = (jnp.concatenate([k] * RET_HEADS, axis=0) * ek_mask).astype(BF16)
        vexp = (jnp.concatenate([v] * RET_HEADS, axis=0) * ev_mask).astype(BF16)
        sc = _nt(q.astype(BF16), kexp) * dmat
        o = _mm(sc.astype(BF16), vexp)
        st = st_scr[...]
        o = o + _nt((q * wq_f).astype(BF16), st.astype(BF16))
        of_scr[pl.ds(base, c), :] = o
        st_scr[...] = st * dec_f + bd_mask * _tn(v.astype(BF16), (k * wk_f).astype(BF16))
        return carry

    lax.fori_loop(0, n_chunks, fwd, 0)
    sfin_ref[0] = st_scr[...]
    st_scr[...] = s0_ref[1]

    def bwd(i, carry):
        base = pl.multiple_of((n_chunks - 1 - i) * c, c)
        q = qk_scr[pl.ds(base, c), 0:128]
        k = qk_scr[pl.ds(base, c), 128:256]
        v = z_ref[pl.ds(base, c), 256:512]
        st = st_scr[...]
        o = of_scr[pl.ds(base, c), :] + _nt((q * wq_b).astype(BF16), st.astype(BF16))
        st_scr[...] = st * dec_b + bd_mask * _tn(v.astype(BF16), (k * wk_b).astype(BF16))
        ms = _mm_hi(o * o, ind_ref[...])
        g = z_ref[pl.ds(base, c), 512:768]
        m_ref[pl.ds(base, c), :] = o * lax.rsqrt(ms + EPS) * _silu(g)
        return carry

    lax.fori_loop(0, n_chunks, bwd, 0)
    sfin_ref[1] = st_scr[...]


def _ret(z, row_off, n_b, t_len, cos_t, sin_t, swap, lgl, lgs, s0t, ind, rope):
    hk, hv = RET_HEADS * RET_DK, RET_HEADS * RET_DV
    return pl.pallas_call(
        functools.partial(_ret_kernel, rope=rope),
        grid=(n_b,),
        in_specs=[
            pl.BlockSpec((t_len, 1024), lambda b: (row_off + b, Z_RET // 1024)),
            pl.BlockSpec((t_len, hk), lambda b: (0, 0)),
            pl.BlockSpec((t_len, hk), lambda b: (0, 0)),
            pl.BlockSpec((hk, hk), lambda b: (0, 0)),
            pl.BlockSpec((2, hk), lambda b: (0, 0)),
            pl.BlockSpec(memory_space=pltpu.SMEM),
            pl.BlockSpec((None, 2, hv, hk), lambda b: (b, 0, 0, 0)),
            pl.BlockSpec((hv, hv), lambda b: (0, 0)),
        ],
        out_specs=[
            pl.BlockSpec((t_len, BR_W), lambda b: (b, 0)),
            pl.BlockSpec((None, 2, hv, hk), lambda b: (b, 0, 0, 0)),
        ],
        out_shape=[
            jax.ShapeDtypeStruct((n_b * t_len, BR_W), F32),
            jax.ShapeDtypeStruct((n_b, 2, hv, hk), F32),
        ],
        scratch_shapes=[
            pltpu.VMEM((t_len, 2 * hk), F32),
            pltpu.VMEM((t_len, hv), F32),
            pltpu.VMEM((hv, hk), F32),
        ],
        compiler_params=_cparams(("parallel",)),
        name="ret",
    )(z, cos_t, sin_t, swap, lgl, lgs, s0t, ind)


def _s5_disc_kernel(are_ref, aim_ref, ldt_ref, coef_ref, pf_ref, pb_ref):
    tb = pf_ref.shape[0]
    re = jnp.minimum(are_ref[...], S5_RE_MAX)
    im = aim_ref[...]
    dt = jnp.exp(ldt_ref[...])
    er = jnp.exp(re * dt)
    lbr = er * jnp.cos(im * dt)
    lbi = er * jnp.sin(im * dt)
    den = re * re + im * im
    nr = lbr - 1.0
    coef_ref[:, 0:S5_CH] = (nr * re + lbi * im) / den
    coef_ref[:, S5_CH:] = (lbi * re - nr * im) / den
    t = lax.broadcasted_iota(jnp.int32, (tb, S5_CH), 0).astype(F32)
    nf = t + 1.0
    nb = float(tb) - t
    mf = jnp.exp(nf * (re[0:1] * dt[0:1]))
    pf_ref[:, 0:S5_CH] = mf * jnp.cos(nf * (im[0:1] * dt[0:1]))
    pf_ref[:, S5_CH:] = mf * jnp.sin(nf * (im[0:1] * dt[0:1]))
    mb = jnp.exp(nb * (re[1:2] * dt[1:2]))
    pb_ref[:, 0:S5_CH] = mb * jnp.cos(nb * (im[1:2] * dt[1:2]))
    pb_ref[:, S5_CH:] = mb * jnp.sin(nb * (im[1:2] * dt[1:2]))


def _s5_disc(are, aim, ldt, tb):
    return pl.pallas_call(
        _s5_disc_kernel,
        out_shape=[
            jax.ShapeDtypeStruct((2, 2 * S5_CH), F32),
            jax.ShapeDtypeStruct((tb, 2 * S5_CH), F32),
            jax.ShapeDtypeStruct((tb, 2 * S5_CH), F32),
        ],
        compiler_params=pltpu.CompilerParams(vmem_limit_bytes=VMEM_LIMIT),
        name="s5_disc",
    )(are, aim, ldt)


def _s5_kernel(u_ref, bbd_ref, cre_ref, cim_ref, coef_ref, pf_ref, pb_ref, dsk_ref, wglu_ref, h0_ref,
               m_ref, hfin_ref, y_scr, bu_scr, hr_scr, hi_scr, car_scr):
    t_len = u_ref.shape[0]
    tb = pf_ref.shape[0]
    n_blocks = t_len // tb
    row = lax.broadcasted_iota(jnp.int32, (tb, LANES), 0)
    steps = [1 << s for s in range(int(math.log2(tb)))]

    def block(base, dr_i, reverse):
        p_ref = pb_ref if reverse else pf_ref
        u = u_ref[pl.ds(base, tb), :]
        bu_scr[...] = _mm(u.astype(BF16), bbd_ref[...])
        for g in range(S5_CH // LANES):
            lo, hi = g * LANES, (g + 1) * LANES
            br = bu_scr[:, lo:hi]
            bi = bu_scr[:, S5_CH + lo:S5_CH + hi]
            cr = coef_ref[dr_i:dr_i + 1, lo:hi]
            ci = coef_ref[dr_i:dr_i + 1, S5_CH + lo:S5_CH + hi]
            hr = cr * br - ci * bi
            hi_ = cr * bi + ci * br
            for d in steps:
                if reverse:
                    pr = p_ref[tb - d:tb - d + 1, lo:hi]
                    pi = p_ref[tb - d:tb - d + 1, S5_CH + lo:S5_CH + hi]
                    keep = row < tb - d
                    sr = jnp.where(keep, pltpu.roll(hr, tb - d, 0), 0.0)
                    si = jnp.where(keep, pltpu.roll(hi_, tb - d, 0), 0.0)
                else:
                    pr = p_ref[d - 1:d, lo:hi]
                    pi = p_ref[d - 1:d, S5_CH + lo:S5_CH + hi]
                    keep = row >= d
                    sr = jnp.where(keep, pltpu.roll(hr, d, 0), 0.0)
                    si = jnp.where(keep, pltpu.roll(hi_, d, 0), 0.0)
                hr, hi_ = hr + pr * sr - pi * si, hi_ + pr * si + pi * sr
            car = car_scr[0:1, lo:hi]
            cai = car_scr[0:1, S5_CH + lo:S5_CH + hi]
            pwr = p_ref[:, lo:hi]
            pwi = p_ref[:, S5_CH + lo:S5_CH + hi]
            hr, hi_ = hr + pwr * car - pwi * cai, hi_ + pwr * cai + pwi * car
            hr_scr[:, lo:hi] = hr
            hi_scr[:, lo:hi] = hi_
            edge = 0 if reverse else tb - 1
            car_scr[0:1, lo:hi] = hr[edge:edge + 1, :]
            car_scr[0:1, S5_CH + lo:S5_CH + hi] = hi_[edge:edge + 1, :]
        y = _mm(hr_scr[...].astype(BF16), cre_ref[...]) - _mm(hi_scr[...].astype(BF16), cim_ref[...])
        return u, y

    car_scr[0:1, :] = h0_ref[0:1, :]

    def fwd(i, carry):
        base = pl.multiple_of(i * tb, tb)
        _, y = block(base, 0, False)
        y_scr[pl.ds(base, tb), :] = y
        return carry

    lax.fori_loop(0, n_blocks, fwd, 0)
    hfin_ref[0:1, :] = car_scr[0:1, :]
    car_scr[0:1, :] = h0_ref[1:2, :]

    def bwd(i, carry):
        base = pl.multiple_of((n_blocks - 1 - i) * tb, tb)
        u, y = block(base, 1, True)
        y = _gelu(y_scr[pl.ds(base, tb), :] + y + dsk_ref[...] * u)
        gg = _mm(y.astype(BF16), wglu_ref[...])
        m_ref[pl.ds(base, tb), :] = gg[:, 0:BR_W] * _sigmoid(gg[:, BR_W:])
        return carry

    lax.fori_loop(0, n_blocks, bwd, 0)
    hfin_ref[1:2, :] = car_scr[0:1, :]


def _s5(z, row_off, n_b, t_len, bbd, cre, cim, coef, pf, pb, dsk, wglu, h0):
    tb = pf.shape[0]
    full = lambda shape: pl.BlockSpec(shape, lambda b: (0,) * len(shape))
    return pl.pallas_call(
        _s5_kernel,
        grid=(n_b,),
        in_specs=[
            pl.BlockSpec((t_len, BR_W), lambda b: (row_off + b, Z_S5 // BR_W)),
            full((BR_W, 2 * S5_CH)),
            full((S5_CH, BR_W)),
            full((S5_CH, BR_W)),
            full((2, 2 * S5_CH)),
            full((tb, 2 * S5_CH)),
            full((tb, 2 * S5_CH)),
            full((1, BR_W)),
            full((BR_W, 2 * BR_W)),
            pl.BlockSpec((None, 2, 2 * S5_CH), lambda b: (b, 0, 0)),
        ],
        out_specs=[
            pl.BlockSpec((t_len, BR_W), lambda b: (b, 0)),
            pl.BlockSpec((None, 2, 2 * S5_CH), lambda b: (b, 0, 0)),
        ],
        out_shape=[
            jax.ShapeDtypeStruct((n_b * t_len, BR_W), F32),
            jax.ShapeDtypeStruct((n_b, 2, 2 * S5_CH), F32),
        ],
        scratch_shapes=[
            pltpu.VMEM((t_len, BR_W), F32),
            pltpu.VMEM((tb, 2 * S5_CH), F32),
            pltpu.VMEM((tb, S5_CH), F32),
            pltpu.VMEM((tb, S5_CH), F32),
            pltpu.VMEM((8, 2 * S5_CH), F32),
        ],
        compiler_params=_cparams(("parallel",)),
        name="s5",
    )(z, bbd, cre, cim, coef, pf, pb, dsk, wglu, h0)


def _lru_kernel(z_ref, cw_ref, cb_ref, wg_ref, bg_ref, lam_ref, h0_ref, m_ref, hfin_ref,
                xc_scr, hf_scr, car_scr):
    t_len = z_ref.shape[0]
    tb = SCAN_TB
    n_blocks = t_len // tb
    steps = [1 << s for s in range(int(math.log2(tb)))]

    x = z_ref[:, 0:BR_W]
    trow = lax.broadcasted_iota(jnp.int32, (t_len, BR_W), 0)
    xm1 = jnp.where(trow >= 1, pltpu.roll(x, 1, 0), 0.0)
    xp1 = jnp.where(trow < t_len - 1, pltpu.roll(x, t_len - 1, 0), 0.0)
    xp2 = jnp.where(trow < t_len - 2, pltpu.roll(x, t_len - 2, 0), 0.0)
    xc_scr[...] = (cw_ref[0:1, :] * xm1 + cw_ref[1:2, :] * x + cw_ref[2:3, :] * xp1
                   + cw_ref[3:4, :] * xp2 + cb_ref[...])

    row = lax.broadcasted_iota(jnp.int32, (tb, BR_W), 0)
    sp = _softplus(-lam_ref[...])

    def block(base, dr_i, reverse):
        xc = xc_scr[pl.ds(base, tb), :]
        off = dr_i * 2 * BR_W
        gates = _mm(xc.astype(BF16), wg_ref[:, off:off + 2 * BR_W]) + bg_ref[:, off:off + 2 * BR_W]
        r = _sigmoid(gates[:, 0:BR_W])
        ig = _sigmoid(gates[:, BR_W:])
        log_a = -LRU_C * r * sp[dr_i:dr_i + 1, :]
        a = jnp.exp(log_a)
        th = jnp.tanh(log_a)
        b = jnp.sqrt(-2.0 * th / (1.0 - th)) * (ig * xc)
        for d in steps:
            if reverse:
                keep = row < tb - d
                a_s = jnp.where(keep, pltpu.roll(a, tb - d, 0), 1.0)
                b_s = jnp.where(keep, pltpu.roll(b, tb - d, 0), 0.0)
            else:
                keep = row >= d
                a_s = jnp.where(keep, pltpu.roll(a, d, 0), 1.0)
                b_s = jnp.where(keep, pltpu.roll(b, d, 0), 0.0)
            b = b + a * b_s
            a = a * a_s
        h = b + a * car_scr[dr_i:dr_i + 1, :]
        edge = 0 if reverse else tb - 1
        car_scr[dr_i:dr_i + 1, :] = h[edge:edge + 1, :]
        return h

    car_scr[0:2, :] = h0_ref[...]

    def fwd(i, carry):
        base = pl.multiple_of(i * tb, tb)
        hf_scr[pl.ds(base, tb), :] = block(base, 0, False)
        return carry

    lax.fori_loop(0, n_blocks, fwd, 0)

    def bwd(i, carry):
        base = pl.multiple_of((n_blocks - 1 - i) * tb, tb)
        h = block(base, 1, True) + hf_scr[pl.ds(base, tb), :]
        m_ref[pl.ds(base, tb), :] = h * _gelu(z_ref[pl.ds(base, tb), BR_W:2 * BR_W])
        return carry

    lax.fori_loop(0, n_blocks, bwd, 0)
    hfin_ref[...] = car_scr[0:2, :]


def _lru(z, row_off, n_b, t_len, cw, cb, wg, bg, lam, h0):
    full = lambda shape: pl.BlockSpec(shape, lambda b: (0,) * len(shape))
    return pl.pallas_call(
        _lru_kernel,
        grid=(n_b,),
        in_specs=[
            pl.BlockSpec((t_len, 2 * BR_W), lambda b: (row_off + b, Z_LRU // (2 * BR_W))),
            full((4, BR_W)),
            full((1, BR_W)),
            full((BR_W, 4 * BR_W)),
            full((1, 4 * BR_W)),
            full((2, BR_W)),
            pl.BlockSpec((None, 2, BR_W), lambda b: (b, 0, 0)),
        ],
        out_specs=[
            pl.BlockSpec((t_len, BR_W), lambda b: (b, 0)),
            pl.BlockSpec((None, 2, BR_W), lambda b: (b, 0, 0)),
        ],
        out_shape=[
            jax.ShapeDtypeStruct((n_b * t_len, BR_W), F32),
            jax.ShapeDtypeStruct((n_b, 2, BR_W), F32),
        ],
        scratch_shapes=[
            pltpu.VMEM((t_len, BR_W), F32),
            pltpu.VMEM((t_len, BR_W), F32),
            pltpu.VMEM((8, BR_W), F32),
        ],
        compiler_params=_cparams(("parallel",)),
        name="lru",
    )(z, cw, cb, wg, bg, lam, h0)


def _merge_kernel(ma_ref, mb_ref, mc_ref, md_ref, zm_ref, x_ref, g1_ref, sh2_ref, sc2_ref, gain_ref,
                  wb_ref, wo_ref, xo_ref, h2t_ref):
    acc = None
    for n, m_ref in enumerate((ma_ref, mb_ref, mc_ref, md_ref)):
        proj = _mm(m_ref[...].astype(BF16), wb_ref[n])
        term = _sigmoid(zm_ref[:, n * D_MODEL:(n + 1) * D_MODEL]) * proj
        acc = term if acc is None else acc + term
    xn = x_ref[...] + g1_ref[...] * _mm(acc.astype(BF16), wo_ref[...])
    xo_ref[...] = xn
    h2t_ref[...] = _rms_mod(xn, gain_ref[...], sc2_ref[...], sh2_ref[...]).T.astype(BF16)


def _merge(ms, z, x, mod3, gain, wb, wo, row_of_tile, tm):
    nt = x.shape[0]
    modspec = lambda k: pl.BlockSpec((None, 1, D_MODEL), lambda i: (row_of_tile(i), 0, k))
    return pl.pallas_call(
        _merge_kernel,
        grid=(nt // tm,),
        in_specs=[pl.BlockSpec((tm, BR_W), lambda i: (i, 0))] * 4 + [
            pl.BlockSpec((tm, 4 * D_MODEL), lambda i: (i, 0)),
            pl.BlockSpec((tm, D_MODEL), lambda i: (i, 0)),
            modspec(2), modspec(3), modspec(4),
            pl.BlockSpec((1, D_MODEL), lambda i: (0, 0)),
            pl.BlockSpec((4, BR_W, D_MODEL), lambda i: (0, 0, 0)),
            pl.BlockSpec((D_MODEL, D_MODEL), lambda i: (0, 0)),
        ],
        out_specs=[
            pl.BlockSpec((tm, D_MODEL), lambda i: (i, 0)),
            pl.BlockSpec((D_MODEL, tm), lambda i: (0, i)),
        ],
        out_shape=[
            jax.ShapeDtypeStruct((nt, D_MODEL), F32),
            jax.ShapeDtypeStruct((D_MODEL, nt), BF16),
        ],
        compiler_params=_cparams(("parallel",)),
        name="merge",
    )(*ms, z, x, mod3, mod3, mod3, gain, wb, wo)


def _peer_kernel(h2t_ref, x_ref, g2_ref, wqt_ref, kbt_ref, u_ref, vt_ref, xo_ref,
                 n_scr, r1_scr, a_scr, b_scr, sc_scr, v_scr, at_scr, wa_scr, yt_scr, *, te, ts):
    tm = h2t_ref.shape[1]
    n_lt = tm // LANES
    ic = te // PEER_KEYS
    n_sub = te // ts
    c_idx = pl.program_id(1)
    nk = PEER_KEYS

    @pl.when(c_idx == 0)
    def _route():
        qt = _mm(wqt_ref[...], h2t_ref[...])
        sct = _mm(kbt_ref[...], qt.astype(BF16))
        for lt in range(n_lt):
            sc_scr[lt] = sct[:, lt * LANES:(lt + 1) * LANES]
        r8 = lax.broadcasted_iota(jnp.int32, (8, LANES), 0)

        def per_tile(lt, carry):
            def per_head(h, carry2):
                rank1 = jnp.full((nk, LANES), float(PEER_TOPK), F32)
                for s in range(2):
                    off = pl.multiple_of(h * (2 * nk) + s * nk, nk)
                    xs = sc_scr[lt, pl.ds(off, nk), :]
                    for r in range(PEER_TOPK):
                        m = jnp.max(xs, axis=0, keepdims=True)
                        v_scr[s * PEER_TOPK + r:s * PEER_TOPK + r + 1, :] = m
                        hit = xs == m
                        if s == 1:
                            rank1 = jnp.where(hit, float(r), rank1)
                        xs = jnp.where(hit, NEG_INF, xs)
                v0 = v_scr[0:PEER_TOPK, :]
                v1 = v_scr[PEER_TOPK:2 * PEER_TOPK, :]
                pieces = [v0[0:1, :] + v1]
                for r0 in range(1, 8):
                    pieces.append(jnp.where(r8 < PEER_TOPK // (r0 + 1), v0[r0:r0 + 1, :] + v1[0:8, :], NEG_INF))
                pieces.append(v0[8:16, :] + v1[0:1, :])
                cand = jnp.concatenate(pieces, axis=0)
                top = v0[0:1, :] + v1[0:1, :]
                zsum = jnp.zeros_like(top)
                tau = top
                for r in range(PEER_TOPK):
                    tau = jnp.max(cand, axis=0, keepdims=True)
                    zsum = zsum + jnp.exp(tau - top)
                    cand = jnp.where(cand == tau, NEG_INF, cand)
                o0 = pl.multiple_of(h * (2 * nk), nk)
                o1 = pl.multiple_of(h * (2 * nk) + nk, nk)
                ho = pl.multiple_of(h * nk, nk)
                s0 = sc_scr[lt, pl.ds(o0, nk), :]
                s1 = sc_scr[lt, pl.ds(o1, nk), :]
                cnt = jnp.zeros((nk, LANES), F32)
                for r0 in range(PEER_TOPK):
                    ok = (v0[r0:r0 + 1, :] + v1) >= tau
                    c_r0 = jnp.sum(jnp.where(ok, 1.0, 0.0), axis=0, keepdims=True)
                    cnt = jnp.where(s0 == v0[r0:r0 + 1, :], c_r0, cnt)
                n_scr[lt, pl.ds(ho, nk), :] = cnt
                r1_scr[lt, pl.ds(ho, nk), :] = rank1.astype(BF16)
                a_scr[lt, pl.ds(ho, nk), :] = jnp.exp(s0 - v0[0:1, :])
                b_scr[lt, pl.ds(ho, nk), :] = (jnp.exp(s1 - v1[0:1, :]) / zsum).astype(BF16)
                return carry2

            return lax.fori_loop(0, PEER_HEADS, per_head, carry)

        lax.fori_loop(0, n_lt, per_tile, 0)
        yt_scr[...] = jnp.zeros_like(yt_scr)

    groups = ts // nk

    def a_stage(k):
        at_scr[k % 2] = _mm(u_ref[k * ts:(k + 1) * ts, :], h2t_ref[...]).astype(BF16)

    def y_stage(k):
        yt_scr[...] += _mm(vt_ref[:, k * ts:(k + 1) * ts], wa_scr[k % 2])

    def w_stage(k, lt):
        accs = [jnp.zeros((nk, LANES), BF16) for _ in range(groups)]
        for h in range(PEER_HEADS):
            r1 = r1_scr[lt, h * nk:(h + 1) * nk, :]
            b1 = b_scr[lt, h * nk:(h + 1) * nk, :]
            for ii in range(groups):
                row = h * nk + c_idx * ic + k * groups + ii
                n_i = _row_tile_bf16(n_scr[lt, pl.ds(row, 1), :], nk)
                a_i = _row_tile_bf16(a_scr[lt, pl.ds(row, 1), :], nk)
                accs[ii] = accs[ii] + jnp.where(r1 < n_i, b1, 0.0) * a_i
        for ii in range(groups):
            g = _gelu_bf16(at_scr[k % 2, ii * nk:(ii + 1) * nk, lt * LANES:(lt + 1) * LANES])
            wa_scr[k % 2, ii * nk:(ii + 1) * nk, lt * LANES:(lt + 1) * LANES] = accs[ii] * g

    a_stage(0)
    for k in range(n_sub):
        if k + 1 < n_sub:
            a_stage(k + 1)
        if k >= 1:
            y_stage(k - 1)
        for lt in range(n_lt):
            w_stage(k, lt)
    y_stage(n_sub - 1)

    @pl.when(c_idx == pl.num_programs(1) - 1)
    def _fin():
        xo_ref[...] = x_ref[...] + g2_ref[...] * yt_scr[...].T


def _peer(h2t, x, mod3, wqt, kbt, u, vt, row_of_tile, tm, te, ts):
    nt = x.shape[0]
    n_lt = tm // LANES
    n_rt = PEER_HEADS * PEER_KEYS
    return pl.pallas_call(
        functools.partial(_peer_kernel, te=te, ts=ts),
        grid=(nt // tm, PEER_EXPERTS // te),
        in_specs=[
            pl.BlockSpec((D_MODEL, tm), lambda i, c: (0, i), pipeline_mode=pl.Buffered(1)),
            pl.BlockSpec((tm, D_MODEL), lambda i, c: (i, 0), pipeline_mode=pl.Buffered(1)),
            pl.BlockSpec((None, 1, D_MODEL), lambda i, c: (row_of_tile(i), 0, 5)),
            pl.BlockSpec((D_MODEL, D_MODEL), lambda i, c: (0, 0), pipeline_mode=pl.Buffered(1)),
            pl.BlockSpec((2 * n_rt, D_MODEL), lambda i, c: (0, 0), pipeline_mode=pl.Buffered(1)),
            pl.BlockSpec((te, D_MODEL), lambda i, c: (c, 0)),
            pl.BlockSpec((D_MODEL, te), lambda i, c: (0, c)),
        ],
        out_specs=pl.BlockSpec((tm, D_MODEL), lambda i, c: (i, 0)),
        out_shape=jax.ShapeDtypeStruct((nt, D_MODEL), F32),
        scratch_shapes=[
            pltpu.VMEM((n_lt, n_rt, LANES), F32),
            pltpu.VMEM((n_lt, n_rt, LANES), BF16),
            pltpu.VMEM((n_lt, n_rt, LANES), F32),
            pltpu.VMEM((n_lt, n_rt, LANES), BF16),
            pltpu.VMEM((n_lt, 2 * n_rt, LANES), F32),
            pltpu.VMEM((2 * PEER_TOPK, LANES), F32),
            pltpu.VMEM((2, ts, tm), BF16),
            pltpu.VMEM((2, ts, tm), BF16),
            pltpu.VMEM((D_MODEL, tm), F32),
        ],
        compiler_params=_cparams(("parallel", "arbitrary")),
        name="peer",
    )(h2t, x, mod3, wqt, kbt, u, vt)


def _final_kernel(x_ref, g_ref, o_ref):
    x = x_ref[...]
    o_ref[...] = x * lax.rsqrt(jnp.mean(x * x, axis=-1, keepdims=True) + EPS) * g_ref[...]


def _final_norm(x, gain, tm):
    nt = x.shape[0]
    return pl.pallas_call(
        _final_kernel,
        grid=(nt // tm,),
        in_specs=[pl.BlockSpec((tm, D_MODEL), lambda i: (i, 0)), pl.BlockSpec((1, D_MODEL), lambda i: (0, 0))],
        out_specs=pl.BlockSpec((tm, D_MODEL), lambda i: (i, 0)),
        out_shape=jax.ShapeDtypeStruct((nt, D_MODEL), F32),
        compiler_params=_cparams(("parallel",)),
        name="final_norm",
    )(x, gain)


def _block_diag(blocks):
    n, r, c = blocks.shape
    eye = jnp.eye(n, dtype=blocks.dtype)
    return jnp.einsum('nrc,nm->nrmc', blocks, eye).reshape(n * r, n * c)


def _state_to_bd_t(s):
    b, two, h, k, v = s.shape
    eye = jnp.eye(h, dtype=s.dtype)
    return jnp.einsum('bdhkv,hg->bdhvgk', s, eye).reshape(b, two, h * v, h * k)


def _bd_t_to_state(st, h, k, v):
    b = st.shape[0]
    return jnp.einsum('bdhvhk->bdhkv', st.reshape(b, 2, h, v, h, k))


def _rope_tables(t_len):
    rows = t_len // GRID_W
    row = jnp.repeat(jnp.arange(rows), GRID_W).astype(F32)
    col = jnp.tile(jnp.arange(GRID_W), rows).astype(F32)
    n_freq = RET_DK // 4
    inv_freq = ROPE_BASE ** (-jnp.arange(n_freq, dtype=F32) / n_freq)
    ang = jnp.concatenate([row[:, None] * inv_freq, col[:, None] * inv_freq], axis=-1)
    cos, sin = jnp.cos(ang), jnp.sin(ang)
    cos_h = jnp.concatenate([cos, cos], axis=-1)
    sin_h = jnp.concatenate([-sin, sin], axis=-1)
    return jnp.tile(cos_h, (1, RET_HEADS)), jnp.tile(sin_h, (1, RET_HEADS))


def _swap_matrix():
    lane = jnp.arange(RET_HEADS * RET_DK)
    half = RET_DK // 2
    src = jnp.where(lane % RET_DK < half, lane + half, lane - half)
    return (lane[:, None] == src[None, :]).astype(F32)


def kernel(x_prompt, x_sample, c, state_gla, state_ret, state_s5, state_lru, c_ctx, w_mod, b_mod, norm_mix, norm_ffn, norm_final, w_in, gla_w_decay, gla_b_decay, ret_decay_logit, s5_a_re, s5_a_im, s5_log_dt, s5_b_re, s5_b_im, s5_c_re, s5_c_im, s5_d, s5_w_glu, lru_conv_w, lru_conv_b, lru_w_a, lru_b_a, lru_w_x, lru_b_x, lru_lambda, w_branch, w_out, peer_w_q, peer_keys, peer_u, peer_v):
    n_bp, t_p, _ = x_prompt.shape
    n_bs, t_s, _ = x_sample.shape
    depth = w_in.shape[0]
    ntp, nts = n_bp * t_p, n_bs * t_s
    tm = TOK_TM
    assert ntp % t_s == 0 and all(ntp % t == 0 and t_s % t == 0 for t in (TOK_TM, PRE_TM, PEER_TM))

    x = jnp.concatenate([x_prompt.reshape(ntp, D_MODEL), x_sample.reshape(nts, D_MODEL)], axis=0)

    n_rows = 8 * ((1 + n_bs + 7) // 8)
    cond = jnp.zeros((n_rows, D_MODEL), F32).at[0].set(c_ctx).at[1:1 + n_bs].set(c)
    mods = _adaln(cond, w_mod, b_mod)

    def make_row_of_tile(tile):
        def row_of_tile(i):
            return jnp.where(i < ntp // tile, 0, 1 + (i - ntp // tile) // (t_s // tile))
        return row_of_tile

    zpad = lambda n: jnp.zeros((depth, D_MODEL, n), F32)
    w_in_p = jnp.concatenate([w_in[:, :, 2336:6432], w_in[:, :, 0:800], zpad(224), w_in[:, :, 800:1568], zpad(256),
                              w_in[:, :, 1824:2336], w_in[:, :, 1568:1824]], axis=2).astype(BF16)

    hk, hv = GLA_HEADS * GLA_DK, GLA_HEADS * GLA_DV
    e_mat = (jnp.arange(hk)[:, None] // GLA_DK == jnp.arange(hv)[None, :] // GLA_DV).astype(BF16)
    ind = (jnp.arange(hv)[:, None] // GLA_DV == jnp.arange(hv)[None, :] // GLA_DV).astype(F32) / GLA_DV
    swap = _swap_matrix()
    cos_s, sin_s = _rope_tables(t_s)
    cos_p, sin_p = jnp.ones((t_p, hk), F32), jnp.zeros((t_p, hk), F32)

    zeros_bd = jnp.zeros((n_bp, 2, hv, hk), F32)
    zeros_s5 = jnp.zeros((n_bp, 2, 2 * S5_CH), F32)
    zeros_lru = jnp.zeros((n_bp, 2, BR_W), F32)

    gla_l, ret_l, s5_l, lru_l = [], [], [], []
    for l in range(depth):
        mod3 = mods[l].reshape(n_rows, 1, N_MOD * D_MODEL)
        z = _premix(x, mod3, norm_mix[l].reshape(1, D_MODEL), w_in_p[l], make_row_of_tile(PRE_TM), PRE_TM)

        wd = jnp.zeros((128, 256), F32)
        wd = wd.at[0:GLA_RANK, 0:hk].set(gla_w_decay[l, 0]).at[GLA_RANK:2 * GLA_RANK, hk:].set(gla_w_decay[l, 1])
        bd = gla_b_decay[l].reshape(1, 2 * hk)
        lg = jax.nn.log_sigmoid(ret_decay_logit[l].astype(F32))
        lgl = jnp.repeat(lg, RET_DK, axis=1)
        bre = _block_diag(jnp.swapaxes(s5_b_re[l], 1, 2))
        bim = _block_diag(jnp.swapaxes(s5_b_im[l], 1, 2))
        bbd = jnp.concatenate([bre, bim], axis=1).astype(BF16)
        cre = _block_diag(jnp.swapaxes(s5_c_re[l], 1, 2)).astype(BF16)
        cim = _block_diag(jnp.swapaxes(s5_c_im[l], 1, 2)).astype(BF16)
        coef, pf, pb = _s5_disc(s5_a_re[l].reshape(2, S5_CH), s5_a_im[l].reshape(2, S5_CH),
                                jnp.repeat(s5_log_dt[l], S5_STATE, axis=1), SCAN_TB)
        dsk = s5_d[l].reshape(1, BR_W)
        wglu = s5_w_glu[l].astype(BF16)
        wg = jnp.concatenate([_block_diag(lru_w_a[l, 0]), _block_diag(lru_w_x[l, 0]),
                              _block_diag(lru_w_a[l, 1]), _block_diag(lru_w_x[l, 1])], axis=1).astype(BF16)
        bg = jnp.concatenate([lru_b_a[l, 0], lru_b_x[l, 0], lru_b_a[l, 1], lru_b_x[l, 1]]).reshape(1, 4 * BR_W)
        cw = lru_conv_w[l]
        cb = lru_conv_b[l].reshape(1, BR_W)
        lam = lru_lambda[l]

        s5_h0 = state_s5[:, l].reshape(n_bs, 2, 2 * S5_CH)

        outs = []
        for (row_off, n_b, t_len, sg, sr, ss, sl, cs, sn, rope) in (
                (0, n_bp, t_p, zeros_bd, zeros_bd, zeros_s5, zeros_lru, cos_p, sin_p, False),
                (ntp // t_s, n_bs, t_s, _state_to_bd_t(state_gla[:, l]), _state_to_bd_t(state_ret[:, l]),
                 s5_h0, state_lru[:, l], cos_s, sin_s, True)):
            m_a, f_gla = _gla(z, row_off, n_b, t_len, wd, bd, sg, e_mat, ind)
            m_b, f_ret = _ret(z, row_off, n_b, t_len, cs, sn, swap, lgl, lg, sr, ind, rope)
            m_c, f_s5 = _s5(z, row_off, n_b, t_len, bbd, cre, cim, coef, pf, pb, dsk, wglu, ss)
            m_d, f_lru = _lru(z, row_off, n_b, t_len, cw, cb, wg, bg, lam, sl)
            outs.append(((m_a, m_b, m_c, m_d), (f_gla, f_ret, f_s5, f_lru)))

        ms = [jnp.concatenate([outs[0][0][n], outs[1][0][n]], axis=0) for n in range(4)]
        f_gla, f_ret, f_s5, f_lru = outs[0][1]
        gla_l.append(_bd_t_to_state(f_gla, GLA_HEADS, GLA_DK, GLA_DV))
        ret_l.append(_bd_t_to_state(f_ret, RET_HEADS, RET_DK, RET_DV))
        s5_l.append(f_s5.reshape(n_bp, 2, 2, S5_GROUPS, S5_STATE))
        lru_l.append(f_lru)

        x, h2t = _merge(ms, z, x, mod3, norm_ffn[l].reshape(1, D_MODEL), w_branch[l].astype(BF16),
                        w_out[l].astype(BF16), make_row_of_tile(tm), tm)

        wqt = peer_w_q[l].T.astype(BF16)
        kbt = _block_diag(peer_keys[l].reshape(2 * PEER_HEADS, PEER_KEYS, PEER_QDIM // 2)).astype(BF16)
        x = _peer(h2t, x, mod3, wqt, kbt, peer_u[l].astype(BF16), peer_v[l].T.astype(BF16),
                  make_row_of_tile(PEER_TM), PEER_TM, PEER_TE, PEER_TS)

    y = _final_norm(x, norm_final.reshape(1, D_MODEL), tm)
    y_p = y[:ntp].reshape(n_bp, t_p, D_MODEL)
    y_s = y[ntp:].reshape(n_bs, t_s, D_MODEL)
    return (y_p, y_s, jnp.stack(gla_l, axis=1), jnp.stack(ret_l, axis=1),
            jnp.stack(s5_l, axis=1), jnp.stack(lru_l, axis=1))
```

```python
import functools
import math

import jax
import jax.numpy as jnp
from jax import lax
from jax.experimental import pallas as pl
from jax.experimental.pallas import tpu as pltpu

F32 = jnp.float32
BF16 = jnp.bfloat16
HI = lax.Precision.HIGHEST

D_MODEL = 1024
N_MOD = 6
EPS = 1e-6
BR_W = 256
GLA_HEADS, GLA_DK, GLA_DV, GLA_RANK, GLA_TAU, GLA_CHUNK = 4, 32, 64, 16, 16.0, 32
RET_HEADS, RET_DK, RET_DV, RET_CHUNK = 4, 32, 64, 64
ROPE_BASE = 10000.0
GRID_W = 64
S5_GROUP, S5_GROUPS, S5_STATE, S5_RE_MAX = 16, 16, 64, -1e-4
S5_CH = S5_GROUPS * S5_STATE
LRU_BLOCKS, LRU_BW, LRU_C = 4, 64, 8.0
PEER_HEADS, PEER_KEYS, PEER_TOPK, PEER_QDIM = 8, 128, 16, 128
PEER_EXPERTS = PEER_KEYS * PEER_KEYS

LANES = 128
VMEM_LIMIT = 56 * 1024 * 1024

Z_MERGE, Z_GLA, Z_RET, Z_LRU, Z_S5 = 0, 4096, 5120, 6144, 6656
Z_W = 6912

SCAN_TB = 128
TOK_TM = 256
PRE_TM = 512
PEER_TM, PEER_TE, PEER_TS = 512, 2048, 512
NEG_INF = float("-inf")


def _cparams(sem):
    return pltpu.CompilerParams(dimension_semantics=sem, vmem_limit_bytes=VMEM_LIMIT)


def _nt(a, b):
    return lax.dot_general(a, b, (((1,), (1,)), ((), ())), preferred_element_type=F32)


def _tn(a, b):
    return lax.dot_general(a, b, (((0,), (0,)), ((), ())), preferred_element_type=F32)


def _mm(a, b):
    return jnp.dot(a, b, preferred_element_type=F32)


def _mm_hi(a, b):
    return jnp.dot(a, b, preferred_element_type=F32, precision=HI)


def _split_bf16(x):
    hi = x.astype(BF16)
    return hi, (x - hi.astype(F32)).astype(BF16)


def _mm_exact_rhs(a, b_exact):
    hi, lo = _split_bf16(a)
    return _mm(hi, b_exact) + _mm(lo, b_exact)


def _mm_exact_lhs(a_exact, b):
    hi, lo = _split_bf16(b)
    return _mm(a_exact, hi) + _mm(a_exact, lo)


def _sigmoid(x):
    return jax.nn.sigmoid(x)


def _silu(x):
    return x * jax.nn.sigmoid(x)


def _gelu(x):
    return jax.nn.gelu(x)


GELU_C0 = math.sqrt(2.0 / math.pi)
GELU_C1 = GELU_C0 * 0.044715
BF16_ROWS = 16
F32_ROWS = 8


def _gelu_bf16(x):
    hx = 0.5 * x
    return hx + hx * jnp.tanh(x * (GELU_C0 + GELU_C1 * (x * x)))


def _row_tile_bf16(row, n_rows):
    one = jnp.broadcast_to(row, (BF16_ROWS, row.shape[1])).astype(BF16)
    return jnp.concatenate([one] * (n_rows // BF16_ROWS), axis=0)


def _log_sigmoid(x):
    return jnp.minimum(x, 0.0) - jnp.log(1.0 + jnp.exp(-jnp.abs(x)))


def _softplus(x):
    return jnp.maximum(x, 0.0) + jnp.log(1.0 + jnp.exp(-jnp.abs(x)))


def _rms_mod(x, gain, sc, sh):
    ms = jnp.mean(x * x, axis=-1, keepdims=True)
    return x * lax.rsqrt(ms + EPS) * gain * (1.0 + sc) + sh


def _adaln_kernel(c_ref, w_ref, b_ref, o_ref):
    o_ref[...] = _mm_hi(_silu(c_ref[...]), w_ref[...]) + b_ref[...]


def _adaln(cond, w_mod, b_mod):
    n_l = w_mod.shape[0]
    rows = cond.shape[0]
    tn = 1536
    return pl.pallas_call(
        _adaln_kernel,
        grid=(n_l, N_MOD * D_MODEL // tn),
        in_specs=[
            pl.BlockSpec((rows, D_MODEL), lambda l, j: (0, 0)),
            pl.BlockSpec((None, D_MODEL, tn), lambda l, j: (l, 0, j)),
            pl.BlockSpec((None, 1, tn), lambda l, j: (l, 0, j)),
        ],
        out_specs=pl.BlockSpec((None, rows, tn), lambda l, j: (l, 0, j)),
        out_shape=jax.ShapeDtypeStruct((n_l, rows, N_MOD * D_MODEL), F32),
        compiler_params=_cparams(("parallel", "parallel")),
        name="adaln",
    )(cond, w_mod, b_mod.reshape(n_l, 1, N_MOD * D_MODEL))


def _premix_kernel(x_ref, sh_ref, sc_ref, g_ref, w_ref, o_ref, h_scr):
    @pl.when(pl.program_id(1) == 0)
    def _():
        h_scr[...] = _rms_mod(x_ref[...], g_ref[...], sc_ref[...], sh_ref[...]).astype(BF16)

    o_ref[...] = _mm(h_scr[...], w_ref[...])


def _premix(x, mod3, gain, w, row_of_tile, tm):
    nt = x.shape[0]
    tn = 1152
    return pl.pallas_call(
        _premix_kernel,
        grid=(nt // tm, Z_W // tn),
        in_specs=[
            pl.BlockSpec((tm, D_MODEL), lambda i, j: (i, 0)),
            pl.BlockSpec((None, 1, D_MODEL), lambda i, j: (row_of_tile(i), 0, 0)),
            pl.BlockSpec((None, 1, D_MODEL), lambda i, j: (row_of_tile(i), 0, 1)),
            pl.BlockSpec((1, D_MODEL), lambda i, j: (0, 0)),
            pl.BlockSpec((D_MODEL, tn), lambda i, j: (0, j)),
        ],
        out_specs=pl.BlockSpec((tm, tn), lambda i, j: (i, j)),
        out_shape=jax.ShapeDtypeStruct((nt, Z_W), F32),
        scratch_shapes=[pltpu.VMEM((tm, D_MODEL), BF16)],
        compiler_params=_cparams(("parallel", "arbitrary")),
        name="premix",
    )(x, mod3, mod3, gain, w)


def _gla_kernel(z_ref, wd_ref, bd_ref, s0_ref, e_ref, ind_ref, m_ref, sfin_ref,
                la_scr, of_scr, st_scr, p_scr, cum_scr, k_scr, v_scr):
    t_len = z_ref.shape[0]
    c = GLA_CHUNK
    n_chunks = t_len // c
    hk = GLA_HEADS * GLA_DK
    hv = GLA_HEADS * GLA_DV
    scale = GLA_DK ** -0.5

    pre = _mm_hi(z_ref[:, 768:896], wd_ref[...]) + bd_ref[...]
    la_scr[...] = _log_sigmoid(pre) * (1.0 / GLA_TAU)

    ri = lax.broadcasted_iota(jnp.int32, (c, c), 0)
    ci = lax.broadcasted_iota(jnp.int32, (c, c), 1)
    tri_lo = (ri >= ci).astype(BF16)
    tri_up = (ri <= ci).astype(BF16)
    row = lax.broadcasted_iota(jnp.int32, (c, hk), 0)
    bd_mask = (lax.broadcasted_iota(jnp.int32, (hv, hk), 0) // GLA_DV
               == lax.broadcasted_iota(jnp.int32, (hv, hk), 1) // GLA_DK).astype(F32)

    def chunk(base, la, tri, reverse):
        cum = _mm_exact_lhs(tri, la)
        q = z_ref[pl.ds(base, c), 0:128] * scale
        k = z_ref[pl.ds(base, c), 128:256]
        v = z_ref[pl.ds(base, c), 256:512]
        edge = cum[0:1, :] if reverse else cum[c - 1:c, :]
        st = st_scr[...]
        o = _nt((q * jnp.exp(cum)).astype(BF16), st.astype(BF16))
        ke = k * jnp.exp(edge - cum)
        cum_scr[...] = cum
        k_scr[...] = k
        v_scr[...] = v
        for j in range(c):
            d = jnp.minimum(cum - cum_scr[j:j + 1, :], 0.0)
            p = q * k_scr[j:j + 1, :] * jnp.exp(d)
            keep = (row <= j) if reverse else (row >= j)
            p_scr[j * c:(j + 1) * c, :] = jnp.where(keep, p, 0.0).astype(BF16)
        pe = _mm(p_scr[...], e_ref[...])
        for j in range(c):
            o = o + pe[j * c:(j + 1) * c, :] * v_scr[j:j + 1, :]
        st_scr[...] = st * jnp.exp(edge) + bd_mask * _tn(v.astype(BF16), ke.astype(BF16))
        return o

    st_scr[...] = s0_ref[0]

    def fwd(i, carry):
        base = pl.multiple_of(i * c, c)
        of_scr[pl.ds(base, c), :] = chunk(base, la_scr[pl.ds(base, c), 0:128], tri_lo, False)
        return carry

    lax.fori_loop(0, n_chunks, fwd, 0)
    sfin_ref[0] = st_scr[...]
    st_scr[...] = s0_ref[1]

    def bwd(i, carry):
        base = pl.multiple_of((n_chunks - 1 - i) * c, c)
        o = of_scr[pl.ds(base, c), :] + chunk(base, la_scr[pl.ds(base, c), 128:256], tri_up, True)
        ms = _mm_exact_rhs(o * o, ind_ref[...])
        g = z_ref[pl.ds(base, c), 512:768]
        m_ref[pl.ds(base, c), :] = o * lax.rsqrt(ms + EPS) * _silu(g)
        return carry

    lax.fori_loop(0, n_chunks, bwd, 0)
    sfin_ref[1] = st_scr[...]


def _gla(z, row_off, n_b, t_len, wd, bd, s0t, e_mat, ind):
    hk, hv = GLA_HEADS * GLA_DK, GLA_HEADS * GLA_DV
    c = GLA_CHUNK
    return pl.pallas_call(
        _gla_kernel,
        grid=(n_b,),
        in_specs=[
            pl.BlockSpec((t_len, 1024), lambda b: (row_off + b, Z_GLA // 1024)),
            pl.BlockSpec((128, 256), lambda b: (0, 0)),
            pl.BlockSpec((1, 256), lambda b: (0, 0)),
            pl.BlockSpec((None, 2, hv, hk), lambda b: (b, 0, 0, 0)),
            pl.BlockSpec((hk, hv), lambda b: (0, 0)),
            pl.BlockSpec((hv, hv), lambda b: (0, 0)),
        ],
        out_specs=[
            pl.BlockSpec((t_len, BR_W), lambda b: (b, 0)),
            pl.BlockSpec((None, 2, hv, hk), lambda b: (b, 0, 0, 0)),
        ],
        out_shape=[
            jax.ShapeDtypeStruct((n_b * t_len, BR_W), F32),
            jax.ShapeDtypeStruct((n_b, 2, hv, hk), F32),
        ],
        scratch_shapes=[
            pltpu.VMEM((t_len, 256), F32),
            pltpu.VMEM((t_len, hv), F32),
            pltpu.VMEM((hv, hk), F32),
            pltpu.VMEM((c * c, hk), BF16),
            pltpu.VMEM((c, hk), F32),
            pltpu.VMEM((c, hk), F32),
            pltpu.VMEM((c, hv), F32),
        ],
        compiler_params=_cparams(("parallel",)),
        name="gla",
    )(z, wd, bd, s0t, e_mat, ind)


def _ret_kernel(z_ref, cos_ref, sin_ref, swap_ref, lgl_ref, lgs_ref, s0_ref, ind_ref, m_ref, sfin_ref,
                qk_scr, of_scr, st_scr, *, rope):
    t_len = z_ref.shape[0]
    c = RET_CHUNK
    n_chunks = t_len // c
    hk = RET_HEADS * RET_DK
    hv = RET_HEADS * RET_DV
    scale = RET_DK ** -0.5

    lgf = lgl_ref[0:1, :]
    lgb = lgl_ref[1:2, :]
    pos = lax.broadcasted_iota(jnp.int32, (c, hk), 0).astype(F32)
    wq_f = jnp.exp(lgf * (pos + 1.0))
    wk_f = jnp.exp(lgf * (c - 1.0 - pos))
    wq_b = jnp.exp(lgb * (c - pos))
    wk_b = jnp.exp(lgb * pos)
    dec_f = jnp.exp(lgf * float(c))
    dec_b = jnp.exp(lgb * float(c))

    ii = lax.broadcasted_iota(jnp.int32, (c, c), 0)
    jj = lax.broadcasted_iota(jnp.int32, (c, c), 1)
    rel = (ii - jj).astype(F32)
    dms = []
    for h in range(RET_HEADS):
        d_f = jnp.where(ii >= jj, jnp.exp(lgs_ref[0, h] * jnp.maximum(rel, 0.0)), 0.0)
        d_b = jnp.where(jj >= ii, jnp.exp(lgs_ref[1, h] * jnp.maximum(-rel, 0.0)), 0.0)
        dms.append(d_f + d_b)
    dmat = jnp.concatenate(dms, axis=1)

    ek_mask = (lax.broadcasted_iota(jnp.int32, (RET_HEADS * c, hk), 0) // c
               == lax.broadcasted_iota(jnp.int32, (RET_HEADS * c, hk), 1) // RET_DK).astype(F32)
    ev_mask = (lax.broadcasted_iota(jnp.int32, (RET_HEADS * c, hv), 0) // c
               == lax.broadcasted_iota(jnp.int32, (RET_HEADS * c, hv), 1) // RET_DV).astype(F32)
    bd_mask = (lax.broadcasted_iota(jnp.int32, (hv, hk), 0) // RET_DV
               == lax.broadcasted_iota(jnp.int32, (hv, hk), 1) // RET_DK).astype(F32)

    st_scr[...] = s0_ref[0]

    def fwd(i, carry):
        base = pl.multiple_of(i * c, c)
        q = z_ref[pl.ds(base, c), 0:128] * scale
        k = z_ref[pl.ds(base, c), 128:256]
        v = z_ref[pl.ds(base, c), 256:512]
        if rope:
            cs = cos_ref[pl.ds(base, c), :]
            sn = sin_ref[pl.ds(base, c), :]
            q = q * cs + _mm_exact_rhs(q, swap_ref[...]) * sn
            k = k * cs + _mm_exact_rhs(k, swap_ref[...]) * sn
        qk_scr[pl.ds(base, c), 0:128] = q
        qk_scr[pl.ds(base, c), 128:256] = k
        kexp = (jnp.concatenate([k] * RET_HEADS, axis=0) * ek_mask).astype(BF16)
        vexp = (jnp.concatenate([v] * RET_HEADS, axis=0) * ev_mask).astype(BF16)
        sc = _nt(q.astype(BF16), kexp) * dmat
        o = _mm(sc.astype(BF16), vexp)
        st = st_scr[...]
        o = o + _nt((q * wq_f).astype(BF16), st.astype(BF16))
        of_scr[pl.ds(base, c), :] = o
        st_scr[...] = st * dec_f + bd_mask * _tn(v.astype(BF16), (k * wk_f).astype(BF16))
        return carry

    lax.fori_loop(0, n_chunks, fwd, 0)
    sfin_ref[0] = st_scr[...]
    st_scr[...] = s0_ref[1]

    def bwd(i, carry):
        base = pl.multiple_of((n_chunks - 1 - i) * c, c)
        q = qk_scr[pl.ds(base, c), 0:128]
        k = qk_scr[pl.ds(base, c), 128:256]
        v = z_ref[pl.ds(base, c), 256:512]
        st = st_scr[...]
        o = of_scr[pl.ds(base, c), :] + _nt((q * wq_b).astype(BF16), st.astype(BF16))
        st_scr[...] = st * dec_b + bd_mask * _tn(v.astype(BF16), (k * wk_b).astype(BF16))
        ms = _mm_exact_rhs(o * o, ind_ref[...])
        g = z_ref[pl.ds(base, c), 512:768]
        m_ref[pl.ds(base, c), :] = o * lax.rsqrt(ms + EPS) * _silu(g)
        return carry

    lax.fori_loop(0, n_chunks, bwd, 0)
    sfin_ref[1] = st_scr[...]


def _ret(z, row_off, n_b, t_len, cos_t, sin_t, swap, lgl, lgs, s0t, ind, rope):
    hk, hv = RET_HEADS * RET_DK, RET_HEADS * RET_DV
    return pl.pallas_call(
        functools.partial(_ret_kernel, rope=rope),
        grid=(n_b,),
        in_specs=[
            pl.BlockSpec((t_len, 1024), lambda b: (row_off + b, Z_RET // 1024)),
            pl.BlockSpec((t_len, hk), lambda b: (0, 0)),
            pl.BlockSpec((t_len, hk), lambda b: (0, 0)),
            pl.BlockSpec((hk, hk), lambda b: (0, 0)),
            pl.BlockSpec((2, hk), lambda b: (0, 0)),
            pl.BlockSpec(memory_space=pltpu.SMEM),
            pl.BlockSpec((None, 2, hv, hk), lambda b: (b, 0, 0, 0)),
            pl.BlockSpec((hv, hv), lambda b: (0, 0)),
        ],
        out_specs=[
            pl.BlockSpec((t_len, BR_W), lambda b: (b, 0)),
            pl.BlockSpec((None, 2, hv, hk), lambda b: (b, 0, 0, 0)),
        ],
        out_shape=[
            jax.ShapeDtypeStruct((n_b * t_len, BR_W), F32),
            jax.ShapeDtypeStruct((n_b, 2, hv, hk), F32),
        ],
        scratch_shapes=[
            pltpu.VMEM((t_len, 2 * hk), F32),
            pltpu.VMEM((t_len, hv), F32),
            pltpu.VMEM((hv, hk), F32),
        ],
        compiler_params=_cparams(("parallel",)),
        name="ret",
    )(z, cos_t, sin_t, swap, lgl, lgs, s0t, ind)


def _s5_disc_kernel(are_ref, aim_ref, ldt_ref, coef_ref, pf_ref, pb_ref):
    tb = pf_ref.shape[0]
    re = jnp.minimum(are_ref[...], S5_RE_MAX)
    im = aim_ref[...]
    dt = jnp.exp(ldt_ref[...])
    er = jnp.exp(re * dt)
    lbr = er * jnp.cos(im * dt)
    lbi = er * jnp.sin(im * dt)
    den = re * re + im * im
    nr = lbr - 1.0
    coef_ref[:, 0:S5_CH] = (nr * re + lbi * im) / den
    coef_ref[:, S5_CH:] = (lbi * re - nr * im) / den
    t = lax.broadcasted_iota(jnp.int32, (tb, S5_CH), 0).astype(F32)
    nf = t + 1.0
    nb = float(tb) - t
    mf = jnp.exp(nf * (re[0:1] * dt[0:1]))
    pf_ref[:, 0:S5_CH] = mf * jnp.cos(nf * (im[0:1] * dt[0:1]))
    pf_ref[:, S5_CH:] = mf * jnp.sin(nf * (im[0:1] * dt[0:1]))
    mb = jnp.exp(nb * (re[1:2] * dt[1:2]))
    pb_ref[:, 0:S5_CH] = mb * jnp.cos(nb * (im[1:2] * dt[1:2]))
    pb_ref[:, S5_CH:] = mb * jnp.sin(nb * (im[1:2] * dt[1:2]))


def _s5_disc(are, aim, ldt, tb):
    return pl.pallas_call(
        _s5_disc_kernel,
        out_shape=[
            jax.ShapeDtypeStruct((2, 2 * S5_CH), F32),
            jax.ShapeDtypeStruct((tb, 2 * S5_CH), F32),
            jax.ShapeDtypeStruct((tb, 2 * S5_CH), F32),
        ],
        compiler_params=pltpu.CompilerParams(vmem_limit_bytes=VMEM_LIMIT),
        name="s5_disc",
    )(are, aim, ldt)


def _s5_kernel(u_ref, bbd_ref, cre_ref, cim_ref, coef_ref, pf_ref, pb_ref, dsk_ref, wglu_ref, h0_ref,
               m_ref, hfin_ref, y_scr, bu_scr, hr_scr, hi_scr, car_scr):
    t_len = u_ref.shape[0]
    tb = SCAN_TB
    sub = pf_ref.shape[0]
    n_blocks = t_len // tb
    row_in = lax.broadcasted_iota(jnp.int32, (tb, LANES), 0) % sub
    steps = [1 << s for s in range(int(math.log2(sub)))]

    def block(base, dr_i, reverse):
        p_ref = pb_ref if reverse else pf_ref
        u = u_ref[pl.ds(base, tb), :]
        bu_scr[...] = _mm(u.astype(BF16), bbd_ref[...])
        for g in range(S5_CH // LANES):
            lo, hi = g * LANES, (g + 1) * LANES
            br = bu_scr[:, lo:hi]
            bi = bu_scr[:, S5_CH + lo:S5_CH + hi]
            cr = coef_ref[dr_i:dr_i + 1, lo:hi]
            ci = coef_ref[dr_i:dr_i + 1, S5_CH + lo:S5_CH + hi]
            hr = cr * br - ci * bi
            hi_ = cr * bi + ci * br
            for d in steps:
                if reverse:
                    pr = p_ref[sub - d:sub - d + 1, lo:hi]
                    pi = p_ref[sub - d:sub - d + 1, S5_CH + lo:S5_CH + hi]
                    keep = row_in < sub - d
                    sr = jnp.where(keep, pltpu.roll(hr, tb - d, 0), 0.0)
                    si = jnp.where(keep, pltpu.roll(hi_, tb - d, 0), 0.0)
                else:
                    pr = p_ref[d - 1:d, lo:hi]
                    pi = p_ref[d - 1:d, S5_CH + lo:S5_CH + hi]
                    keep = row_in >= d
                    sr = jnp.where(keep, pltpu.roll(hr, d, 0), 0.0)
                    si = jnp.where(keep, pltpu.roll(hi_, d, 0), 0.0)
                hr, hi_ = hr + pr * sr - pi * si, hi_ + pr * si + pi * sr
            car = car_scr[0:1, lo:hi]
            cai = car_scr[0:1, S5_CH + lo:S5_CH + hi]
            pwr = p_ref[:, lo:hi]
            pwi = p_ref[:, S5_CH + lo:S5_CH + hi]
            n_grp = tb // sub
            for v in (range(n_grp - 1, -1, -1) if reverse else range(n_grp)):
                gr = hr[v * sub:(v + 1) * sub, :] + pwr * car - pwi * cai
                gi = hi_[v * sub:(v + 1) * sub, :] + pwr * cai + pwi * car
                hr_scr[v * sub:(v + 1) * sub, lo:hi] = gr
                hi_scr[v * sub:(v + 1) * sub, lo:hi] = gi
                edge = 0 if reverse else sub - 1
                car, cai = gr[edge:edge + 1, :], gi[edge:edge + 1, :]
            car_scr[0:1, lo:hi] = car
            car_scr[0:1, S5_CH + lo:S5_CH + hi] = cai
        y = _mm(hr_scr[...].astype(BF16), cre_ref[...]) - _mm(hi_scr[...].astype(BF16), cim_ref[...])
        return u, y

    car_scr[0:1, :] = h0_ref[0:1, :]

    def fwd(i, carry):
        base = pl.multiple_of(i * tb, tb)
        _, y = block(base, 0, False)
        y_scr[pl.ds(base, tb), :] = y
        return carry

    lax.fori_loop(0, n_blocks, fwd, 0)
    hfin_ref[0:1, :] = car_scr[0:1, :]
    car_scr[0:1, :] = h0_ref[1:2, :]

    def bwd(i, carry):
        base = pl.multiple_of((n_blocks - 1 - i) * tb, tb)
        u, y = block(base, 1, True)
        y = _gelu(y_scr[pl.ds(base, tb), :] + y + dsk_ref[...] * u)
        gg = _mm(y.astype(BF16), wglu_ref[...])
        m_ref[pl.ds(base, tb), :] = gg[:, 0:BR_W] * _sigmoid(gg[:, BR_W:])
        return carry

    lax.fori_loop(0, n_blocks, bwd, 0)
    hfin_ref[1:2, :] = car_scr[0:1, :]


def _s5(z, row_off, n_b, t_len, bbd, cre, cim, coef, pf, pb, dsk, wglu, h0):
    tb = SCAN_TB
    sub = pf.shape[0]
    full = lambda shape: pl.BlockSpec(shape, lambda b: (0,) * len(shape))
    return pl.pallas_call(
        _s5_kernel,
        grid=(n_b,),
        in_specs=[
            pl.BlockSpec((t_len, BR_W), lambda b: (row_off + b, Z_S5 // BR_W)),
            full((BR_W, 2 * S5_CH)),
            full((S5_CH, BR_W)),
            full((S5_CH, BR_W)),
            full((2, 2 * S5_CH)),
            full((sub, 2 * S5_CH)),
            full((sub, 2 * S5_CH)),
            full((1, BR_W)),
            full((BR_W, 2 * BR_W)),
            pl.BlockSpec((None, 2, 2 * S5_CH), lambda b: (b, 0, 0)),
        ],
        out_specs=[
            pl.BlockSpec((t_len, BR_W), lambda b: (b, 0)),
            pl.BlockSpec((None, 2, 2 * S5_CH), lambda b: (b, 0, 0)),
        ],
        out_shape=[
            jax.ShapeDtypeStruct((n_b * t_len, BR_W), F32),
            jax.ShapeDtypeStruct((n_b, 2, 2 * S5_CH), F32),
        ],
        scratch_shapes=[
            pltpu.VMEM((t_len, BR_W), F32),
            pltpu.VMEM((tb, 2 * S5_CH), F32),
            pltpu.VMEM((tb, S5_CH), F32),
            pltpu.VMEM((tb, S5_CH), F32),
            pltpu.VMEM((8, 2 * S5_CH), F32),
        ],
        compiler_params=_cparams(("parallel",)),
        name="s5",
    )(z, bbd, cre, cim, coef, pf, pb, dsk, wglu, h0)


def _lru_kernel(z_ref, cw_ref, cb_ref, wg_ref, bg_ref, lam_ref, h0_ref, m_ref, hfin_ref,
                xc_scr, hf_scr, car_scr):
    t_len = z_ref.shape[0]
    tb = SCAN_TB
    n_blocks = t_len // tb
    steps = [1 << s for s in range(int(math.log2(tb)))]

    x = z_ref[:, 0:BR_W]
    trow = lax.broadcasted_iota(jnp.int32, (t_len, BR_W), 0)
    xm1 = jnp.where(trow >= 1, pltpu.roll(x, 1, 0), 0.0)
    xp1 = jnp.where(trow < t_len - 1, pltpu.roll(x, t_len - 1, 0), 0.0)
    xp2 = jnp.where(trow < t_len - 2, pltpu.roll(x, t_len - 2, 0), 0.0)
    xc_scr[...] = (cw_ref[0:1, :] * xm1 + cw_ref[1:2, :] * x + cw_ref[2:3, :] * xp1
                   + cw_ref[3:4, :] * xp2 + cb_ref[...])

    row = lax.broadcasted_iota(jnp.int32, (tb, BR_W), 0)
    sp = _softplus(-lam_ref[...])

    def block(base, dr_i, reverse):
        xc = xc_scr[pl.ds(base, tb), :]
        off = dr_i * 2 * BR_W
        gates = _mm(xc.astype(BF16), wg_ref[:, off:off + 2 * BR_W]) + bg_ref[:, off:off + 2 * BR_W]
        r = _sigmoid(gates[:, 0:BR_W])
        ig = _sigmoid(gates[:, BR_W:])
        log_a = -LRU_C * r * sp[dr_i:dr_i + 1, :]
        a = jnp.exp(log_a)
        th = jnp.tanh(log_a)
        b = jnp.sqrt(-2.0 * th / (1.0 - th)) * (ig * xc)
        for d in steps:
            if reverse:
                keep = row < tb - d
                a_s = jnp.where(keep, pltpu.roll(a, tb - d, 0), 1.0)
                b_s = jnp.where(keep, pltpu.roll(b, tb - d, 0), 0.0)
            else:
                keep = row >= d
                a_s = jnp.where(keep, pltpu.roll(a, d, 0), 1.0)
                b_s = jnp.where(keep, pltpu.roll(b, d, 0), 0.0)
            b = b + a * b_s
            a = a * a_s
        h = b + a * car_scr[dr_i:dr_i + 1, :]
        edge = 0 if reverse else tb - 1
        car_scr[dr_i:dr_i + 1, :] = h[edge:edge + 1, :]
        return h

    car_scr[0:2, :] = h0_ref[...]

    def fwd(i, carry):
        base = pl.multiple_of(i * tb, tb)
        hf_scr[pl.ds(base, tb), :] = block(base, 0, False)
        return carry

    lax.fori_loop(0, n_blocks, fwd, 0)

    def bwd(i, carry):
        base = pl.multiple_of((n_blocks - 1 - i) * tb, tb)
        h = block(base, 1, True) + hf_scr[pl.ds(base, tb), :]
        m_ref[pl.ds(base, tb), :] = h * _gelu(z_ref[pl.ds(base, tb), BR_W:2 * BR_W])
        return carry

    lax.fori_loop(0, n_blocks, bwd, 0)
    hfin_ref[...] = car_scr[0:2, :]


def _lru(z, row_off, n_b, t_len, cw, cb, wg, bg, lam, h0):
    full = lambda shape: pl.BlockSpec(shape, lambda b: (0,) * len(shape))
    return pl.pallas_call(
        _lru_kernel,
        grid=(n_b,),
        in_specs=[
            pl.BlockSpec((t_len, 2 * BR_W), lambda b: (row_off + b, Z_LRU // (2 * BR_W))),
            full((4, BR_W)),
            full((1, BR_W)),
            full((BR_W, 4 * BR_W)),
            full((1, 4 * BR_W)),
            full((2, BR_W)),
            pl.BlockSpec((None, 2, BR_W), lambda b: (b, 0, 0)),
        ],
        out_specs=[
            pl.BlockSpec((t_len, BR_W), lambda b: (b, 0)),
            pl.BlockSpec((None, 2, BR_W), lambda b: (b, 0, 0)),
        ],
        out_shape=[
            jax.ShapeDtypeStruct((n_b * t_len, BR_W), F32),
            jax.ShapeDtypeStruct((n_b, 2, BR_W), F32),
        ],
        scratch_shapes=[
            pltpu.VMEM((t_len, BR_W), F32),
            pltpu.VMEM((t_len, BR_W), F32),
            pltpu.VMEM((8, BR_W), F32),
        ],
        compiler_params=_cparams(("parallel",)),
        name="lru",
    )(z, cw, cb, wg, bg, lam, h0)


def _merge_kernel(ma_ref, mb_ref, mc_ref, md_ref, zm_ref, x_ref, g1_ref, sh2_ref, sc2_ref, gain_ref,
                  wb_ref, wo_ref, xo_ref, h2t_ref):
    acc = None
    for n, m_ref in enumerate((ma_ref, mb_ref, mc_ref, md_ref)):
        proj = _mm(m_ref[...].astype(BF16), wb_ref[n])
        term = _sigmoid(zm_ref[:, n * D_MODEL:(n + 1) * D_MODEL]) * proj
        acc = term if acc is None else acc + term
    xn = x_ref[...] + g1_ref[...] * _mm(acc.astype(BF16), wo_ref[...])
    xo_ref[...] = xn
    h2t_ref[...] = _rms_mod(xn, gain_ref[...], sc2_ref[...], sh2_ref[...]).T.astype(BF16)


def _merge(ms, z, x, mod3, gain, wb, wo, row_of_tile, tm):
    nt = x.shape[0]
    modspec = lambda k: pl.BlockSpec((None, 1, D_MODEL), lambda i: (row_of_tile(i), 0, k))
    return pl.pallas_call(
        _merge_kernel,
        grid=(nt // tm,),
        in_specs=[pl.BlockSpec((tm, BR_W), lambda i: (i, 0))] * 4 + [
            pl.BlockSpec((tm, 4 * D_MODEL), lambda i: (i, 0)),
            pl.BlockSpec((tm, D_MODEL), lambda i: (i, 0)),
            modspec(2), modspec(3), modspec(4),
            pl.BlockSpec((1, D_MODEL), lambda i: (0, 0)),
            pl.BlockSpec((4, BR_W, D_MODEL), lambda i: (0, 0, 0)),
            pl.BlockSpec((D_MODEL, D_MODEL), lambda i: (0, 0)),
        ],
        out_specs=[
            pl.BlockSpec((tm, D_MODEL), lambda i: (i, 0)),
            pl.BlockSpec((D_MODEL, tm), lambda i: (0, i)),
        ],
        out_shape=[
            jax.ShapeDtypeStruct((nt, D_MODEL), F32),
            jax.ShapeDtypeStruct((D_MODEL, nt), BF16),
        ],
        compiler_params=_cparams(("parallel",)),
        name="merge",
    )(*ms, z, x, mod3, mod3, mod3, gain, wb, wo)


def _peer_kernel(h2t_ref, x_ref, g2_ref, wqt_ref, kbt_ref, u_ref, vt_ref, xo_ref,
                 n_scr, r1_scr, a_scr, b_scr, sc_scr, v_scr, at_scr, wa_scr, yt_scr, *, te, ts):
    tm = h2t_ref.shape[1]
    n_lt = tm // LANES
    ic = te // PEER_KEYS
    n_sub = te // ts
    c_idx = pl.program_id(1)
    nk = PEER_KEYS

    @pl.when(c_idx == 0)
    def _route():
        qt = _mm(wqt_ref[...], h2t_ref[...])
        sct = _mm(kbt_ref[...], qt.astype(BF16))
        for lt in range(n_lt):
            sc_scr[lt] = sct[:, lt * LANES:(lt + 1) * LANES]
        r8 = lax.broadcasted_iota(jnp.int32, (8, LANES), 0)

        def per_tile(lt, carry):
            def per_head(h, vo):
                rank1 = jnp.full((nk, LANES), float(PEER_TOPK), F32)
                for s in range(2):
                    off = pl.multiple_of(h * (2 * nk) + s * nk, nk)
                    xs = sc_scr[lt, pl.ds(off, nk), :]
                    for r in range(PEER_TOPK):
                        m = jnp.max(xs, axis=0, keepdims=True)
                        v_scr[vo + s * PEER_TOPK + r:vo + s * PEER_TOPK + r + 1, :] = m
                        hit = xs == m
                        if s == 1:
                            rank1 = jnp.where(hit, float(r), rank1)
                        xs = jnp.where(hit, NEG_INF, xs)
                v0 = v_scr[vo:vo + PEER_TOPK, :]
                v1 = v_scr[vo + PEER_TOPK:vo + 2 * PEER_TOPK, :]
                pieces = [v0[0:1, :] + v1]
                for r0 in range(1, 8):
                    pieces.append(jnp.where(r8 < PEER_TOPK // (r0 + 1), v0[r0:r0 + 1, :] + v1[0:8, :], NEG_INF))
                pieces.append(v0[8:16, :] + v1[0:1, :])
                cand = jnp.concatenate(pieces, axis=0)
                top = v0[0:1, :] + v1[0:1, :]
                zsum = jnp.zeros_like(top)
                tau = top
                for r in range(PEER_TOPK):
                    tau = jnp.max(cand, axis=0, keepdims=True)
                    zsum = zsum + jnp.exp(tau - top)
                    cand = jnp.where(cand == tau, NEG_INF, cand)
                o0 = pl.multiple_of(h * (2 * nk), nk)
                o1 = pl.multiple_of(h * (2 * nk) + nk, nk)
                ho = pl.multiple_of(h * nk, nk)
                s0 = sc_scr[lt, pl.ds(o0, nk), :]
                s1 = sc_scr[lt, pl.ds(o1, nk), :]
                cnt = jnp.zeros((nk, LANES), F32)
                for r0 in range(PEER_TOPK):
                    ok = (v0[r0:r0 + 1, :] + v1) >= tau
                    c_r0 = jnp.sum(jnp.where(ok, 1.0, 0.0), axis=0, keepdims=True)
                    cnt = jnp.where(s0 == v0[r0:r0 + 1, :], c_r0, cnt)
                n_scr[lt, pl.ds(ho, nk), :] = cnt
                r1_scr[lt, pl.ds(ho, nk), :] = rank1.astype(BF16)
                a_scr[lt, pl.ds(ho, nk), :] = jnp.exp(s0 - v0[0:1, :])
                b_scr[lt, pl.ds(ho, nk), :] = (jnp.exp(s1 - v1[0:1, :]) / zsum).astype(BF16)

            def per_pair(hp, carry2):
                per_head(2 * hp, 0)
                per_head(2 * hp + 1, 2 * PEER_TOPK)
                return carry2

            return lax.fori_loop(0, PEER_HEADS // 2, per_pair, carry)

        lax.fori_loop(0, n_lt, per_tile, 0)
        yt_scr[...] = jnp.zeros_like(yt_scr)

    groups = ts // nk

    def a_stage(k):
        at_scr[k % 2] = _mm(u_ref[k * ts:(k + 1) * ts, :], h2t_ref[...]).astype(BF16)

    def y_stage(k):
        yt_scr[...] += _mm(vt_ref[:, k * ts:(k + 1) * ts], wa_scr[k % 2])

    def w_stage(k, lt):
        pk = BF16_ROWS
        nv = nk // pk
        accs = [[None] * nv for _ in range(groups)]
        for h in range(PEER_HEADS):
            n_t, a_t = [], []
            for ii in range(groups):
                row = h * nk + c_idx * ic + k * groups + ii
                n_t.append(_row_tile_bf16(n_scr[lt, pl.ds(row, 1), :], pk))
                a_t.append(_row_tile_bf16(a_scr[lt, pl.ds(row, 1), :], pk))
            for jv in range(nv):
                r1 = r1_scr[lt, h * nk + jv * pk:h * nk + (jv + 1) * pk, :]
                b1 = b_scr[lt, h * nk + jv * pk:h * nk + (jv + 1) * pk, :]
                for ii in range(groups):
                    term = jnp.where(r1 < n_t[ii], b1, 0.0) * a_t[ii]
                    accs[ii][jv] = term if accs[ii][jv] is None else accs[ii][jv] + term
        cols = slice(lt * LANES, (lt + 1) * LANES)
        for ii in range(groups):
            for jv in range(nv):
                rows = slice(ii * nk + jv * pk, ii * nk + (jv + 1) * pk)
                wa_scr[k % 2, rows, cols] = accs[ii][jv] * _gelu_bf16(at_scr[k % 2, rows, cols])

    a_stage(0)
    for k in range(n_sub):
        if k + 1 < n_sub:
            a_stage(k + 1)
        if k >= 1:
            y_stage(k - 1)
        for lt in range(n_lt):
            w_stage(k, lt)
    y_stage(n_sub - 1)

    @pl.when(c_idx == pl.num_programs(1) - 1)
    def _fin():
        xo_ref[...] = x_ref[...] + g2_ref[...] * yt_scr[...].T


def _peer(h2t, x, mod3, wqt, kbt, u, vt, row_of_tile, tm, te, ts):
    nt = x.shape[0]
    n_lt = tm // LANES
    n_rt = PEER_HEADS * PEER_KEYS
    return pl.pallas_call(
        functools.partial(_peer_kernel, te=te, ts=ts),
        grid=(nt // tm, PEER_EXPERTS // te),
        in_specs=[
            pl.BlockSpec((D_MODEL, tm), lambda i, c: (0, i), pipeline_mode=pl.Buffered(1)),
            pl.BlockSpec((tm, D_MODEL), lambda i, c: (i, 0), pipeline_mode=pl.Buffered(1)),
            pl.BlockSpec((None, 1, D_MODEL), lambda i, c: (row_of_tile(i), 0, 5)),
            pl.BlockSpec((D_MODEL, D_MODEL), lambda i, c: (0, 0), pipeline_mode=pl.Buffered(1)),
            pl.BlockSpec((2 * n_rt, D_MODEL), lambda i, c: (0, 0), pipeline_mode=pl.Buffered(1)),
            pl.BlockSpec((te, D_MODEL), lambda i, c: (c, 0)),
            pl.BlockSpec((None, D_MODEL, te), lambda i, c: (c, 0, 0)),
        ],
        out_specs=pl.BlockSpec((tm, D_MODEL), lambda i, c: (i, 0)),
        out_shape=jax.ShapeDtypeStruct((nt, D_MODEL), F32),
        scratch_shapes=[
            pltpu.VMEM((n_lt, n_rt, LANES), F32),
            pltpu.VMEM((n_lt, n_rt, LANES), BF16),
            pltpu.VMEM((n_lt, n_rt, LANES), F32),
            pltpu.VMEM((n_lt, n_rt, LANES), BF16),
            pltpu.VMEM((n_lt, 2 * n_rt, LANES), F32),
            pltpu.VMEM((4 * PEER_TOPK, LANES), F32),
            pltpu.VMEM((2, ts, tm), BF16),
            pltpu.VMEM((2, ts, tm), BF16),
            pltpu.VMEM((D_MODEL, tm), F32),
        ],
        compiler_params=_cparams(("parallel", "arbitrary")),
        name="peer",
    )(h2t, x, mod3, wqt, kbt, u, vt)


def _final_kernel(x_ref, g_ref, o_ref):
    x = x_ref[...]
    o_ref[...] = x * lax.rsqrt(jnp.mean(x * x, axis=-1, keepdims=True) + EPS) * g_ref[...]


def _final_norm(x, gain, tm):
    nt = x.shape[0]
    return pl.pallas_call(
        _final_kernel,
        grid=(nt // tm,),
        in_specs=[pl.BlockSpec((tm, D_MODEL), lambda i: (i, 0)), pl.BlockSpec((1, D_MODEL), lambda i: (0, 0))],
        out_specs=pl.BlockSpec((tm, D_MODEL), lambda i: (i, 0)),
        out_shape=jax.ShapeDtypeStruct((nt, D_MODEL), F32),
        compiler_params=_cparams(("parallel",)),
        name="final_norm",
    )(x, gain)


def _block_diag(blocks):
    n, r, c = blocks.shape
    eye = jnp.eye(n, dtype=blocks.dtype)
    return jnp.einsum('nrc,nm->nrmc', blocks, eye).reshape(n * r, n * c)


def _state_to_bd_t(s):
    b, two, h, k, v = s.shape
    eye = jnp.eye(h, dtype=s.dtype)
    return jnp.einsum('bdhkv,hg->bdhvgk', s, eye).reshape(b, two, h * v, h * k)


def _bd_t_to_state(st, h, k, v):
    b = st.shape[0]
    return jnp.einsum('bdhvhk->bdhkv', st.reshape(b, 2, h, v, h, k))


def _rope_tables(t_len):
    rows = t_len // GRID_W
    row = jnp.repeat(jnp.arange(rows), GRID_W).astype(F32)
    col = jnp.tile(jnp.arange(GRID_W), rows).astype(F32)
    n_freq = RET_DK // 4
    inv_freq = ROPE_BASE ** (-jnp.arange(n_freq, dtype=F32) / n_freq)
    ang = jnp.concatenate([row[:, None] * inv_freq, col[:, None] * inv_freq], axis=-1)
    cos, sin = jnp.cos(ang), jnp.sin(ang)
    cos_h = jnp.concatenate([cos, cos], axis=-1)
    sin_h = jnp.concatenate([-sin, sin], axis=-1)
    return jnp.tile(cos_h, (1, RET_HEADS)), jnp.tile(sin_h, (1, RET_HEADS))


def _swap_matrix():
    lane = jnp.arange(RET_HEADS * RET_DK)
    half = RET_DK // 2
    src = jnp.where(lane % RET_DK < half, lane + half, lane - half)
    return (lane[:, None] == src[None, :]).astype(BF16)


def kernel(x_prompt, x_sample, c, state_gla, state_ret, state_s5, state_lru, c_ctx, w_mod, b_mod, norm_mix, norm_ffn, norm_final, w_in, gla_w_decay, gla_b_decay, ret_decay_logit, s5_a_re, s5_a_im, s5_log_dt, s5_b_re, s5_b_im, s5_c_re, s5_c_im, s5_d, s5_w_glu, lru_conv_w, lru_conv_b, lru_w_a, lru_b_a, lru_w_x, lru_b_x, lru_lambda, w_branch, w_out, peer_w_q, peer_keys, peer_u, peer_v):
    n_bp, t_p, _ = x_prompt.shape
    n_bs, t_s, _ = x_sample.shape
    depth = w_in.shape[0]
    ntp, nts = n_bp * t_p, n_bs * t_s
    tm = TOK_TM
    assert ntp % t_s == 0 and all(ntp % t == 0 and t_s % t == 0 for t in (TOK_TM, PRE_TM, PEER_TM))

    x = jnp.concatenate([x_prompt.reshape(ntp, D_MODEL), x_sample.reshape(nts, D_MODEL)], axis=0)

    n_rows = 8 * ((1 + n_bs + 7) // 8)
    cond = jnp.zeros((n_rows, D_MODEL), F32).at[0].set(c_ctx).at[1:1 + n_bs].set(c)
    mods = _adaln(cond, w_mod, b_mod)

    def make_row_of_tile(tile):
        def row_of_tile(i):
            return jnp.where(i < ntp // tile, 0, 1 + (i - ntp // tile) // (t_s // tile))
        return row_of_tile

    zpad = lambda n: jnp.zeros((depth, D_MODEL, n), F32)
    w_in_p = jnp.concatenate([w_in[:, :, 2336:6432], w_in[:, :, 0:800], zpad(224), w_in[:, :, 800:1568], zpad(256),
                              w_in[:, :, 1824:2336], w_in[:, :, 1568:1824]], axis=2).astype(BF16)

    hk, hv = GLA_HEADS * GLA_DK, GLA_HEADS * GLA_DV
    e_mat = (jnp.arange(hk)[:, None] // GLA_DK == jnp.arange(hv)[None, :] // GLA_DV).astype(BF16)
    ind = ((jnp.arange(hv)[:, None] // GLA_DV == jnp.arange(hv)[None, :] // GLA_DV).astype(F32) / GLA_DV).astype(BF16)
    swap = _swap_matrix()
    cos_s, sin_s = _rope_tables(t_s)
    cos_p, sin_p = jnp.ones((t_p, hk), F32), jnp.zeros((t_p, hk), F32)

    zeros_bd = jnp.zeros((n_bp, 2, hv, hk), F32)
    zeros_s5 = jnp.zeros((n_bp, 2, 2 * S5_CH), F32)
    zeros_lru = jnp.zeros((n_bp, 2, BR_W), F32)

    gla_l, ret_l, s5_l, lru_l = [], [], [], []
    for l in range(depth):
        mod3 = mods[l].reshape(n_rows, 1, N_MOD * D_MODEL)
        z = _premix(x, mod3, norm_mix[l].reshape(1, D_MODEL), w_in_p[l], make_row_of_tile(PRE_TM), PRE_TM)

        wd = jnp.zeros((128, 256), F32)
        wd = wd.at[0:GLA_RANK, 0:hk].set(gla_w_decay[l, 0]).at[GLA_RANK:2 * GLA_RANK, hk:].set(gla_w_decay[l, 1])
        bd = gla_b_decay[l].reshape(1, 2 * hk)
        lg = jax.nn.log_sigmoid(ret_decay_logit[l].astype(F32))
        lgl = jnp.repeat(lg, RET_DK, axis=1)
        bre = _block_diag(jnp.swapaxes(s5_b_re[l], 1, 2))
        bim = _block_diag(jnp.swapaxes(s5_b_im[l], 1, 2))
        bbd = jnp.concatenate([bre, bim], axis=1).astype(BF16)
        cre = _block_diag(jnp.swapaxes(s5_c_re[l], 1, 2)).astype(BF16)
        cim = _block_diag(jnp.swapaxes(s5_c_im[l], 1, 2)).astype(BF16)
        coef, pf, pb = _s5_disc(s5_a_re[l].reshape(2, S5_CH), s5_a_im[l].reshape(2, S5_CH),
                                jnp.repeat(s5_log_dt[l], S5_STATE, axis=1), F32_ROWS)
        dsk = s5_d[l].reshape(1, BR_W)
        wglu = s5_w_glu[l].astype(BF16)
        wg = jnp.concatenate([_block_diag(lru_w_a[l, 0]), _block_diag(lru_w_x[l, 0]),
                              _block_diag(lru_w_a[l, 1]), _block_diag(lru_w_x[l, 1])], axis=1).astype(BF16)
        bg = jnp.concatenate([lru_b_a[l, 0], lru_b_x[l, 0], lru_b_a[l, 1], lru_b_x[l, 1]]).reshape(1, 4 * BR_W)
        cw = lru_conv_w[l]
        cb = lru_conv_b[l].reshape(1, BR_W)
        lam = lru_lambda[l]

        s5_h0 = state_s5[:, l].reshape(n_bs, 2, 2 * S5_CH)

        outs = []
        for (row_off, n_b, t_len, sg, sr, ss, sl, cs, sn, rope) in (
                (0, n_bp, t_p, zeros_bd, zeros_bd, zeros_s5, zeros_lru, cos_p, sin_p, False),
                (ntp // t_s, n_bs, t_s, _state_to_bd_t(state_gla[:, l]), _state_to_bd_t(state_ret[:, l]),
                 s5_h0, state_lru[:, l], cos_s, sin_s, True)):
            m_a, f_gla = _gla(z, row_off, n_b, t_len, wd, bd, sg, e_mat, ind)
            m_b, f_ret = _ret(z, row_off, n_b, t_len, cs, sn, swap, lgl, lg, sr, ind, rope)
            m_c, f_s5 = _s5(z, row_off, n_b, t_len, bbd, cre, cim, coef, pf, pb, dsk, wglu, ss)
            m_d, f_lru = _lru(z, row_off, n_b, t_len, cw, cb, wg, bg, lam, sl)
            outs.append(((m_a, m_b, m_c, m_d), (f_gla, f_ret, f_s5, f_lru)))

        ms = [jnp.concatenate([outs[0][0][n], outs[1][0][n]], axis=0) for n in range(4)]
        f_gla, f_ret, f_s5, f_lru = outs[0][1]
        gla_l.append(_bd_t_to_state(f_gla, GLA_HEADS, GLA_DK, GLA_DV))
        ret_l.append(_bd_t_to_state(f_ret, RET_HEADS, RET_DK, RET_DV))
        s5_l.append(f_s5.reshape(n_bp, 2, 2, S5_GROUPS, S5_STATE))
        lru_l.append(f_lru)

        x, h2t = _merge(ms, z, x, mod3, norm_ffn[l].reshape(1, D_MODEL), w_branch[l].astype(BF16),
                        w_out[l].astype(BF16), make_row_of_tile(tm), tm)

        wqt = peer_w_q[l].T.astype(BF16)
        vt_l = peer_v[l].reshape(PEER_EXPERTS // PEER_TE, PEER_TE, D_MODEL).transpose(0, 2, 1).astype(BF16)
        kbt = _block_diag(peer_keys[l].reshape(2 * PEER_HEADS, PEER_KEYS, PEER_QDIM // 2)).astype(BF16)
        x = _peer(h2t, x, mod3, wqt, kbt, peer_u[l].astype(BF16), vt_l,
                  make_row_of_tile(PEER_TM), PEER_TM, PEER_TE, PEER_TS)

    y = _final_norm(x, norm_final.reshape(1, D_MODEL), tm)
    y_p = y[:ntp].reshape(n_bp, t_p, D_MODEL)
    y_s = y[ntp:].reshape(n_bs, t_s, D_MODEL)
    return (y_p, y_s, jnp.stack(gla_l, axis=1), jnp.stack(ret_l, axis=1),
            jnp.stack(s5_l, axis=1), jnp.stack(lru_l, axis=1))
```

```python
import functools
import math

import jax
import jax.numpy as jnp
from jax import lax
from jax.experimental import pallas as pl
from jax.experimental.pallas import tpu as pltpu

F32 = jnp.float32
BF16 = jnp.bfloat16
HI = lax.Precision.HIGHEST

D_MODEL = 1024
N_MOD = 6
EPS = 1e-6
BR_W = 256
GLA_HEADS, GLA_DK, GLA_DV, GLA_RANK, GLA_TAU, GLA_CHUNK = 4, 32, 64, 16, 16.0, 32
RET_HEADS, RET_DK, RET_DV, RET_CHUNK = 4, 32, 64, 64
ROPE_BASE = 10000.0
GRID_W = 64
S5_GROUP, S5_GROUPS, S5_STATE, S5_RE_MAX = 16, 16, 64, -1e-4
S5_CH = S5_GROUPS * S5_STATE
LRU_BLOCKS, LRU_BW, LRU_C = 4, 64, 8.0
PEER_HEADS, PEER_KEYS, PEER_TOPK, PEER_QDIM = 8, 128, 16, 128
PEER_EXPERTS = PEER_KEYS * PEER_KEYS

LANES = 128
VMEM_LIMIT = 56 * 1024 * 1024

Z_TN = 1024
Z_GATE_W = 4 * D_MODEL
Z_GLA, Z_RET, Z_LRU, Z_S5 = 0, 1024, 2048, 2560
Z_MIX_W = 3072
Z_W = Z_GATE_W + Z_MIX_W

SCAN_TB = 128
TOK_TM = 256
PRE_TM = 512
PEER_TM, PEER_TE, PEER_TS = 512, 2048, 512
NEG_INF = float("-inf")


def _cparams(sem):
    return pltpu.CompilerParams(dimension_semantics=sem, vmem_limit_bytes=VMEM_LIMIT)


def _nt(a, b):
    return lax.dot_general(a, b, (((1,), (1,)), ((), ())), preferred_element_type=F32)


def _tn(a, b):
    return lax.dot_general(a, b, (((0,), (0,)), ((), ())), preferred_element_type=F32)


def _mm(a, b):
    return jnp.dot(a, b, preferred_element_type=F32)


def _mm_hi(a, b):
    return jnp.dot(a, b, preferred_element_type=F32, precision=HI)


def _split_bf16(x):
    hi = x.astype(BF16)
    return hi, (x - hi.astype(F32)).astype(BF16)


def _mm_exact_rhs(a, b_exact):
    hi, lo = _split_bf16(a)
    return _mm(hi, b_exact) + _mm(lo, b_exact)


def _mm_exact_lhs(a_exact, b):
    hi, lo = _split_bf16(b)
    return _mm(a_exact, hi) + _mm(a_exact, lo)


def _sigmoid(x):
    return jax.nn.sigmoid(x)


def _silu(x):
    return x * jax.nn.sigmoid(x)


def _gelu(x):
    return jax.nn.gelu(x)


GELU_C0 = math.sqrt(2.0 / math.pi)
GELU_C1 = GELU_C0 * 0.044715
BF16_ROWS = 16
F32_ROWS = 8


def _gelu_bf16(x):
    hx = 0.5 * x
    return hx + hx * jnp.tanh(x * (GELU_C0 + GELU_C1 * (x * x)))


def _row_tile_bf16(row, n_rows):
    one = jnp.broadcast_to(row, (BF16_ROWS, row.shape[1])).astype(BF16)
    return jnp.concatenate([one] * (n_rows // BF16_ROWS), axis=0)


def _log_sigmoid(x):
    return jnp.minimum(x, 0.0) - jnp.log(1.0 + jnp.exp(-jnp.abs(x)))


def _softplus(x):
    return jnp.maximum(x, 0.0) + jnp.log(1.0 + jnp.exp(-jnp.abs(x)))


def _rms_mod(x, gain, sc, sh):
    ms = jnp.mean(x * x, axis=-1, keepdims=True)
    return x * lax.rsqrt(ms + EPS) * gain * (1.0 + sc) + sh


def _adaln_kernel(c_ref, w_ref, b_ref, o_ref):
    o_ref[...] = _mm_hi(_silu(c_ref[...]), w_ref[...]) + b_ref[...]


def _adaln(cond, w_mod, b_mod):
    n_l = w_mod.shape[0]
    rows = cond.shape[0]
    tn = 1536
    return pl.pallas_call(
        _adaln_kernel,
        grid=(n_l, N_MOD * D_MODEL // tn),
        in_specs=[
            pl.BlockSpec((rows, D_MODEL), lambda l, j: (0, 0)),
            pl.BlockSpec((None, D_MODEL, tn), lambda l, j: (l, 0, j)),
            pl.BlockSpec((None, 1, tn), lambda l, j: (l, 0, j)),
        ],
        out_specs=pl.BlockSpec((None, rows, tn), lambda l, j: (l, 0, j)),
        out_shape=jax.ShapeDtypeStruct((n_l, rows, N_MOD * D_MODEL), F32),
        compiler_params=_cparams(("parallel", "parallel")),
        name="adaln",
    )(cond, w_mod, b_mod.reshape(n_l, 1, N_MOD * D_MODEL))


def _premix_kernel(x_ref, sh_ref, sc_ref, g_ref, w_ref, zg_ref, zm_ref, h_scr):
    j = pl.program_id(1)

    @pl.when(j == 0)
    def _():
        h_scr[...] = _rms_mod(x_ref[...], g_ref[...], sc_ref[...], sh_ref[...]).astype(BF16)

    z = _mm(h_scr[...], w_ref[...])

    @pl.when(j < Z_GATE_W // Z_TN)
    def _():
        zg_ref[...] = z.astype(BF16)

    @pl.when(j >= Z_GATE_W // Z_TN)
    def _():
        zm_ref[...] = z


def _premix(x, mod3, gain, w, row_of_tile, tm):
    nt = x.shape[0]
    n_gate = Z_GATE_W // Z_TN
    return pl.pallas_call(
        _premix_kernel,
        grid=(nt // tm, Z_W // Z_TN),
        in_specs=[
            pl.BlockSpec((tm, D_MODEL), lambda i, j: (i, 0)),
            pl.BlockSpec((None, 1, D_MODEL), lambda i, j: (row_of_tile(i), 0, 0)),
            pl.BlockSpec((None, 1, D_MODEL), lambda i, j: (row_of_tile(i), 0, 1)),
            pl.BlockSpec((1, D_MODEL), lambda i, j: (0, 0)),
            pl.BlockSpec((D_MODEL, Z_TN), lambda i, j: (0, j)),
        ],
        out_specs=[
            pl.BlockSpec((tm, Z_TN), lambda i, j: (i, jnp.minimum(j, n_gate - 1))),
            pl.BlockSpec((tm, Z_TN), lambda i, j: (i, jnp.maximum(j - n_gate, 0))),
        ],
        out_shape=[
            jax.ShapeDtypeStruct((nt, Z_GATE_W), BF16),
            jax.ShapeDtypeStruct((nt, Z_MIX_W), F32),
        ],
        scratch_shapes=[pltpu.VMEM((tm, D_MODEL), BF16)],
        compiler_params=_cparams(("parallel", "arbitrary")),
        name="premix",
    )(x, mod3, mod3, gain, w)


def _gla_kernel(z_ref, wd_ref, bd_ref, s0_ref, e_ref, ind_ref, m_ref, sfin_ref,
                la_scr, of_scr, st_scr, p_scr, cum_scr, k_scr, v_scr):
    t_len = z_ref.shape[0]
    c = GLA_CHUNK
    n_chunks = t_len // c
    hk = GLA_HEADS * GLA_DK
    hv = GLA_HEADS * GLA_DV
    scale = GLA_DK ** -0.5

    pre = _mm_hi(z_ref[:, 768:896], wd_ref[...]) + bd_ref[...]
    la_scr[...] = _log_sigmoid(pre) * (1.0 / GLA_TAU)

    ri = lax.broadcasted_iota(jnp.int32, (c, c), 0)
    ci = lax.broadcasted_iota(jnp.int32, (c, c), 1)
    tri_lo = (ri >= ci).astype(BF16)
    tri_up = (ri <= ci).astype(BF16)
    row = lax.broadcasted_iota(jnp.int32, (c, hk), 0)
    bd_mask = (lax.broadcasted_iota(jnp.int32, (hv, hk), 0) // GLA_DV
               == lax.broadcasted_iota(jnp.int32, (hv, hk), 1) // GLA_DK).astype(F32)

    def chunk(base, la, tri, reverse, slot):
        cum = _mm_exact_lhs(tri, la)
        q = z_ref[pl.ds(base, c), 0:128] * scale
        k = z_ref[pl.ds(base, c), 128:256]
        v = z_ref[pl.ds(base, c), 256:512]
        edge = cum[0:1, :] if reverse else cum[c - 1:c, :]
        st = st_scr[slot]
        o = _nt((q * jnp.exp(cum)).astype(BF16), st.astype(BF16))
        ke = k * jnp.exp(edge - cum)
        cum_scr[slot] = cum
        k_scr[slot] = k
        v_scr[slot] = v
        for j in range(c):
            d = jnp.minimum(cum - cum_scr[slot, j:j + 1, :], 0.0)
            p = q * k_scr[slot, j:j + 1, :] * jnp.exp(d)
            keep = (row <= j) if reverse else (row >= j)
            p_scr[slot, j * c:(j + 1) * c, :] = jnp.where(keep, p, 0.0).astype(BF16)
        pe = _mm(p_scr[slot], e_ref[...])
        for j in range(c):
            o = o + pe[j * c:(j + 1) * c, :] * v_scr[slot, j:j + 1, :]
        st_scr[slot] = st * jnp.exp(edge) + bd_mask * _tn(v.astype(BF16), ke.astype(BF16))
        return o

    st_scr[0] = s0_ref[0]
    st_scr[1] = s0_ref[1]

    def sweep(i, carry):
        bf = pl.multiple_of(i * c, c)
        bb = pl.multiple_of((n_chunks - 1 - i) * c, c)
        of_scr[pl.ds(bf, c), :] = chunk(bf, la_scr[pl.ds(bf, c), 0:128], tri_lo, False, 0)
        m_ref[pl.ds(bb, c), :] = chunk(bb, la_scr[pl.ds(bb, c), 128:256], tri_up, True, 1)
        return carry

    lax.fori_loop(0, n_chunks, sweep, 0)
    sfin_ref[0] = st_scr[0]
    sfin_ref[1] = st_scr[1]

    fb = 8 * c

    def finish(i, carry):
        base = pl.multiple_of(i * fb, fb)
        o = of_scr[pl.ds(base, fb), :] + m_ref[pl.ds(base, fb), :]
        ms = _mm_exact_rhs(o * o, ind_ref[...])
        g = z_ref[pl.ds(base, fb), 512:768]
        m_ref[pl.ds(base, fb), :] = o * lax.rsqrt(ms + EPS) * _silu(g)
        return carry

    lax.fori_loop(0, t_len // fb, finish, 0)


def _gla(z, row_off, n_b, t_len, wd, bd, s0t, e_mat, ind):
    hk, hv = GLA_HEADS * GLA_DK, GLA_HEADS * GLA_DV
    c = GLA_CHUNK
    return pl.pallas_call(
        _gla_kernel,
        grid=(n_b,),
        in_specs=[
            pl.BlockSpec((t_len, 1024), lambda b: (row_off + b, Z_GLA // 1024)),
            pl.BlockSpec((128, 256), lambda b: (0, 0)),
            pl.BlockSpec((1, 256), lambda b: (0, 0)),
            pl.BlockSpec((None, 2, hv, hk), lambda b: (b, 0, 0, 0)),
            pl.BlockSpec((hk, hv), lambda b: (0, 0)),
            pl.BlockSpec((hv, hv), lambda b: (0, 0)),
        ],
        out_specs=[
            pl.BlockSpec((t_len, BR_W), lambda b: (b, 0)),
            pl.BlockSpec((None, 2, hv, hk), lambda b: (b, 0, 0, 0)),
        ],
        out_shape=[
            jax.ShapeDtypeStruct((n_b * t_len, BR_W), F32),
            jax.ShapeDtypeStruct((n_b, 2, hv, hk), F32),
        ],
        scratch_shapes=[
            pltpu.VMEM((t_len, 256), F32),
            pltpu.VMEM((t_len, hv), F32),
            pltpu.VMEM((2, hv, hk), F32),
            pltpu.VMEM((2, c * c, hk), BF16),
            pltpu.VMEM((2, c, hk), F32),
            pltpu.VMEM((2, c, hk), F32),
            pltpu.VMEM((2, c, hv), F32),
        ],
        compiler_params=_cparams(("parallel",)),
        name="gla",
    )(z, wd, bd, s0t, e_mat, ind)


def _ret_kernel(z_ref, cos_ref, sin_ref, swap_ref, lgl_ref, lgs_ref, s0_ref, ind_ref, m_ref, sfin_ref,
                qk_scr, of_scr, st_scr, *, rope):
    t_len = z_ref.shape[0]
    c = RET_CHUNK
    n_chunks = t_len // c
    hk = RET_HEADS * RET_DK
    hv = RET_HEADS * RET_DV
    scale = RET_DK ** -0.5

    lgf = lgl_ref[0:1, :]
    lgb = lgl_ref[1:2, :]
    pos = lax.broadcasted_iota(jnp.int32, (c, hk), 0).astype(F32)
    wq_f = jnp.exp(lgf * (pos + 1.0))
    wk_f = jnp.exp(lgf * (c - 1.0 - pos))
    wq_b = jnp.exp(lgb * (c - pos))
    wk_b = jnp.exp(lgb * pos)
    dec_f = jnp.exp(lgf * float(c))
    dec_b = jnp.exp(lgb * float(c))

    ii = lax.broadcasted_iota(jnp.int32, (c, c), 0)
    jj = lax.broadcasted_iota(jnp.int32, (c, c), 1)
    rel = (ii - jj).astype(F32)
    dms = []
    for h in range(RET_HEADS):
        d_f = jnp.where(ii >= jj, jnp.exp(lgs_ref[0, h] * jnp.maximum(rel, 0.0)), 0.0)
        d_b = jnp.where(jj >= ii, jnp.exp(lgs_ref[1, h] * jnp.maximum(-rel, 0.0)), 0.0)
        dms.append(d_f + d_b)
    dmat = jnp.concatenate(dms, axis=1)

    ek_mask = (lax.broadcasted_iota(jnp.int32, (RET_HEADS * c, hk), 0) // c
               == lax.broadcasted_iota(jnp.int32, (RET_HEADS * c, hk), 1) // RET_DK).astype(F32)
    ev_mask = (lax.broadcasted_iota(jnp.int32, (RET_HEADS * c, hv), 0) // c
               == lax.broadcasted_iota(jnp.int32, (RET_HEADS * c, hv), 1) // RET_DV).astype(F32)
    bd_mask = (lax.broadcasted_iota(jnp.int32, (hv, hk), 0) // RET_DV
               == lax.broadcasted_iota(jnp.int32, (hv, hk), 1) // RET_DK).astype(F32)

    st_scr[...] = s0_ref[0]

    def fwd(i, carry):
        base = pl.multiple_of(i * c, c)
        q = z_ref[pl.ds(base, c), 0:128] * scale
        k = z_ref[pl.ds(base, c), 128:256]
        v = z_ref[pl.ds(base, c), 256:512]
        if rope:
            cs = cos_ref[pl.ds(base, c), :]
            sn = sin_ref[pl.ds(base, c), :]
            q = q * cs + _mm_exact_rhs(q, swap_ref[...]) * sn
            k = k * cs + _mm_exact_rhs(k, swap_ref[...]) * sn
        qk_scr[pl.ds(base, c), 0:128] = q
        qk_scr[pl.ds(base, c), 128:256] = k
        kexp = (jnp.concatenate([k] * RET_HEADS, axis=0) * ek_mask).astype(BF16)
        vexp = (jnp.concatenate([v] * RET_HEADS, axis=0) * ev_mask).astype(BF16)
        sc = _nt(q.astype(BF16), kexp) * dmat
        o = _mm(sc.astype(BF16), vexp)
        st = st_scr[...]
        o = o + _nt((q * wq_f).astype(BF16), st.astype(BF16))
        of_scr[pl.ds(base, c), :] = o
        st_scr[...] = st * dec_f + bd_mask * _tn(v.astype(BF16), (k * wk_f).astype(BF16))
        return carry

    lax.fori_loop(0, n_chunks, fwd, 0)
    sfin_ref[0] = st_scr[...]
    st_scr[...] = s0_ref[1]

    def bwd(i, carry):
        base = pl.multiple_of((n_chunks - 1 - i) * c, c)
        q = qk_scr[pl.ds(base, c), 0:128]
        k = qk_scr[pl.ds(base, c), 128:256]
        v = z_ref[pl.ds(base, c), 256:512]
        st = st_scr[...]
        o = of_scr[pl.ds(base, c), :] + _nt((q * wq_b).astype(BF16), st.astype(BF16))
        st_scr[...] = st * dec_b + bd_mask * _tn(v.astype(BF16), (k * wk_b).astype(BF16))
        ms = _mm_exact_rhs(o * o, ind_ref[...])
        g = z_ref[pl.ds(base, c), 512:768]
        m_ref[pl.ds(base, c), :] = o * lax.rsqrt(ms + EPS) * _silu(g)
        return carry

    lax.fori_loop(0, n_chunks, bwd, 0)
    sfin_ref[1] = st_scr[...]


def _ret(z, row_off, n_b, t_len, cos_t, sin_t, swap, lgl, lgs, s0t, ind, rope):
    hk, hv = RET_HEADS * RET_DK, RET_HEADS * RET_DV
    return pl.pallas_call(
        functools.partial(_ret_kernel, rope=rope),
        grid=(n_b,),
        in_specs=[
            pl.BlockSpec((t_len, 1024), lambda b: (row_off + b, Z_RET // 1024)),
            pl.BlockSpec((t_len, hk), lambda b: (0, 0)),
            pl.BlockSpec((t_len, hk), lambda b: (0, 0)),
            pl.BlockSpec((hk, hk), lambda b: (0, 0)),
            pl.BlockSpec((2, hk), lambda b: (0, 0)),
            pl.BlockSpec(memory_space=pltpu.SMEM),
            pl.BlockSpec((None, 2, hv, hk), lambda b: (b, 0, 0, 0)),
            pl.BlockSpec((hv, hv), lambda b: (0, 0)),
        ],
        out_specs=[
            pl.BlockSpec((t_len, BR_W), lambda b: (b, 0)),
            pl.BlockSpec((None, 2, hv, hk), lambda b: (b, 0, 0, 0)),
        ],
        out_shape=[
            jax.ShapeDtypeStruct((n_b * t_len, BR_W), F32),
            jax.ShapeDtypeStruct((n_b, 2, hv, hk), F32),
        ],
        scratch_shapes=[
            pltpu.VMEM((t_len, 2 * hk), F32),
            pltpu.VMEM((t_len, hv), F32),
            pltpu.VMEM((hv, hk), F32),
        ],
        compiler_params=_cparams(("parallel",)),
        name="ret",
    )(z, cos_t, sin_t, swap, lgl, lgs, s0t, ind)


def _s5_disc_kernel(are_ref, aim_ref, ldt_ref, coef_ref, pf_ref, pb_ref):
    tb = pf_ref.shape[0]
    re = jnp.minimum(are_ref[...], S5_RE_MAX)
    im = aim_ref[...]
    dt = jnp.exp(ldt_ref[...])
    er = jnp.exp(re * dt)
    lbr = er * jnp.cos(im * dt)
    lbi = er * jnp.sin(im * dt)
    den = re * re + im * im
    nr = lbr - 1.0
    coef_ref[:, 0:S5_CH] = (nr * re + lbi * im) / den
    coef_ref[:, S5_CH:] = (lbi * re - nr * im) / den
    t = lax.broadcasted_iota(jnp.int32, (tb, S5_CH), 0).astype(F32)
    nf = t + 1.0
    nb = float(tb) - t
    mf = jnp.exp(nf * (re[0:1] * dt[0:1]))
    pf_ref[:, 0:S5_CH] = mf * jnp.cos(nf * (im[0:1] * dt[0:1]))
    pf_ref[:, S5_CH:] = mf * jnp.sin(nf * (im[0:1] * dt[0:1]))
    mb = jnp.exp(nb * (re[1:2] * dt[1:2]))
    pb_ref[:, 0:S5_CH] = mb * jnp.cos(nb * (im[1:2] * dt[1:2]))
    pb_ref[:, S5_CH:] = mb * jnp.sin(nb * (im[1:2] * dt[1:2]))


def _s5_disc(are, aim, ldt, tb):
    return pl.pallas_call(
        _s5_disc_kernel,
        out_shape=[
            jax.ShapeDtypeStruct((2, 2 * S5_CH), F32),
            jax.ShapeDtypeStruct((tb, 2 * S5_CH), F32),
            jax.ShapeDtypeStruct((tb, 2 * S5_CH), F32),
        ],
        compiler_params=pltpu.CompilerParams(vmem_limit_bytes=VMEM_LIMIT),
        name="s5_disc",
    )(are, aim, ldt)


def _s5_kernel(u_ref, bbd_ref, cre_ref, cim_ref, coef_ref, pf_ref, pb_ref, dsk_ref, wglu_ref, h0_ref,
               m_ref, hfin_ref, y_scr, bu_scr, hr_scr, hi_scr, car_scr):
    t_len = u_ref.shape[0]
    tb = SCAN_TB
    sub = pf_ref.shape[0]
    n_blocks = t_len // tb
    row_in = lax.broadcasted_iota(jnp.int32, (tb, LANES), 0) % sub
    steps = [1 << s for s in range(int(math.log2(sub)))]

    def block(base, dr_i, reverse):
        p_ref = pb_ref if reverse else pf_ref
        u = u_ref[pl.ds(base, tb), :]
        bu_scr[...] = _mm(u.astype(BF16), bbd_ref[...])
        for g in range(S5_CH // LANES):
            lo, hi = g * LANES, (g + 1) * LANES
            br = bu_scr[:, lo:hi]
            bi = bu_scr[:, S5_CH + lo:S5_CH + hi]
            cr = coef_ref[dr_i:dr_i + 1, lo:hi]
            ci = coef_ref[dr_i:dr_i + 1, S5_CH + lo:S5_CH + hi]
            hr = cr * br - ci * bi
            hi_ = cr * bi + ci * br
            for d in steps:
                if reverse:
                    pr = p_ref[sub - d:sub - d + 1, lo:hi]
                    pi = p_ref[sub - d:sub - d + 1, S5_CH + lo:S5_CH + hi]
                    keep = row_in < sub - d
                    sr = jnp.where(keep, pltpu.roll(hr, tb - d, 0), 0.0)
                    si = jnp.where(keep, pltpu.roll(hi_, tb - d, 0), 0.0)
                else:
                    pr = p_ref[d - 1:d, lo:hi]
                    pi = p_ref[d - 1:d, S5_CH + lo:S5_CH + hi]
                    keep = row_in >= d
                    sr = jnp.where(keep, pltpu.roll(hr, d, 0), 0.0)
                    si = jnp.where(keep, pltpu.roll(hi_, d, 0), 0.0)
                hr, hi_ = hr + pr * sr - pi * si, hi_ + pr * si + pi * sr
            car = car_scr[0:1, lo:hi]
            cai = car_scr[0:1, S5_CH + lo:S5_CH + hi]
            pwr = p_ref[:, lo:hi]
            pwi = p_ref[:, S5_CH + lo:S5_CH + hi]
            n_grp = tb // sub
            for v in (range(n_grp - 1, -1, -1) if reverse else range(n_grp)):
                gr = hr[v * sub:(v + 1) * sub, :] + pwr * car - pwi * cai
                gi = hi_[v * sub:(v + 1) * sub, :] + pwr * cai + pwi * car
                hr_scr[v * sub:(v + 1) * sub, lo:hi] = gr
                hi_scr[v * sub:(v + 1) * sub, lo:hi] = gi
                edge = 0 if reverse else sub - 1
                car, cai = gr[edge:edge + 1, :], gi[edge:edge + 1, :]
            car_scr[0:1, lo:hi] = car
            car_scr[0:1, S5_CH + lo:S5_CH + hi] = cai
        y = _mm(hr_scr[...].astype(BF16), cre_ref[...]) - _mm(hi_scr[...].astype(BF16), cim_ref[...])
        return u, y

    car_scr[0:1, :] = h0_ref[0:1, :]

    def fwd(i, carry):
        base = pl.multiple_of(i * tb, tb)
        _, y = block(base, 0, False)
        y_scr[pl.ds(base, tb), :] = y
        return carry

    lax.fori_loop(0, n_blocks, fwd, 0)
    hfin_ref[0:1, :] = car_scr[0:1, :]
    car_scr[0:1, :] = h0_ref[1:2, :]

    def bwd(i, carry):
        base = pl.multiple_of((n_blocks - 1 - i) * tb, tb)
        u, y = block(base, 1, True)
        y = _gelu(y_scr[pl.ds(base, tb), :] + y + dsk_ref[...] * u)
        gg = _mm(y.astype(BF16), wglu_ref[...])
        m_ref[pl.ds(base, tb), :] = gg[:, 0:BR_W] * _sigmoid(gg[:, BR_W:])
        return carry

    lax.fori_loop(0, n_blocks, bwd, 0)
    hfin_ref[1:2, :] = car_scr[0:1, :]


def _s5(z, row_off, n_b, t_len, bbd, cre, cim, coef, pf, pb, dsk, wglu, h0):
    tb = SCAN_TB
    sub = pf.shape[0]
    full = lambda shape: pl.BlockSpec(shape, lambda b: (0,) * len(shape))
    return pl.pallas_call(
        _s5_kernel,
        grid=(n_b,),
        in_specs=[
            pl.BlockSpec((t_len, BR_W), lambda b: (row_off + b, Z_S5 // BR_W)),
            full((BR_W, 2 * S5_CH)),
            full((S5_CH, BR_W)),
            full((S5_CH, BR_W)),
            full((2, 2 * S5_CH)),
            full((sub, 2 * S5_CH)),
            full((sub, 2 * S5_CH)),
            full((1, BR_W)),
            full((BR_W, 2 * BR_W)),
            pl.BlockSpec((None, 2, 2 * S5_CH), lambda b: (b, 0, 0)),
        ],
        out_specs=[
            pl.BlockSpec((t_len, BR_W), lambda b: (b, 0)),
            pl.BlockSpec((None, 2, 2 * S5_CH), lambda b: (b, 0, 0)),
        ],
        out_shape=[
            jax.ShapeDtypeStruct((n_b * t_len, BR_W), F32),
            jax.ShapeDtypeStruct((n_b, 2, 2 * S5_CH), F32),
        ],
        scratch_shapes=[
            pltpu.VMEM((t_len, BR_W), F32),
            pltpu.VMEM((tb, 2 * S5_CH), F32),
            pltpu.VMEM((tb, S5_CH), F32),
            pltpu.VMEM((tb, S5_CH), F32),
            pltpu.VMEM((8, 2 * S5_CH), F32),
        ],
        compiler_params=_cparams(("parallel",)),
        name="s5",
    )(z, bbd, cre, cim, coef, pf, pb, dsk, wglu, h0)


def _lru_kernel(z_ref, cw_ref, cb_ref, wg_ref, bg_ref, lam_ref, h0_ref, m_ref, hfin_ref,
                xc_scr, hf_scr, car_scr):
    t_len = z_ref.shape[0]
    tb = SCAN_TB
    n_blocks = t_len // tb
    steps = [1 << s for s in range(int(math.log2(tb)))]

    x = z_ref[:, 0:BR_W]
    trow = lax.broadcasted_iota(jnp.int32, (t_len, BR_W), 0)
    xm1 = jnp.where(trow >= 1, pltpu.roll(x, 1, 0), 0.0)
    xp1 = jnp.where(trow < t_len - 1, pltpu.roll(x, t_len - 1, 0), 0.0)
    xp2 = jnp.where(trow < t_len - 2, pltpu.roll(x, t_len - 2, 0), 0.0)
    xc_scr[...] = (cw_ref[0:1, :] * xm1 + cw_ref[1:2, :] * x + cw_ref[2:3, :] * xp1
                   + cw_ref[3:4, :] * xp2 + cb_ref[...])

    row = lax.broadcasted_iota(jnp.int32, (tb, BR_W), 0)
    sp = _softplus(-lam_ref[...])

    def block(base, dr_i, reverse):
        xc = xc_scr[pl.ds(base, tb), :]
        off = dr_i * 2 * BR_W
        gates = _mm(xc.astype(BF16), wg_ref[:, off:off + 2 * BR_W]) + bg_ref[:, off:off + 2 * BR_W]
        r = _sigmoid(gates[:, 0:BR_W])
        ig = _sigmoid(gates[:, BR_W:])
        log_a = -LRU_C * r * sp[dr_i:dr_i + 1, :]
        a = jnp.exp(log_a)
        th = jnp.tanh(log_a)
        b = jnp.sqrt(-2.0 * th / (1.0 - th)) * (ig * xc)
        for d in steps:
            if reverse:
                keep = row < tb - d
                a_s = jnp.where(keep, pltpu.roll(a, tb - d, 0), 1.0)
                b_s = jnp.where(keep, pltpu.roll(b, tb - d, 0), 0.0)
            else:
                keep = row >= d
                a_s = jnp.where(keep, pltpu.roll(a, d, 0), 1.0)
                b_s = jnp.where(keep, pltpu.roll(b, d, 0), 0.0)
            b = b + a * b_s
            a = a * a_s
        h = b + a * car_scr[dr_i:dr_i + 1, :]
        edge = 0 if reverse else tb - 1
        car_scr[dr_i:dr_i + 1, :] = h[edge:edge + 1, :]
        return h

    car_scr[0:2, :] = h0_ref[...]

    def fwd(i, carry):
        base = pl.multiple_of(i * tb, tb)
        hf_scr[pl.ds(base, tb), :] = block(base, 0, False)
        return carry

    lax.fori_loop(0, n_blocks, fwd, 0)

    def bwd(i, carry):
        base = pl.multiple_of((n_blocks - 1 - i) * tb, tb)
        h = block(base, 1, True) + hf_scr[pl.ds(base, tb), :]
        m_ref[pl.ds(base, tb), :] = h * _gelu(z_ref[pl.ds(base, tb), BR_W:2 * BR_W])
        return carry

    lax.fori_loop(0, n_blocks, bwd, 0)
    hfin_ref[...] = car_scr[0:2, :]


def _lru(z, row_off, n_b, t_len, cw, cb, wg, bg, lam, h0):
    full = lambda shape: pl.BlockSpec(shape, lambda b: (0,) * len(shape))
    return pl.pallas_call(
        _lru_kernel,
        grid=(n_b,),
        in_specs=[
            pl.BlockSpec((t_len, 2 * BR_W), lambda b: (row_off + b, Z_LRU // (2 * BR_W))),
            full((4, BR_W)),
            full((1, BR_W)),
            full((BR_W, 4 * BR_W)),
            full((1, 4 * BR_W)),
            full((2, BR_W)),
            pl.BlockSpec((None, 2, BR_W), lambda b: (b, 0, 0)),
        ],
        out_specs=[
            pl.BlockSpec((t_len, BR_W), lambda b: (b, 0)),
            pl.BlockSpec((None, 2, BR_W), lambda b: (b, 0, 0)),
        ],
        out_shape=[
            jax.ShapeDtypeStruct((n_b * t_len, BR_W), F32),
            jax.ShapeDtypeStruct((n_b, 2, BR_W), F32),
        ],
        scratch_shapes=[
            pltpu.VMEM((t_len, BR_W), F32),
            pltpu.VMEM((t_len, BR_W), F32),
            pltpu.VMEM((8, BR_W), F32),
        ],
        compiler_params=_cparams(("parallel",)),
        name="lru",
    )(z, cw, cb, wg, bg, lam, h0)


def _merge_kernel(ma_ref, mb_ref, mc_ref, md_ref, zg_ref, x_ref, g1_ref, sh2_ref, sc2_ref, gain_ref,
                  wb_ref, wo_ref, xo_ref, h2t_ref):
    acc = None
    for n, m_ref in enumerate((ma_ref, mb_ref, mc_ref, md_ref)):
        proj = _mm(m_ref[...].astype(BF16), wb_ref[n])
        term = _sigmoid(zg_ref[:, n * D_MODEL:(n + 1) * D_MODEL].astype(F32)) * proj
        acc = term if acc is None else acc + term
    xn = x_ref[...] + g1_ref[...] * _mm(acc.astype(BF16), wo_ref[...])
    xo_ref[...] = xn
    h2t_ref[...] = _rms_mod(xn, gain_ref[...], sc2_ref[...], sh2_ref[...]).T.astype(BF16)


def _merge(ms, z, x, mod3, gain, wb, wo, row_of_tile, tm):
    nt = x.shape[0]
    modspec = lambda k: pl.BlockSpec((None, 1, D_MODEL), lambda i: (row_of_tile(i), 0, k))
    return pl.pallas_call(
        _merge_kernel,
        grid=(nt // tm,),
        in_specs=[pl.BlockSpec((tm, BR_W), lambda i: (i, 0))] * 4 + [
            pl.BlockSpec((tm, 4 * D_MODEL), lambda i: (i, 0)),
            pl.BlockSpec((tm, D_MODEL), lambda i: (i, 0)),
            modspec(2), modspec(3), modspec(4),
            pl.BlockSpec((1, D_MODEL), lambda i: (0, 0)),
            pl.BlockSpec((4, BR_W, D_MODEL), lambda i: (0, 0, 0)),
            pl.BlockSpec((D_MODEL, D_MODEL), lambda i: (0, 0)),
        ],
        out_specs=[
            pl.BlockSpec((tm, D_MODEL), lambda i: (i, 0)),
            pl.BlockSpec((D_MODEL, tm), lambda i: (0, i)),
        ],
        out_shape=[
            jax.ShapeDtypeStruct((nt, D_MODEL), F32),
            jax.ShapeDtypeStruct((D_MODEL, nt), BF16),
        ],
        compiler_params=_cparams(("parallel",)),
        name="merge",
    )(*ms, z, x, mod3, mod3, mod3, gain, wb, wo)


def _peer_kernel(h2t_ref, x_ref, g2_ref, wqt_ref, kbt_ref, u_ref, vt_ref, xo_ref,
                 n_scr, r1_scr, a_scr, b_scr, sc_scr, v_scr, at_scr, wa_scr, yt_scr, *, te, ts):
    tm = h2t_ref.shape[1]
    n_lt = tm // LANES
    ic = te // PEER_KEYS
    n_sub = te // ts
    c_idx = pl.program_id(1)
    nk = PEER_KEYS

    @pl.when(c_idx == 0)
    def _route():
        qt = _mm(wqt_ref[...], h2t_ref[...])
        sct = _mm(kbt_ref[...], qt.astype(BF16))
        for lt in range(n_lt):
            sc_scr[lt] = sct[:, lt * LANES:(lt + 1) * LANES]
        r8 = lax.broadcasted_iota(jnp.int32, (8, LANES), 0)

        def per_tile(lt, carry):
            def per_head(h, vo):
                rank1 = jnp.full((nk, LANES), float(PEER_TOPK), F32)
                for s in range(2):
                    off = pl.multiple_of(h * (2 * nk) + s * nk, nk)
                    xs = sc_scr[lt, pl.ds(off, nk), :]
                    for r in range(PEER_TOPK):
                        m = jnp.max(xs, axis=0, keepdims=True)
                        v_scr[vo + s * PEER_TOPK + r:vo + s * PEER_TOPK + r + 1, :] = m
                        hit = xs == m
                        if s == 1:
                            rank1 = jnp.where(hit, float(r), rank1)
                        xs = jnp.where(hit, NEG_INF, xs)
                v0 = v_scr[vo:vo + PEER_TOPK, :]
                v1 = v_scr[vo + PEER_TOPK:vo + 2 * PEER_TOPK, :]
                pieces = [v0[0:1, :] + v1]
                for r0 in range(1, 8):
                    pieces.append(jnp.where(r8 < PEER_TOPK // (r0 + 1), v0[r0:r0 + 1, :] + v1[0:8, :], NEG_INF))
                pieces.append(v0[8:16, :] + v1[0:1, :])
                cand = jnp.concatenate(pieces, axis=0)
                top = v0[0:1, :] + v1[0:1, :]
                zsum = jnp.zeros_like(top)
                tau = top
                for r in range(PEER_TOPK):
                    tau = jnp.max(cand, axis=0, keepdims=True)
                    zsum = zsum + jnp.exp(tau - top)
                    cand = jnp.where(cand == tau, NEG_INF, cand)
                o0 = pl.multiple_of(h * (2 * nk), nk)
                o1 = pl.multiple_of(h * (2 * nk) + nk, nk)
                ho = pl.multiple_of(h * nk, nk)
                s0 = sc_scr[lt, pl.ds(o0, nk), :]
                s1 = sc_scr[lt, pl.ds(o1, nk), :]
                cnt = jnp.zeros((nk, LANES), F32)
                for r0 in range(PEER_TOPK):
                    ok = (v0[r0:r0 + 1, :] + v1) >= tau
                    c_r0 = jnp.sum(jnp.where(ok, 1.0, 0.0), axis=0, keepdims=True)
                    cnt = jnp.where(s0 == v0[r0:r0 + 1, :], c_r0, cnt)
                n_scr[lt, pl.ds(ho, nk), :] = cnt
                r1_scr[lt, pl.ds(ho, nk), :] = rank1.astype(BF16)
                a_scr[lt, pl.ds(ho, nk), :] = jnp.exp(s0 - v0[0:1, :])
                b_scr[lt, pl.ds(ho, nk), :] = (jnp.exp(s1 - v1[0:1, :]) / zsum).astype(BF16)

            def per_pair(hp, carry2):
                per_head(2 * hp, 0)
                per_head(2 * hp + 1, 2 * PEER_TOPK)
                return carry2

            return lax.fori_loop(0, PEER_HEADS // 2, per_pair, carry)

        lax.fori_loop(0, n_lt, per_tile, 0)
        yt_scr[...] = jnp.zeros_like(yt_scr)

    groups = ts // nk

    def a_stage(k):
        at_scr[k % 2] = _mm(u_ref[k * ts:(k + 1) * ts, :], h2t_ref[...]).astype(BF16)

    def y_stage(k):
        yt_scr[...] += _mm(vt_ref[:, k * ts:(k + 1) * ts], wa_scr[k % 2])

    def w_block(k, lt, ii, dep):
        pk = BF16_ROWS
        nv = nk // pk
        cols = slice(lt * LANES, (lt + 1) * LANES)
        acc = [None] * nv
        for h in range(PEER_HEADS):
            row = h * nk + c_idx * ic + k * groups + ii
            n_t = _row_tile_bf16(n_scr[lt, pl.ds(row, 1), :] + dep, pk)
            a_t = _row_tile_bf16(a_scr[lt, pl.ds(row, 1), :] + dep, pk)
            for jv in range(nv):
                r1 = r1_scr[lt, h * nk + jv * pk:h * nk + (jv + 1) * pk, :]
                b1 = b_scr[lt, h * nk + jv * pk:h * nk + (jv + 1) * pk, :]
                term = jnp.where(r1 < n_t, b1, 0.0) * a_t
                acc[jv] = term if acc[jv] is None else acc[jv] + term
        out = None
        for jv in range(nv):
            rows = slice(ii * nk + jv * pk, ii * nk + (jv + 1) * pk)
            out = acc[jv] * _gelu_bf16(at_scr[k % 2, rows, cols])
            wa_scr[k % 2, rows, cols] = out
        last = out[0:1, :].astype(F32)
        return jnp.where((last < 2.0) & (last > -2.0), last, 1.0) * 0.0

    dep = jnp.zeros((1, LANES), F32)
    a_stage(0)
    for k in range(n_sub):
        if k + 1 < n_sub:
            a_stage(k + 1)
        if k >= 1:
            y_stage(k - 1)
        for lt in range(n_lt):
            for ii in range(groups):
                dep = w_block(k, lt, ii, dep)
    y_stage(n_sub - 1)

    @pl.when(c_idx == pl.num_programs(1) - 1)
    def _fin():
        xo_ref[...] = x_ref[...] + g2_ref[...] * yt_scr[...].T


def _peer(h2t, x, mod3, wqt, kbt, u, vt, row_of_tile, tm, te, ts):
    nt = x.shape[0]
    n_lt = tm // LANES
    n_rt = PEER_HEADS * PEER_KEYS
    return pl.pallas_call(
        functools.partial(_peer_kernel, te=te, ts=ts),
        grid=(nt // tm, PEER_EXPERTS // te),
        in_specs=[
            pl.BlockSpec((D_MODEL, tm), lambda i, c: (0, i), pipeline_mode=pl.Buffered(1)),
            pl.BlockSpec((tm, D_MODEL), lambda i, c: (i, 0), pipeline_mode=pl.Buffered(1)),
            pl.BlockSpec((None, 1, D_MODEL), lambda i, c: (row_of_tile(i), 0, 5)),
            pl.BlockSpec((D_MODEL, D_MODEL), lambda i, c: (0, 0), pipeline_mode=pl.Buffered(1)),
            pl.BlockSpec((2 * n_rt, D_MODEL), lambda i, c: (0, 0), pipeline_mode=pl.Buffered(1)),
            pl.BlockSpec((te, D_MODEL), lambda i, c: (c, 0)),
            pl.BlockSpec((None, D_MODEL, te), lambda i, c: (c, 0, 0)),
        ],
        out_specs=pl.BlockSpec((tm, D_MODEL), lambda i, c: (i, 0)),
        out_shape=jax.ShapeDtypeStruct((nt, D_MODEL), F32),
        scratch_shapes=[
            pltpu.VMEM((n_lt, n_rt, LANES), F32),
            pltpu.VMEM((n_lt, n_rt, LANES), BF16),
            pltpu.VMEM((n_lt, n_rt, LANES), F32),
            pltpu.VMEM((n_lt, n_rt, LANES), BF16),
            pltpu.VMEM((n_lt, 2 * n_rt, LANES), F32),
            pltpu.VMEM((4 * PEER_TOPK, LANES), F32),
            pltpu.VMEM((2, ts, tm), BF16),
            pltpu.VMEM((2, ts, tm), BF16),
            pltpu.VMEM((D_MODEL, tm), F32),
        ],
        compiler_params=_cparams(("parallel", "arbitrary")),
        name="peer",
    )(h2t, x, mod3, wqt, kbt, u, vt)


def _final_kernel(x_ref, g_ref, o_ref):
    x = x_ref[...]
    o_ref[...] = x * lax.rsqrt(jnp.mean(x * x, axis=-1, keepdims=True) + EPS) * g_ref[...]


def _final_norm(x, gain, tm):
    nt = x.shape[0]
    return pl.pallas_call(
        _final_kernel,
        grid=(nt // tm,),
        in_specs=[pl.BlockSpec((tm, D_MODEL), lambda i: (i, 0)), pl.BlockSpec((1, D_MODEL), lambda i: (0, 0))],
        out_specs=pl.BlockSpec((tm, D_MODEL), lambda i: (i, 0)),
        out_shape=jax.ShapeDtypeStruct((nt, D_MODEL), F32),
        compiler_params=_cparams(("parallel",)),
        name="final_norm",
    )(x, gain)


def _block_diag(blocks):
    n, r, c = blocks.shape
    eye = jnp.eye(n, dtype=blocks.dtype)
    return jnp.einsum('nrc,nm->nrmc', blocks, eye).reshape(n * r, n * c)


def _state_to_bd_t(s):
    b, two, h, k, v = s.shape
    eye = jnp.eye(h, dtype=s.dtype)
    return jnp.einsum('bdhkv,hg->bdhvgk', s, eye).reshape(b, two, h * v, h * k)


def _bd_t_to_state(st, h, k, v):
    b = st.shape[0]
    return jnp.einsum('bdhvhk->bdhkv', st.reshape(b, 2, h, v, h, k))


def _rope_tables(t_len):
    rows = t_len // GRID_W
    row = jnp.repeat(jnp.arange(rows), GRID_W).astype(F32)
    col = jnp.tile(jnp.arange(GRID_W), rows).astype(F32)
    n_freq = RET_DK // 4
    inv_freq = ROPE_BASE ** (-jnp.arange(n_freq, dtype=F32) / n_freq)
    ang = jnp.concatenate([row[:, None] * inv_freq, col[:, None] * inv_freq], axis=-1)
    cos, sin = jnp.cos(ang), jnp.sin(ang)
    cos_h = jnp.concatenate([cos, cos], axis=-1)
    sin_h = jnp.concatenate([-sin, sin], axis=-1)
    return jnp.tile(cos_h, (1, RET_HEADS)), jnp.tile(sin_h, (1, RET_HEADS))


def _swap_matrix():
    lane = jnp.arange(RET_HEADS * RET_DK)
    half = RET_DK // 2
    src = jnp.where(lane % RET_DK < half, lane + half, lane - half)
    return (lane[:, None] == src[None, :]).astype(BF16)


def kernel(x_prompt, x_sample, c, state_gla, state_ret, state_s5, state_lru, c_ctx, w_mod, b_mod, norm_mix, norm_ffn, norm_final, w_in, gla_w_decay, gla_b_decay, ret_decay_logit, s5_a_re, s5_a_im, s5_log_dt, s5_b_re, s5_b_im, s5_c_re, s5_c_im, s5_d, s5_w_glu, lru_conv_w, lru_conv_b, lru_w_a, lru_b_a, lru_w_x, lru_b_x, lru_lambda, w_branch, w_out, peer_w_q, peer_keys, peer_u, peer_v):
    n_bp, t_p, _ = x_prompt.shape
    n_bs, t_s, _ = x_sample.shape
    depth = w_in.shape[0]
    ntp, nts = n_bp * t_p, n_bs * t_s
    tm = TOK_TM
    assert ntp % t_s == 0 and all(ntp % t == 0 and t_s % t == 0 for t in (TOK_TM, PRE_TM, PEER_TM))

    x = jnp.concatenate([x_prompt.reshape(ntp, D_MODEL), x_sample.reshape(nts, D_MODEL)], axis=0)

    n_rows = 8 * ((1 + n_bs + 7) // 8)
    cond = jnp.zeros((n_rows, D_MODEL), F32).at[0].set(c_ctx).at[1:1 + n_bs].set(c)
    mods = _adaln(cond, w_mod, b_mod)

    def make_row_of_tile(tile):
        def row_of_tile(i):
            return jnp.where(i < ntp // tile, 0, 1 + (i - ntp // tile) // (t_s // tile))
        return row_of_tile

    zpad = lambda n: jnp.zeros((depth, D_MODEL, n), F32)
    w_in_p = jnp.concatenate([w_in[:, :, 2336:6432], w_in[:, :, 0:800], zpad(224), w_in[:, :, 800:1568], zpad(256),
                              w_in[:, :, 1824:2336], w_in[:, :, 1568:1824], zpad(256)], axis=2).astype(BF16)
    assert w_in_p.shape[2] == Z_W

    hk, hv = GLA_HEADS * GLA_DK, GLA_HEADS * GLA_DV
    e_mat = (jnp.arange(hk)[:, None] // GLA_DK == jnp.arange(hv)[None, :] // GLA_DV).astype(BF16)
    ind = ((jnp.arange(hv)[:, None] // GLA_DV == jnp.arange(hv)[None, :] // GLA_DV).astype(F32) / GLA_DV).astype(BF16)
    swap = _swap_matrix()
    cos_s, sin_s = _rope_tables(t_s)
    cos_p, sin_p = jnp.ones((t_p, hk), F32), jnp.zeros((t_p, hk), F32)

    zeros_bd = jnp.zeros((n_bp, 2, hv, hk), F32)
    zeros_s5 = jnp.zeros((n_bp, 2, 2 * S5_CH), F32)
    zeros_lru = jnp.zeros((n_bp, 2, BR_W), F32)

    gla_l, ret_l, s5_l, lru_l = [], [], [], []
    for l in range(depth):
        mod3 = mods[l].reshape(n_rows, 1, N_MOD * D_MODEL)
        zg, z = _premix(x, mod3, norm_mix[l].reshape(1, D_MODEL), w_in_p[l], make_row_of_tile(PRE_TM), PRE_TM)

        wd = jnp.zeros((128, 256), F32)
        wd = wd.at[0:GLA_RANK, 0:hk].set(gla_w_decay[l, 0]).at[GLA_RANK:2 * GLA_RANK, hk:].set(gla_w_decay[l, 1])
        bd = gla_b_decay[l].reshape(1, 2 * hk)
        lg = jax.nn.log_sigmoid(ret_decay_logit[l].astype(F32))
        lgl = jnp.repeat(lg, RET_DK, axis=1)
        bre = _block_diag(jnp.swapaxes(s5_b_re[l], 1, 2))
        bim = _block_diag(jnp.swapaxes(s5_b_im[l], 1, 2))
        bbd = jnp.concatenate([bre, bim], axis=1).astype(BF16)
        cre = _block_diag(jnp.swapaxes(s5_c_re[l], 1, 2)).astype(BF16)
        cim = _block_diag(jnp.swapaxes(s5_c_im[l], 1, 2)).astype(BF16)
        coef, pf, pb = _s5_disc(s5_a_re[l].reshape(2, S5_CH), s5_a_im[l].reshape(2, S5_CH),
                                jnp.repeat(s5_log_dt[l], S5_STATE, axis=1), F32_ROWS)
        dsk = s5_d[l].reshape(1, BR_W)
        wglu = s5_w_glu[l].astype(BF16)
        wg = jnp.concatenate([_block_diag(lru_w_a[l, 0]), _block_diag(lru_w_x[l, 0]),
                              _block_diag(lru_w_a[l, 1]), _block_diag(lru_w_x[l, 1])], axis=1).astype(BF16)
        bg = jnp.concatenate([lru_b_a[l, 0], lru_b_x[l, 0], lru_b_a[l, 1], lru_b_x[l, 1]]).reshape(1, 4 * BR_W)
        cw = lru_conv_w[l]
        cb = lru_conv_b[l].reshape(1, BR_W)
        lam = lru_lambda[l]

        s5_h0 = state_s5[:, l].reshape(n_bs, 2, 2 * S5_CH)

        outs = []
        for (row_off, n_b, t_len, sg, sr, ss, sl, cs, sn, rope) in (
                (0, n_bp, t_p, zeros_bd, zeros_bd, zeros_s5, zeros_lru, cos_p, sin_p, False),
                (ntp // t_s, n_bs, t_s, _state_to_bd_t(state_gla[:, l]), _state_to_bd_t(state_ret[:, l]),
                 s5_h0, state_lru[:, l], cos_s, sin_s, True)):
            m_a, f_gla = _gla(z, row_off, n_b, t_len, wd, bd, sg, e_mat, ind)
            m_b, f_ret = _ret(z, row_off, n_b, t_len, cs, sn, swap, lgl, lg, sr, ind, rope)
            m_c, f_s5 = _s5(z, row_off, n_b, t_len, bbd, cre, cim, coef, pf, pb, dsk, wglu, ss)
            m_d, f_lru = _lru(z, row_off, n_b, t_len, cw, cb, wg, bg, lam, sl)
            outs.append(((m_a, m_b, m_c, m_d), (f_gla, f_ret, f_s5, f_lru)))

        ms = [jnp.concatenate([outs[0][0][n], outs[1][0][n]], axis=0) for n in range(4)]
        f_gla, f_ret, f_s5, f_lru = outs[0][1]
        gla_l.append(_bd_t_to_state(f_gla, GLA_HEADS, GLA_DK, GLA_DV))
        ret_l.append(_bd_t_to_state(f_ret, RET_HEADS, RET_DK, RET_DV))
        s5_l.append(f_s5.reshape(n_bp, 2, 2, S5_GROUPS, S5_STATE))
        lru_l.append(f_lru)

        x, h2t = _merge(ms, zg, x, mod3, norm_ffn[l].reshape(1, D_MODEL), w_branch[l].astype(BF16),
                        w_out[l].astype(BF16), make_row_of_tile(tm), tm)

        wqt = peer_w_q[l].T.astype(BF16)
        vt_l = peer_v[l].reshape(PEER_EXPERTS // PEER_TE, PEER_TE, D_MODEL).transpose(0, 2, 1).astype(BF16)
        kbt = _block_diag(peer_keys[l].reshape(2 * PEER_HEADS, PEER_KEYS, PEER_QDIM // 2)).astype(BF16)
        x = _peer(h2t, x, mod3, wqt, kbt, peer_u[l].astype(BF16), vt_l,
                  make_row_of_tile(PEER_TM), PEER_TM, PEER_TE, PEER_TS)

    y = _final_norm(x, norm_final.reshape(1, D_MODEL), tm)
    y_p = y[:ntp].reshape(n_bp, t_p, D_MODEL)
    y_s = y[ntp:].reshape(n_bs, t_s, D_MODEL)
    return (y_p, y_s, jnp.stack(gla_l, axis=1), jnp.stack(ret_l, axis=1),
            jnp.stack(s5_l, axis=1), jnp.stack(lru_l, axis=1))
```

```python
import functools
import math

import jax
import jax.numpy as jnp
from jax import lax
from jax.experimental import pallas as pl
from jax.experimental.pallas import tpu as pltpu

F32 = jnp.float32
BF16 = jnp.bfloat16
HI = lax.Precision.HIGHEST

D_MODEL = 1024
N_MOD = 6
EPS = 1e-6
BR_W = 256
GLA_HEADS, GLA_DK, GLA_DV, GLA_RANK, GLA_TAU, GLA_CHUNK = 4, 32, 64, 16, 16.0, 32
RET_HEADS, RET_DK, RET_DV, RET_CHUNK = 4, 32, 64, 64
ROPE_BASE = 10000.0
GRID_W = 64
S5_GROUP, S5_GROUPS, S5_STATE, S5_RE_MAX = 16, 16, 64, -1e-4
S5_CH = S5_GROUPS * S5_STATE
LRU_BLOCKS, LRU_BW, LRU_C = 4, 64, 8.0
PEER_HEADS, PEER_KEYS, PEER_TOPK, PEER_QDIM = 8, 128, 16, 128
PEER_EXPERTS = PEER_KEYS * PEER_KEYS

LANES = 128
VMEM_LIMIT = 56 * 1024 * 1024

Z_TN = 1024
Z_GATE_W = 4 * D_MODEL
Z_GLA, Z_RET, Z_LRU, Z_S5 = 0, 1024, 2048, 2560
Z_MIX_W = 3072
Z_W = Z_GATE_W + Z_MIX_W

SCAN_TB = 128
TOK_TM = 256
PRE_TM = 512
PEER_TM, PEER_TE, PEER_TS = 512, 2048, 512
NEG_INF = float("-inf")


def _cparams(sem):
    return pltpu.CompilerParams(dimension_semantics=sem, vmem_limit_bytes=VMEM_LIMIT)


def _nt(a, b):
    return lax.dot_general(a, b, (((1,), (1,)), ((), ())), preferred_element_type=F32)


def _tn(a, b):
    return lax.dot_general(a, b, (((0,), (0,)), ((), ())), preferred_element_type=F32)


def _mm(a, b):
    return jnp.dot(a, b, preferred_element_type=F32)


def _mm_hi(a, b):
    return jnp.dot(a, b, preferred_element_type=F32, precision=HI)


def _split_bf16(x):
    hi = x.astype(BF16)
    return hi, (x - hi.astype(F32)).astype(BF16)


def _mm_exact_rhs(a, b_exact):
    hi, lo = _split_bf16(a)
    return _mm(hi, b_exact) + _mm(lo, b_exact)


def _mm_exact_lhs(a_exact, b):
    hi, lo = _split_bf16(b)
    return _mm(a_exact, hi) + _mm(a_exact, lo)


def _sigmoid(x):
    return jax.nn.sigmoid(x)


def _silu(x):
    return x * jax.nn.sigmoid(x)


def _gelu(x):
    return jax.nn.gelu(x)


GELU_C0 = math.sqrt(2.0 / math.pi)
GELU_C1 = GELU_C0 * 0.044715
BF16_ROWS = 16
F32_ROWS = 8


def _gelu_bf16(x):
    hx = 0.5 * x
    return hx + hx * jnp.tanh(x * (GELU_C0 + GELU_C1 * (x * x)))


def _row_tile_bf16(row, n_rows):
    one = jnp.broadcast_to(row, (BF16_ROWS, row.shape[1])).astype(BF16)
    return jnp.concatenate([one] * (n_rows // BF16_ROWS), axis=0)


def _log_sigmoid(x):
    return jnp.minimum(x, 0.0) - jnp.log(1.0 + jnp.exp(-jnp.abs(x)))


def _softplus(x):
    return jnp.maximum(x, 0.0) + jnp.log(1.0 + jnp.exp(-jnp.abs(x)))


def _rms_mod(x, gain, sc, sh):
    ms = jnp.mean(x * x, axis=-1, keepdims=True)
    return x * lax.rsqrt(ms + EPS) * gain * (1.0 + sc) + sh


def _adaln_kernel(c_ref, w_ref, b_ref, o_ref):
    o_ref[...] = _mm_hi(_silu(c_ref[...]), w_ref[...]) + b_ref[...]


def _adaln(cond, w_mod, b_mod):
    n_l = w_mod.shape[0]
    rows = cond.shape[0]
    tn = 1536
    return pl.pallas_call(
        _adaln_kernel,
        grid=(n_l, N_MOD * D_MODEL // tn),
        in_specs=[
            pl.BlockSpec((rows, D_MODEL), lambda l, j: (0, 0)),
            pl.BlockSpec((None, D_MODEL, tn), lambda l, j: (l, 0, j)),
            pl.BlockSpec((None, 1, tn), lambda l, j: (l, 0, j)),
        ],
        out_specs=pl.BlockSpec((None, rows, tn), lambda l, j: (l, 0, j)),
        out_shape=jax.ShapeDtypeStruct((n_l, rows, N_MOD * D_MODEL), F32),
        compiler_params=_cparams(("parallel", "parallel")),
        name="adaln",
    )(cond, w_mod, b_mod.reshape(n_l, 1, N_MOD * D_MODEL))


def _premix_kernel(x_ref, sh_ref, sc_ref, g_ref, w_ref, zg_ref, zm_ref, h_scr):
    j = pl.program_id(1)

    @pl.when(j == 0)
    def _():
        h_scr[...] = _rms_mod(x_ref[...], g_ref[...], sc_ref[...], sh_ref[...]).astype(BF16)

    z = _mm(h_scr[...], w_ref[j])

    @pl.when(j < Z_GATE_W // Z_TN)
    def _():
        zg_ref[...] = z.astype(BF16)

    @pl.when(j >= Z_GATE_W // Z_TN)
    def _():
        zm_ref[...] = z


def _premix(x, mod3, gain, w, row_of_tile, tm):
    nt = x.shape[0]
    n_gate = Z_GATE_W // Z_TN
    return pl.pallas_call(
        _premix_kernel,
        grid=(nt // tm, Z_W // Z_TN),
        in_specs=[
            pl.BlockSpec((tm, D_MODEL), lambda i, j: (i, 0)),
            pl.BlockSpec((None, 1, D_MODEL), lambda i, j: (row_of_tile(i), 0, 0)),
            pl.BlockSpec((None, 1, D_MODEL), lambda i, j: (row_of_tile(i), 0, 1)),
            pl.BlockSpec((1, D_MODEL), lambda i, j: (0, 0)),
            pl.BlockSpec((Z_W // Z_TN, D_MODEL, Z_TN), lambda i, j: (0, 0, 0), pipeline_mode=pl.Buffered(1)),
        ],
        out_specs=[
            pl.BlockSpec((tm, Z_TN), lambda i, j: (i, jnp.minimum(j, n_gate - 1))),
            pl.BlockSpec((tm, Z_TN), lambda i, j: (i, jnp.maximum(j - n_gate, 0))),
        ],
        out_shape=[
            jax.ShapeDtypeStruct((nt, Z_GATE_W), BF16),
            jax.ShapeDtypeStruct((nt, Z_MIX_W), F32),
        ],
        scratch_shapes=[pltpu.VMEM((tm, D_MODEL), BF16)],
        compiler_params=_cparams(("parallel", "arbitrary")),
        name="premix",
    )(x, mod3, mod3, gain, w)


def _gla_kernel(z_ref, wd_ref, bd_ref, s0_ref, e_ref, ind_ref, m_ref, sfin_ref,
                la_scr, of_scr, st_scr, p_scr, cum_scr, k_scr, v_scr):
    t_len = z_ref.shape[0]
    c = GLA_CHUNK
    n_chunks = t_len // c
    hk = GLA_HEADS * GLA_DK
    hv = GLA_HEADS * GLA_DV
    scale = GLA_DK ** -0.5

    pre = _mm_hi(z_ref[:, 768:896], wd_ref[...]) + bd_ref[...]
    la_scr[...] = _log_sigmoid(pre) * (1.0 / GLA_TAU)

    ri = lax.broadcasted_iota(jnp.int32, (c, c), 0)
    ci = lax.broadcasted_iota(jnp.int32, (c, c), 1)
    tri_lo = (ri >= ci).astype(BF16)
    tri_up = (ri <= ci).astype(BF16)
    row = lax.broadcasted_iota(jnp.int32, (c, hk), 0)
    bd_mask = (lax.broadcasted_iota(jnp.int32, (hv, hk), 0) // GLA_DV
               == lax.broadcasted_iota(jnp.int32, (hv, hk), 1) // GLA_DK).astype(F32)

    def chunk(base, la, tri, reverse, slot):
        cum = _mm_exact_lhs(tri, la)
        q = z_ref[pl.ds(base, c), 0:128] * scale
        k = z_ref[pl.ds(base, c), 128:256]
        v = z_ref[pl.ds(base, c), 256:512]
        edge = cum[0:1, :] if reverse else cum[c - 1:c, :]
        st = st_scr[slot]
        o = _nt((q * jnp.exp(cum)).astype(BF16), st.astype(BF16))
        ke = k * jnp.exp(edge - cum)
        cum_scr[slot] = cum
        k_scr[slot] = k
        v_scr[slot] = v
        for j in range(c):
            d = jnp.minimum(cum - cum_scr[slot, j:j + 1, :], 0.0)
            p = q * k_scr[slot, j:j + 1, :] * jnp.exp(d)
            keep = (row <= j) if reverse else (row >= j)
            p_scr[slot, j * c:(j + 1) * c, :] = jnp.where(keep, p, 0.0).astype(BF16)
        pe = _mm(p_scr[slot], e_ref[...])
        for j in range(c):
            o = o + pe[j * c:(j + 1) * c, :] * v_scr[slot, j:j + 1, :]
        st_scr[slot] = st * jnp.exp(edge) + bd_mask * _tn(v.astype(BF16), ke.astype(BF16))
        return o

    st_scr[0] = s0_ref[0]
    st_scr[1] = s0_ref[1]

    def sweep(i, carry):
        bf = pl.multiple_of(i * c, c)
        bb = pl.multiple_of((n_chunks - 1 - i) * c, c)
        of_scr[pl.ds(bf, c), :] = chunk(bf, la_scr[pl.ds(bf, c), 0:128], tri_lo, False, 0)
        m_ref[pl.ds(bb, c), :] = chunk(bb, la_scr[pl.ds(bb, c), 128:256], tri_up, True, 1)
        return carry

    lax.fori_loop(0, n_chunks, sweep, 0)
    sfin_ref[0] = st_scr[0]
    sfin_ref[1] = st_scr[1]

    fb = 8 * c

    def finish(i, carry):
        base = pl.multiple_of(i * fb, fb)
        o = of_scr[pl.ds(base, fb), :] + m_ref[pl.ds(base, fb), :]
        ms = _mm_exact_rhs(o * o, ind_ref[...])
        g = z_ref[pl.ds(base, fb), 512:768]
        m_ref[pl.ds(base, fb), :] = o * lax.rsqrt(ms + EPS) * _silu(g)
        return carry

    lax.fori_loop(0, t_len // fb, finish, 0)


def _gla(z, row_off, n_b, t_len, wd, bd, s0t, e_mat, ind):
    hk, hv = GLA_HEADS * GLA_DK, GLA_HEADS * GLA_DV
    c = GLA_CHUNK
    return pl.pallas_call(
        _gla_kernel,
        grid=(n_b,),
        in_specs=[
            pl.BlockSpec((t_len, 1024), lambda b: (row_off + b, Z_GLA // 1024)),
            pl.BlockSpec((128, 256), lambda b: (0, 0)),
            pl.BlockSpec((1, 256), lambda b: (0, 0)),
            pl.BlockSpec((None, 2, hv, hk), lambda b: (b, 0, 0, 0)),
            pl.BlockSpec((hk, hv), lambda b: (0, 0)),
            pl.BlockSpec((hv, hv), lambda b: (0, 0)),
        ],
        out_specs=[
            pl.BlockSpec((t_len, BR_W), lambda b: (b, 0)),
            pl.BlockSpec((None, 2, hv, hk), lambda b: (b, 0, 0, 0)),
        ],
        out_shape=[
            jax.ShapeDtypeStruct((n_b * t_len, BR_W), F32),
            jax.ShapeDtypeStruct((n_b, 2, hv, hk), F32),
        ],
        scratch_shapes=[
            pltpu.VMEM((t_len, 256), F32),
            pltpu.VMEM((t_len, hv), F32),
            pltpu.VMEM((2, hv, hk), F32),
            pltpu.VMEM((2, c * c, hk), BF16),
            pltpu.VMEM((2, c, hk), F32),
            pltpu.VMEM((2, c, hk), F32),
            pltpu.VMEM((2, c, hv), F32),
        ],
        compiler_params=_cparams(("parallel",)),
        name="gla",
    )(z, wd, bd, s0t, e_mat, ind)


def _ret_kernel(z_ref, cos_ref, sin_ref, swap_ref, lgl_ref, lgs_ref, s0_ref, ind_ref, m_ref, sfin_ref,
                qk_scr, of_scr, st_scr, *, rope):
    t_len = z_ref.shape[0]
    c = RET_CHUNK
    n_chunks = t_len // c
    hk = RET_HEADS * RET_DK
    hv = RET_HEADS * RET_DV
    scale = RET_DK ** -0.5

    lgf = lgl_ref[0:1, :]
    lgb = lgl_ref[1:2, :]
    pos = lax.broadcasted_iota(jnp.int32, (c, hk), 0).astype(F32)
    wq_f = jnp.exp(lgf * (pos + 1.0))
    wk_f = jnp.exp(lgf * (c - 1.0 - pos))
    wq_b = jnp.exp(lgb * (c - pos))
    wk_b = jnp.exp(lgb * pos)
    dec_f = jnp.exp(lgf * float(c))
    dec_b = jnp.exp(lgb * float(c))

    ii = lax.broadcasted_iota(jnp.int32, (c, c), 0)
    jj = lax.broadcasted_iota(jnp.int32, (c, c), 1)
    rel = (ii - jj).astype(F32)
    dms = []
    for h in range(RET_HEADS):
        d_f = jnp.where(ii >= jj, jnp.exp(lgs_ref[0, h] * jnp.maximum(rel, 0.0)), 0.0)
        d_b = jnp.where(jj >= ii, jnp.exp(lgs_ref[1, h] * jnp.maximum(-rel, 0.0)), 0.0)
        dms.append(d_f + d_b)
    dmat = jnp.concatenate(dms, axis=1)

    ek_mask = (lax.broadcasted_iota(jnp.int32, (RET_HEADS * c, hk), 0) // c
               == lax.broadcasted_iota(jnp.int32, (RET_HEADS * c, hk), 1) // RET_DK).astype(F32)
    ev_mask = (lax.broadcasted_iota(jnp.int32, (RET_HEADS * c, hv), 0) // c
               == lax.broadcasted_iota(jnp.int32, (RET_HEADS * c, hv), 1) // RET_DV).astype(F32)
    bd_mask = (lax.broadcasted_iota(jnp.int32, (hv, hk), 0) // RET_DV
               == lax.broadcasted_iota(jnp.int32, (hv, hk), 1) // RET_DK).astype(F32)

    st_scr[...] = s0_ref[0]

    def fwd(i, carry):
        base = pl.multiple_of(i * c, c)
        q = z_ref[pl.ds(base, c), 0:128] * scale
        k = z_ref[pl.ds(base, c), 128:256]
        v = z_ref[pl.ds(base, c), 256:512]
        if rope:
            cs = cos_ref[pl.ds(base, c), :]
            sn = sin_ref[pl.ds(base, c), :]
            q = q * cs + _mm_exact_rhs(q, swap_ref[...]) * sn
            k = k * cs + _mm_exact_rhs(k, swap_ref[...]) * sn
        qk_scr[pl.ds(base, c), 0:128] = q
        qk_scr[pl.ds(base, c), 128:256] = k
        kexp = (jnp.concatenate([k] * RET_HEADS, axis=0) * ek_mask).astype(BF16)
        vexp = (jnp.concatenate([v] * RET_HEADS, axis=0) * ev_mask).astype(BF16)
        sc = _nt(q.astype(BF16), kexp) * dmat
        o = _mm(sc.astype(BF16), vexp)
        st = st_scr[...]
        o = o + _nt((q * wq_f).astype(BF16), st.astype(BF16))
        of_scr[pl.ds(base, c), :] = o
        st_scr[...] = st * dec_f + bd_mask * _tn(v.astype(BF16), (k * wk_f).astype(BF16))
        return carry

    lax.fori_loop(0, n_chunks, fwd, 0)
    sfin_ref[0] = st_scr[...]
    st_scr[...] = s0_ref[1]

    def bwd(i, carry):
        base = pl.multiple_of((n_chunks - 1 - i) * c, c)
        q = qk_scr[pl.ds(base, c), 0:128]
        k = qk_scr[pl.ds(base, c), 128:256]
        v = z_ref[pl.ds(base, c), 256:512]
        st = st_scr[...]
        o = of_scr[pl.ds(base, c), :] + _nt((q * wq_b).astype(BF16), st.astype(BF16))
        st_scr[...] = st * dec_b + bd_mask * _tn(v.astype(BF16), (k * wk_b).astype(BF16))
        ms = _mm_exact_rhs(o * o, ind_ref[...])
        g = z_ref[pl.ds(base, c), 512:768]
        m_ref[pl.ds(base, c), :] = o * lax.rsqrt(ms + EPS) * _silu(g)
        return carry

    lax.fori_loop(0, n_chunks, bwd, 0)
    sfin_ref[1] = st_scr[...]


def _ret(z, row_off, n_b, t_len, cos_t, sin_t, swap, lgl, lgs, s0t, ind, rope):
    hk, hv = RET_HEADS * RET_DK, RET_HEADS * RET_DV
    return pl.pallas_call(
        functools.partial(_ret_kernel, rope=rope),
        grid=(n_b,),
        in_specs=[
            pl.BlockSpec((t_len, 1024), lambda b: (row_off + b, Z_RET // 1024)),
            pl.BlockSpec((t_len, hk), lambda b: (0, 0)),
            pl.BlockSpec((t_len, hk), lambda b: (0, 0)),
            pl.BlockSpec((hk, hk), lambda b: (0, 0)),
            pl.BlockSpec((2, hk), lambda b: (0, 0)),
            pl.BlockSpec(memory_space=pltpu.SMEM),
            pl.BlockSpec((None, 2, hv, hk), lambda b: (b, 0, 0, 0)),
            pl.BlockSpec((hv, hv), lambda b: (0, 0)),
        ],
        out_specs=[
            pl.BlockSpec((t_len, BR_W), lambda b: (b, 0)),
            pl.BlockSpec((None, 2, hv, hk), lambda b: (b, 0, 0, 0)),
        ],
        out_shape=[
            jax.ShapeDtypeStruct((n_b * t_len, BR_W), F32),
            jax.ShapeDtypeStruct((n_b, 2, hv, hk), F32),
        ],
        scratch_shapes=[
            pltpu.VMEM((t_len, 2 * hk), F32),
            pltpu.VMEM((t_len, hv), F32),
            pltpu.VMEM((hv, hk), F32),
        ],
        compiler_params=_cparams(("parallel",)),
        name="ret",
    )(z, cos_t, sin_t, swap, lgl, lgs, s0t, ind)


def _s5_disc_kernel(are_ref, aim_ref, ldt_ref, coef_ref, pf_ref, pb_ref):
    tb = pf_ref.shape[0]
    re = jnp.minimum(are_ref[...], S5_RE_MAX)
    im = aim_ref[...]
    dt = jnp.exp(ldt_ref[...])
    er = jnp.exp(re * dt)
    lbr = er * jnp.cos(im * dt)
    lbi = er * jnp.sin(im * dt)
    den = re * re + im * im
    nr = lbr - 1.0
    coef_ref[:, 0:S5_CH] = (nr * re + lbi * im) / den
    coef_ref[:, S5_CH:] = (lbi * re - nr * im) / den
    t = lax.broadcasted_iota(jnp.int32, (tb, S5_CH), 0).astype(F32)
    nf = t + 1.0
    nb = float(tb) - t
    mf = jnp.exp(nf * (re[0:1] * dt[0:1]))
    pf_ref[:, 0:S5_CH] = mf * jnp.cos(nf * (im[0:1] * dt[0:1]))
    pf_ref[:, S5_CH:] = mf * jnp.sin(nf * (im[0:1] * dt[0:1]))
    mb = jnp.exp(nb * (re[1:2] * dt[1:2]))
    pb_ref[:, 0:S5_CH] = mb * jnp.cos(nb * (im[1:2] * dt[1:2]))
    pb_ref[:, S5_CH:] = mb * jnp.sin(nb * (im[1:2] * dt[1:2]))


def _s5_disc(are, aim, ldt, tb):
    return pl.pallas_call(
        _s5_disc_kernel,
        out_shape=[
            jax.ShapeDtypeStruct((2, 2 * S5_CH), F32),
            jax.ShapeDtypeStruct((tb, 2 * S5_CH), F32),
            jax.ShapeDtypeStruct((tb, 2 * S5_CH), F32),
        ],
        compiler_params=pltpu.CompilerParams(vmem_limit_bytes=VMEM_LIMIT),
        name="s5_disc",
    )(are, aim, ldt)


def _s5_kernel(u_ref, bbd_ref, cre_ref, cim_ref, coef_ref, pf_ref, pb_ref, dsk_ref, wglu_ref, h0_ref,
               m_ref, hfin_ref, y_scr, bu_scr, hr_scr, hi_scr, car_scr):
    t_len = u_ref.shape[0]
    tb = SCAN_TB
    sub = pf_ref.shape[0]
    n_blocks = t_len // tb
    row_in = lax.broadcasted_iota(jnp.int32, (tb, LANES), 0) % sub
    steps = [1 << s for s in range(int(math.log2(sub)))]

    def block(base, dr_i, reverse):
        p_ref = pb_ref if reverse else pf_ref
        u = u_ref[pl.ds(base, tb), :]
        bu_scr[...] = _mm(u.astype(BF16), bbd_ref[...])
        for g in range(S5_CH // LANES):
            lo, hi = g * LANES, (g + 1) * LANES
            br = bu_scr[:, lo:hi]
            bi = bu_scr[:, S5_CH + lo:S5_CH + hi]
            cr = coef_ref[dr_i:dr_i + 1, lo:hi]
            ci = coef_ref[dr_i:dr_i + 1, S5_CH + lo:S5_CH + hi]
            hr = cr * br - ci * bi
            hi_ = cr * bi + ci * br
            for d in steps:
                if reverse:
                    pr = p_ref[sub - d:sub - d + 1, lo:hi]
                    pi = p_ref[sub - d:sub - d + 1, S5_CH + lo:S5_CH + hi]
                    keep = row_in < sub - d
                    sr = jnp.where(keep, pltpu.roll(hr, tb - d, 0), 0.0)
                    si = jnp.where(keep, pltpu.roll(hi_, tb - d, 0), 0.0)
                else:
                    pr = p_ref[d - 1:d, lo:hi]
                    pi = p_ref[d - 1:d, S5_CH + lo:S5_CH + hi]
                    keep = row_in >= d
                    sr = jnp.where(keep, pltpu.roll(hr, d, 0), 0.0)
                    si = jnp.where(keep, pltpu.roll(hi_, d, 0), 0.0)
                hr, hi_ = hr + pr * sr - pi * si, hi_ + pr * si + pi * sr
            car = car_scr[0:1, lo:hi]
            cai = car_scr[0:1, S5_CH + lo:S5_CH + hi]
            pwr = p_ref[:, lo:hi]
            pwi = p_ref[:, S5_CH + lo:S5_CH + hi]
            n_grp = tb // sub
            for v in (range(n_grp - 1, -1, -1) if reverse else range(n_grp)):
                gr = hr[v * sub:(v + 1) * sub, :] + pwr * car - pwi * cai
                gi = hi_[v * sub:(v + 1) * sub, :] + pwr * cai + pwi * car
                hr_scr[v * sub:(v + 1) * sub, lo:hi] = gr
                hi_scr[v * sub:(v + 1) * sub, lo:hi] = gi
                edge = 0 if reverse else sub - 1
                car, cai = gr[edge:edge + 1, :], gi[edge:edge + 1, :]
            car_scr[0:1, lo:hi] = car
            car_scr[0:1, S5_CH + lo:S5_CH + hi] = cai
        y = _mm(hr_scr[...].astype(BF16), cre_ref[...]) - _mm(hi_scr[...].astype(BF16), cim_ref[...])
        return u, y

    car_scr[0:1, :] = h0_ref[0:1, :]

    def fwd(i, carry):
        base = pl.multiple_of(i * tb, tb)
        _, y = block(base, 0, False)
        y_scr[pl.ds(base, tb), :] = y
        return carry

    lax.fori_loop(0, n_blocks, fwd, 0)
    hfin_ref[0:1, :] = car_scr[0:1, :]
    car_scr[0:1, :] = h0_ref[1:2, :]

    def bwd(i, carry):
        base = pl.multiple_of((n_blocks - 1 - i) * tb, tb)
        u, y = block(base, 1, True)
        y = _gelu(y_scr[pl.ds(base, tb), :] + y + dsk_ref[...] * u)
        gg = _mm(y.astype(BF16), wglu_ref[...])
        m_ref[pl.ds(base, tb), :] = gg[:, 0:BR_W] * _sigmoid(gg[:, BR_W:])
        return carry

    lax.fori_loop(0, n_blocks, bwd, 0)
    hfin_ref[1:2, :] = car_scr[0:1, :]


def _s5(z, row_off, n_b, t_len, bbd, cre, cim, coef, pf, pb, dsk, wglu, h0):
    tb = SCAN_TB
    sub = pf.shape[0]
    full = lambda shape: pl.BlockSpec(shape, lambda b: (0,) * len(shape))
    return pl.pallas_call(
        _s5_kernel,
        grid=(n_b,),
        in_specs=[
            pl.BlockSpec((t_len, BR_W), lambda b: (row_off + b, Z_S5 // BR_W)),
            full((BR_W, 2 * S5_CH)),
            full((S5_CH, BR_W)),
            full((S5_CH, BR_W)),
            full((2, 2 * S5_CH)),
            full((sub, 2 * S5_CH)),
            full((sub, 2 * S5_CH)),
            full((1, BR_W)),
            full((BR_W, 2 * BR_W)),
            pl.BlockSpec((None, 2, 2 * S5_CH), lambda b: (b, 0, 0)),
        ],
        out_specs=[
            pl.BlockSpec((t_len, BR_W), lambda b: (b, 0)),
            pl.BlockSpec((None, 2, 2 * S5_CH), lambda b: (b, 0, 0)),
        ],
        out_shape=[
            jax.ShapeDtypeStruct((n_b * t_len, BR_W), F32),
            jax.ShapeDtypeStruct((n_b, 2, 2 * S5_CH), F32),
        ],
        scratch_shapes=[
            pltpu.VMEM((t_len, BR_W), F32),
            pltpu.VMEM((tb, 2 * S5_CH), F32),
            pltpu.VMEM((tb, S5_CH), F32),
            pltpu.VMEM((tb, S5_CH), F32),
            pltpu.VMEM((8, 2 * S5_CH), F32),
        ],
        compiler_params=_cparams(("parallel",)),
        name="s5",
    )(z, bbd, cre, cim, coef, pf, pb, dsk, wglu, h0)


def _lru_kernel(z_ref, cw_ref, cb_ref, wg_ref, bg_ref, lam_ref, h0_ref, m_ref, hfin_ref,
                xc_scr, hf_scr, car_scr):
    t_len = z_ref.shape[0]
    tb = SCAN_TB
    n_blocks = t_len // tb
    steps = [1 << s for s in range(int(math.log2(tb)))]

    x = z_ref[:, 0:BR_W]
    trow = lax.broadcasted_iota(jnp.int32, (t_len, BR_W), 0)
    xm1 = jnp.where(trow >= 1, pltpu.roll(x, 1, 0), 0.0)
    xp1 = jnp.where(trow < t_len - 1, pltpu.roll(x, t_len - 1, 0), 0.0)
    xp2 = jnp.where(trow < t_len - 2, pltpu.roll(x, t_len - 2, 0), 0.0)
    xc_scr[...] = (cw_ref[0:1, :] * xm1 + cw_ref[1:2, :] * x + cw_ref[2:3, :] * xp1
                   + cw_ref[3:4, :] * xp2 + cb_ref[...])

    row = lax.broadcasted_iota(jnp.int32, (tb, BR_W), 0)
    sp = _softplus(-lam_ref[...])

    def block(base, dr_i, reverse):
        xc = xc_scr[pl.ds(base, tb), :]
        off = dr_i * 2 * BR_W
        gates = _mm(xc.astype(BF16), wg_ref[:, off:off + 2 * BR_W]) + bg_ref[:, off:off + 2 * BR_W]
        r = _sigmoid(gates[:, 0:BR_W])
        ig = _sigmoid(gates[:, BR_W:])
        log_a = -LRU_C * r * sp[dr_i:dr_i + 1, :]
        a = jnp.exp(log_a)
        th = jnp.tanh(log_a)
        b = jnp.sqrt(-2.0 * th / (1.0 - th)) * (ig * xc)
        for d in steps:
            if reverse:
                keep = row < tb - d
                a_s = jnp.where(keep, pltpu.roll(a, tb - d, 0), 1.0)
                b_s = jnp.where(keep, pltpu.roll(b, tb - d, 0), 0.0)
            else:
                keep = row >= d
                a_s = jnp.where(keep, pltpu.roll(a, d, 0), 1.0)
                b_s = jnp.where(keep, pltpu.roll(b, d, 0), 0.0)
            b = b + a * b_s
            a = a * a_s
        h = b + a * car_scr[dr_i:dr_i + 1, :]
        edge = 0 if reverse else tb - 1
        car_scr[dr_i:dr_i + 1, :] = h[edge:edge + 1, :]
        return h

    car_scr[0:2, :] = h0_ref[...]

    def fwd(i, carry):
        base = pl.multiple_of(i * tb, tb)
        hf_scr[pl.ds(base, tb), :] = block(base, 0, False)
        return carry

    lax.fori_loop(0, n_blocks, fwd, 0)

    def bwd(i, carry):
        base = pl.multiple_of((n_blocks - 1 - i) * tb, tb)
        h = block(base, 1, True) + hf_scr[pl.ds(base, tb), :]
        m_ref[pl.ds(base, tb), :] = h * _gelu(z_ref[pl.ds(base, tb), BR_W:2 * BR_W])
        return carry

    lax.fori_loop(0, n_blocks, bwd, 0)
    hfin_ref[...] = car_scr[0:2, :]


def _lru(z, row_off, n_b, t_len, cw, cb, wg, bg, lam, h0):
    full = lambda shape: pl.BlockSpec(shape, lambda b: (0,) * len(shape))
    return pl.pallas_call(
        _lru_kernel,
        grid=(n_b,),
        in_specs=[
            pl.BlockSpec((t_len, 2 * BR_W), lambda b: (row_off + b, Z_LRU // (2 * BR_W))),
            full((4, BR_W)),
            full((1, BR_W)),
            full((BR_W, 4 * BR_W)),
            full((1, 4 * BR_W)),
            full((2, BR_W)),
            pl.BlockSpec((None, 2, BR_W), lambda b: (b, 0, 0)),
        ],
        out_specs=[
            pl.BlockSpec((t_len, BR_W), lambda b: (b, 0)),
            pl.BlockSpec((None, 2, BR_W), lambda b: (b, 0, 0)),
        ],
        out_shape=[
            jax.ShapeDtypeStruct((n_b * t_len, BR_W), F32),
            jax.ShapeDtypeStruct((n_b, 2, BR_W), F32),
        ],
        scratch_shapes=[
            pltpu.VMEM((t_len, BR_W), F32),
            pltpu.VMEM((t_len, BR_W), F32),
            pltpu.VMEM((8, BR_W), F32),
        ],
        compiler_params=_cparams(("parallel",)),
        name="lru",
    )(z, cw, cb, wg, bg, lam, h0)


def _merge_kernel(ma_ref, mb_ref, mc_ref, md_ref, zg_ref, x_ref, g1_ref, sh2_ref, sc2_ref, gain_ref,
                  wb_ref, wo_ref, xo_ref, h2t_ref):
    acc = None
    for n, m_ref in enumerate((ma_ref, mb_ref, mc_ref, md_ref)):
        proj = _mm(m_ref[...].astype(BF16), wb_ref[n])
        term = _sigmoid(zg_ref[:, n * D_MODEL:(n + 1) * D_MODEL].astype(F32)) * proj
        acc = term if acc is None else acc + term
    xn = x_ref[...] + g1_ref[...] * _mm(acc.astype(BF16), wo_ref[...])
    xo_ref[...] = xn
    h2t_ref[...] = _rms_mod(xn, gain_ref[...], sc2_ref[...], sh2_ref[...]).T.astype(BF16)


def _merge(ms, z, x, mod3, gain, wb, wo, row_of_tile, tm):
    nt = x.shape[0]
    modspec = lambda k: pl.BlockSpec((None, 1, D_MODEL), lambda i: (row_of_tile(i), 0, k))
    return pl.pallas_call(
        _merge_kernel,
        grid=(nt // tm,),
        in_specs=[pl.BlockSpec((tm, BR_W), lambda i: (i, 0))] * 4 + [
            pl.BlockSpec((tm, 4 * D_MODEL), lambda i: (i, 0)),
            pl.BlockSpec((tm, D_MODEL), lambda i: (i, 0)),
            modspec(2), modspec(3), modspec(4),
            pl.BlockSpec((1, D_MODEL), lambda i: (0, 0)),
            pl.BlockSpec((4, BR_W, D_MODEL), lambda i: (0, 0, 0)),
            pl.BlockSpec((D_MODEL, D_MODEL), lambda i: (0, 0)),
        ],
        out_specs=[
            pl.BlockSpec((tm, D_MODEL), lambda i: (i, 0)),
            pl.BlockSpec((D_MODEL, tm), lambda i: (0, i)),
        ],
        out_shape=[
            jax.ShapeDtypeStruct((nt, D_MODEL), F32),
            jax.ShapeDtypeStruct((D_MODEL, nt), BF16),
        ],
        compiler_params=_cparams(("parallel",)),
        name="merge",
    )(*ms, z, x, mod3, mod3, mod3, gain, wb, wo)


def _peer_kernel(h2t_ref, x_ref, g2_ref, wqt_ref, kbt_ref, *rest, te, ts):
    n_sub = te // ts
    u_refs, vt_refs = rest[:n_sub], rest[n_sub:2 * n_sub]
    xo_ref, n_scr, r1_scr, a_scr, b_scr, sc_scr, v_scr, at_scr, wa_scr, yt_scr = rest[2 * n_sub:]
    tm = h2t_ref.shape[1]
    n_lt = tm // LANES
    ic = te // PEER_KEYS
    c_idx = pl.program_id(1)
    nk = PEER_KEYS

    @pl.when(c_idx == 0)
    def _route():
        qt = _mm(wqt_ref[...], h2t_ref[...])
        sct = _mm(kbt_ref[...], qt.astype(BF16))
        for lt in range(n_lt):
            sc_scr[lt] = sct[:, lt * LANES:(lt + 1) * LANES]
        r8 = lax.broadcasted_iota(jnp.int32, (8, LANES), 0)

        def per_tile(lt, carry):
            def per_head(h, vo):
                rank1 = jnp.full((nk, LANES), float(PEER_TOPK), F32)
                for s in range(2):
                    off = pl.multiple_of(h * (2 * nk) + s * nk, nk)
                    xs = sc_scr[lt, pl.ds(off, nk), :]
                    for r in range(PEER_TOPK):
                        m = jnp.max(xs, axis=0, keepdims=True)
                        v_scr[vo + s * PEER_TOPK + r:vo + s * PEER_TOPK + r + 1, :] = m
                        hit = xs == m
                        if s == 1:
                            rank1 = jnp.where(hit, float(r), rank1)
                        xs = jnp.where(hit, NEG_INF, xs)
                v0 = v_scr[vo:vo + PEER_TOPK, :]
                v1 = v_scr[vo + PEER_TOPK:vo + 2 * PEER_TOPK, :]
                pieces = [v0[0:1, :] + v1]
                for r0 in range(1, 8):
                    pieces.append(jnp.where(r8 < PEER_TOPK // (r0 + 1), v0[r0:r0 + 1, :] + v1[0:8, :], NEG_INF))
                pieces.append(v0[8:16, :] + v1[0:1, :])
                cand = jnp.concatenate(pieces, axis=0)
                top = v0[0:1, :] + v1[0:1, :]
                zsum = jnp.zeros_like(top)
                tau = top
                for r in range(PEER_TOPK):
                    tau = jnp.max(cand, axis=0, keepdims=True)
                    zsum = zsum + jnp.exp(tau - top)
                    cand = jnp.where(cand == tau, NEG_INF, cand)
                o0 = pl.multiple_of(h * (2 * nk), nk)
                o1 = pl.multiple_of(h * (2 * nk) + nk, nk)
                ho = pl.multiple_of(h * nk, nk)
                s0 = sc_scr[lt, pl.ds(o0, nk), :]
                s1 = sc_scr[lt, pl.ds(o1, nk), :]
                cnt = jnp.zeros((nk, LANES), F32)
                for r0 in range(PEER_TOPK):
                    ok = (v0[r0:r0 + 1, :] + v1) >= tau
                    c_r0 = jnp.sum(jnp.where(ok, 1.0, 0.0), axis=0, keepdims=True)
                    cnt = jnp.where(s0 == v0[r0:r0 + 1, :], c_r0, cnt)
                n_scr[lt, pl.ds(ho, nk), :] = cnt
                r1_scr[lt, pl.ds(ho, nk), :] = rank1.astype(BF16)
                a_scr[lt, pl.ds(ho, nk), :] = jnp.exp(s0 - v0[0:1, :])
                b_scr[lt, pl.ds(ho, nk), :] = (jnp.exp(s1 - v1[0:1, :]) / zsum).astype(BF16)

            def per_pair(hp, carry2):
                per_head(2 * hp, 0)
                per_head(2 * hp + 1, 2 * PEER_TOPK)
                return carry2

            return lax.fori_loop(0, PEER_HEADS // 2, per_pair, carry)

        lax.fori_loop(0, n_lt, per_tile, 0)
        yt_scr[...] = jnp.zeros_like(yt_scr)

    groups = ts // nk

    def a_stage(k):
        at_scr[k % 2] = _mm(u_refs[k][...], h2t_ref[...]).astype(BF16)

    def y_stage(k):
        yt_scr[...] += _mm(vt_refs[k][...], wa_scr[k % 2])

    def w_block(k, lt, ii, dep):
        pk = BF16_ROWS
        nv = nk // pk
        cols = slice(lt * LANES, (lt + 1) * LANES)
        acc = [None] * nv
        for h in range(PEER_HEADS):
            row = h * nk + c_idx * ic + k * groups + ii
            n_t = _row_tile_bf16(n_scr[lt, pl.ds(row, 1), :] + dep, pk)
            a_t = _row_tile_bf16(a_scr[lt, pl.ds(row, 1), :] + dep, pk)
            for jv in range(nv):
                r1 = r1_scr[lt, h * nk + jv * pk:h * nk + (jv + 1) * pk, :]
                b1 = b_scr[lt, h * nk + jv * pk:h * nk + (jv + 1) * pk, :]
                term = jnp.where(r1 < n_t, b1, 0.0) * a_t
                acc[jv] = term if acc[jv] is None else acc[jv] + term
        out = None
        for jv in range(nv):
            rows = slice(ii * nk + jv * pk, ii * nk + (jv + 1) * pk)
            out = acc[jv] * _gelu_bf16(at_scr[k % 2, rows, cols])
            wa_scr[k % 2, rows, cols] = out
        last = out[0:1, :].astype(F32)
        return jnp.where((last < 2.0) & (last > -2.0), last, 1.0) * 0.0

    dep = jnp.zeros((1, LANES), F32)
    a_stage(0)
    for k in range(n_sub):
        if k + 1 < n_sub:
            a_stage(k + 1)
        if k >= 1:
            y_stage(k - 1)
        for lt in range(n_lt):
            for ii in range(groups):
                dep = w_block(k, lt, ii, dep)
    y_stage(n_sub - 1)

    @pl.when(c_idx == pl.num_programs(1) - 1)
    def _fin():
        xo_ref[...] = x_ref[...] + g2_ref[...] * yt_scr[...].T


def _peer(h2t, x, mod3, wqt, kbt, u, vt, row_of_tile, tm, te, ts):
    nt = x.shape[0]
    n_lt = tm // LANES
    n_rt = PEER_HEADS * PEER_KEYS
    n_sub = te // ts
    u_specs = [pl.BlockSpec((ts, D_MODEL), lambda i, c, k=k: (c * n_sub + k, 0)) for k in range(n_sub)]
    vt_specs = [pl.BlockSpec((None, D_MODEL, ts), lambda i, c, k=k: (c * n_sub + k, 0, 0)) for k in range(n_sub)]
    return pl.pallas_call(
        functools.partial(_peer_kernel, te=te, ts=ts),
        grid=(nt // tm, PEER_EXPERTS // te),
        in_specs=[
            pl.BlockSpec((D_MODEL, tm), lambda i, c: (0, i), pipeline_mode=pl.Buffered(1)),
            pl.BlockSpec((tm, D_MODEL), lambda i, c: (i, 0), pipeline_mode=pl.Buffered(1)),
            pl.BlockSpec((None, 1, D_MODEL), lambda i, c: (row_of_tile(i), 0, 5)),
            pl.BlockSpec((D_MODEL, D_MODEL), lambda i, c: (0, 0), pipeline_mode=pl.Buffered(1)),
            pl.BlockSpec((2 * n_rt, D_MODEL), lambda i, c: (0, 0), pipeline_mode=pl.Buffered(1)),
        ] + u_specs + vt_specs,
        out_specs=pl.BlockSpec((tm, D_MODEL), lambda i, c: (i, 0)),
        out_shape=jax.ShapeDtypeStruct((nt, D_MODEL), F32),
        scratch_shapes=[
            pltpu.VMEM((n_lt, n_rt, LANES), F32),
            pltpu.VMEM((n_lt, n_rt, LANES), BF16),
            pltpu.VMEM((n_lt, n_rt, LANES), F32),
            pltpu.VMEM((n_lt, n_rt, LANES), BF16),
            pltpu.VMEM((n_lt, 2 * n_rt, LANES), F32),
            pltpu.VMEM((4 * PEER_TOPK, LANES), F32),
            pltpu.VMEM((2, ts, tm), BF16),
            pltpu.VMEM((2, ts, tm), BF16),
            pltpu.VMEM((D_MODEL, tm), F32),
        ],
        compiler_params=_cparams(("parallel", "arbitrary")),
        name="peer",
    )(h2t, x, mod3, wqt, kbt, *([u] * n_sub), *([vt] * n_sub))


def _final_kernel(x_ref, g_ref, o_ref):
    x = x_ref[...]
    o_ref[...] = x * lax.rsqrt(jnp.mean(x * x, axis=-1, keepdims=True) + EPS) * g_ref[...]


def _final_norm(x, gain, tm):
    nt = x.shape[0]
    return pl.pallas_call(
        _final_kernel,
        grid=(nt // tm,),
        in_specs=[pl.BlockSpec((tm, D_MODEL), lambda i: (i, 0)), pl.BlockSpec((1, D_MODEL), lambda i: (0, 0))],
        out_specs=pl.BlockSpec((tm, D_MODEL), lambda i: (i, 0)),
        out_shape=jax.ShapeDtypeStruct((nt, D_MODEL), F32),
        compiler_params=_cparams(("parallel",)),
        name="final_norm",
    )(x, gain)


def _block_diag(blocks):
    n, r, c = blocks.shape
    eye = jnp.eye(n, dtype=blocks.dtype)
    return jnp.einsum('nrc,nm->nrmc', blocks, eye).reshape(n * r, n * c)


def _state_to_bd_t(s):
    b, two, h, k, v = s.shape
    eye = jnp.eye(h, dtype=s.dtype)
    return jnp.einsum('bdhkv,hg->bdhvgk', s, eye).reshape(b, two, h * v, h * k)


def _bd_t_to_state(st, h, k, v):
    b = st.shape[0]
    return jnp.einsum('bdhvhk->bdhkv', st.reshape(b, 2, h, v, h, k))


def _rope_tables(t_len):
    rows = t_len // GRID_W
    row = jnp.repeat(jnp.arange(rows), GRID_W).astype(F32)
    col = jnp.tile(jnp.arange(GRID_W), rows).astype(F32)
    n_freq = RET_DK // 4
    inv_freq = ROPE_BASE ** (-jnp.arange(n_freq, dtype=F32) / n_freq)
    ang = jnp.concatenate([row[:, None] * inv_freq, col[:, None] * inv_freq], axis=-1)
    cos, sin = jnp.cos(ang), jnp.sin(ang)
    cos_h = jnp.concatenate([cos, cos], axis=-1)
    sin_h = jnp.concatenate([-sin, sin], axis=-1)
    return jnp.tile(cos_h, (1, RET_HEADS)), jnp.tile(sin_h, (1, RET_HEADS))


def _swap_matrix():
    lane = jnp.arange(RET_HEADS * RET_DK)
    half = RET_DK // 2
    src = jnp.where(lane % RET_DK < half, lane + half, lane - half)
    return (lane[:, None] == src[None, :]).astype(BF16)


def kernel(x_prompt, x_sample, c, state_gla, state_ret, state_s5, state_lru, c_ctx, w_mod, b_mod, norm_mix, norm_ffn, norm_final, w_in, gla_w_decay, gla_b_decay, ret_decay_logit, s5_a_re, s5_a_im, s5_log_dt, s5_b_re, s5_b_im, s5_c_re, s5_c_im, s5_d, s5_w_glu, lru_conv_w, lru_conv_b, lru_w_a, lru_b_a, lru_w_x, lru_b_x, lru_lambda, w_branch, w_out, peer_w_q, peer_keys, peer_u, peer_v):
    n_bp, t_p, _ = x_prompt.shape
    n_bs, t_s, _ = x_sample.shape
    depth = w_in.shape[0]
    ntp, nts = n_bp * t_p, n_bs * t_s
    tm = TOK_TM
    assert ntp % t_s == 0 and all(ntp % t == 0 and t_s % t == 0 for t in (TOK_TM, PRE_TM, PEER_TM))

    x = jnp.concatenate([x_prompt.reshape(ntp, D_MODEL), x_sample.reshape(nts, D_MODEL)], axis=0)

    n_rows = 8 * ((1 + n_bs + 7) // 8)
    cond = jnp.zeros((n_rows, D_MODEL), F32).at[0].set(c_ctx).at[1:1 + n_bs].set(c)
    mods = _adaln(cond, w_mod, b_mod)

    def make_row_of_tile(tile):
        def row_of_tile(i):
            return jnp.where(i < ntp // tile, 0, 1 + (i - ntp // tile) // (t_s // tile))
        return row_of_tile

    zpad = lambda n: jnp.zeros((depth, D_MODEL, n), F32)
    w_in_p = jnp.concatenate([w_in[:, :, 2336:6432], w_in[:, :, 0:800], zpad(224), w_in[:, :, 800:1568], zpad(256),
                              w_in[:, :, 1824:2336], w_in[:, :, 1568:1824], zpad(256)], axis=2).astype(BF16)
    assert w_in_p.shape[2] == Z_W
    w_in_p = w_in_p.reshape(depth, D_MODEL, Z_W // Z_TN, Z_TN).transpose(0, 2, 1, 3)

    hk, hv = GLA_HEADS * GLA_DK, GLA_HEADS * GLA_DV
    e_mat = (jnp.arange(hk)[:, None] // GLA_DK == jnp.arange(hv)[None, :] // GLA_DV).astype(BF16)
    ind = ((jnp.arange(hv)[:, None] // GLA_DV == jnp.arange(hv)[None, :] // GLA_DV).astype(F32) / GLA_DV).astype(BF16)
    swap = _swap_matrix()
    cos_s, sin_s = _rope_tables(t_s)
    cos_p, sin_p = jnp.ones((t_p, hk), F32), jnp.zeros((t_p, hk), F32)

    zeros_bd = jnp.zeros((n_bp, 2, hv, hk), F32)
    zeros_s5 = jnp.zeros((n_bp, 2, 2 * S5_CH), F32)
    zeros_lru = jnp.zeros((n_bp, 2, BR_W), F32)

    gla_l, ret_l, s5_l, lru_l = [], [], [], []
    for l in range(depth):
        mod3 = mods[l].reshape(n_rows, 1, N_MOD * D_MODEL)
        zg, z = _premix(x, mod3, norm_mix[l].reshape(1, D_MODEL), w_in_p[l], make_row_of_tile(PRE_TM), PRE_TM)

        wd = jnp.zeros((128, 256), F32)
        wd = wd.at[0:GLA_RANK, 0:hk].set(gla_w_decay[l, 0]).at[GLA_RANK:2 * GLA_RANK, hk:].set(gla_w_decay[l, 1])
        bd = gla_b_decay[l].reshape(1, 2 * hk)
        lg = jax.nn.log_sigmoid(ret_decay_logit[l].astype(F32))
        lgl = jnp.repeat(lg, RET_DK, axis=1)
        bre = _block_diag(jnp.swapaxes(s5_b_re[l], 1, 2))
        bim = _block_diag(jnp.swapaxes(s5_b_im[l], 1, 2))
        bbd = jnp.concatenate([bre, bim], axis=1).astype(BF16)
        cre = _block_diag(jnp.swapaxes(s5_c_re[l], 1, 2)).astype(BF16)
        cim = _block_diag(jnp.swapaxes(s5_c_im[l], 1, 2)).astype(BF16)
        coef, pf, pb = _s5_disc(s5_a_re[l].reshape(2, S5_CH), s5_a_im[l].reshape(2, S5_CH),
                                jnp.repeat(s5_log_dt[l], S5_STATE, axis=1), F32_ROWS)
        dsk = s5_d[l].reshape(1, BR_W)
        wglu = s5_w_glu[l].astype(BF16)
        wg = jnp.concatenate([_block_diag(lru_w_a[l, 0]), _block_diag(lru_w_x[l, 0]),
                              _block_diag(lru_w_a[l, 1]), _block_diag(lru_w_x[l, 1])], axis=1).astype(BF16)
        bg = jnp.concatenate([lru_b_a[l, 0], lru_b_x[l, 0], lru_b_a[l, 1], lru_b_x[l, 1]]).reshape(1, 4 * BR_W)
        cw = lru_conv_w[l]
        cb = lru_conv_b[l].reshape(1, BR_W)
        lam = lru_lambda[l]

        s5_h0 = state_s5[:, l].reshape(n_bs, 2, 2 * S5_CH)

        outs = []
        for (row_off, n_b, t_len, sg, sr, ss, sl, cs, sn, rope) in (
                (0, n_bp, t_p, zeros_bd, zeros_bd, zeros_s5, zeros_lru, cos_p, sin_p, False),
                (ntp // t_s, n_bs, t_s, _state_to_bd_t(state_gla[:, l]), _state_to_bd_t(state_ret[:, l]),
                 s5_h0, state_lru[:, l], cos_s, sin_s, True)):
            m_a, f_gla = _gla(z, row_off, n_b, t_len, wd, bd, sg, e_mat, ind)
            m_b, f_ret = _ret(z, row_off, n_b, t_len, cs, sn, swap, lgl, lg, sr, ind, rope)
            m_c, f_s5 = _s5(z, row_off, n_b, t_len, bbd, cre, cim, coef, pf, pb, dsk, wglu, ss)
            m_d, f_lru = _lru(z, row_off, n_b, t_len, cw, cb, wg, bg, lam, sl)
            outs.append(((m_a, m_b, m_c, m_d), (f_gla, f_ret, f_s5, f_lru)))

        ms = [jnp.concatenate([outs[0][0][n], outs[1][0][n]], axis=0) for n in range(4)]
        f_gla, f_ret, f_s5, f_lru = outs[0][1]
        gla_l.append(_bd_t_to_state(f_gla, GLA_HEADS, GLA_DK, GLA_DV))
        ret_l.append(_bd_t_to_state(f_ret, RET_HEADS, RET_DK, RET_DV))
        s5_l.append(f_s5.reshape(n_bp, 2, 2, S5_GROUPS, S5_STATE))
        lru_l.append(f_lru)

        x, h2t = _merge(ms, zg, x, mod3, norm_ffn[l].reshape(1, D_MODEL), w_branch[l].astype(BF16),
                        w_out[l].astype(BF16), make_row_of_tile(tm), tm)

        wqt = peer_w_q[l].T.astype(BF16)
        vt_l = peer_v[l].reshape(PEER_EXPERTS // PEER_TS, PEER_TS, D_MODEL).transpose(0, 2, 1).astype(BF16)
        kbt = _block_diag(peer_keys[l].reshape(2 * PEER_HEADS, PEER_KEYS, PEER_QDIM // 2)).astype(BF16)
        x = _peer(h2t, x, mod3, wqt, kbt, peer_u[l].astype(BF16), vt_l,
                  make_row_of_tile(PEER_TM), PEER_TM, PEER_TE, PEER_TS)

    y = _final_norm(x, norm_final.reshape(1, D_MODEL), tm)
    y_p = y[:ntp].reshape(n_bp, t_p, D_MODEL)
    y_s = y[ntp:].reshape(n_bs, t_s, D_MODEL)
    return (y_p, y_s, jnp.stack(gla_l, axis=1), jnp.stack(ret_l, axis=1),
            jnp.stack(s5_l, axis=1), jnp.stack(lru_l, axis=1))
```

```python
import functools
import math

import jax
import jax.numpy as jnp
from jax import lax
from jax.experimental import pallas as pl
from jax.experimental.pallas import tpu as pltpu

F32 = jnp.float32
BF16 = jnp.bfloat16
HI = lax.Precision.HIGHEST

D_MODEL = 1024
N_MOD = 6
EPS = 1e-6
BR_W = 256
GLA_HEADS, GLA_DK, GLA_DV, GLA_RANK, GLA_TAU, GLA_CHUNK = 4, 32, 64, 16, 16.0, 32
RET_HEADS, RET_DK, RET_DV, RET_CHUNK = 4, 32, 64, 64
ROPE_BASE = 10000.0
GRID_W = 64
S5_GROUP, S5_GROUPS, S5_STATE, S5_RE_MAX = 16, 16, 64, -1e-4
S5_CH = S5_GROUPS * S5_STATE
LRU_BLOCKS, LRU_BW, LRU_C = 4, 64, 8.0
PEER_HEADS, PEER_KEYS, PEER_TOPK, PEER_QDIM = 8, 128, 16, 128
PEER_EXPERTS = PEER_KEYS * PEER_KEYS

LANES = 128
VMEM_LIMIT = 56 * 1024 * 1024

Z_TN = 1024
Z_GATE_W = 4 * D_MODEL
Z_GLA, Z_RET, Z_LRU, Z_S5 = 0, 1024, 2048, 2560
Z_MIX_W = 3072
Z_W = Z_GATE_W + Z_MIX_W

SCAN_TB = 128
TOK_TM = 256
PRE_TM = 512
PEER_TM, PEER_TE, PEER_TS = 512, 2048, 512
NEG_INF = float("-inf")


def _cparams(sem):
    return pltpu.CompilerParams(dimension_semantics=sem, vmem_limit_bytes=VMEM_LIMIT)


def _nt(a, b):
    return lax.dot_general(a, b, (((1,), (1,)), ((), ())), preferred_element_type=F32)


def _tn(a, b):
    return lax.dot_general(a, b, (((0,), (0,)), ((), ())), preferred_element_type=F32)


def _mm(a, b):
    return jnp.dot(a, b, preferred_element_type=F32)


def _mm_hi(a, b):
    return jnp.dot(a, b, preferred_element_type=F32, precision=HI)


def _split_bf16(x):
    hi = x.astype(BF16)
    return hi, (x - hi.astype(F32)).astype(BF16)


def _mm_exact_rhs(a, b_exact):
    hi, lo = _split_bf16(a)
    return _mm(hi, b_exact) + _mm(lo, b_exact)


def _mm_exact_lhs(a_exact, b):
    hi, lo = _split_bf16(b)
    return _mm(a_exact, hi) + _mm(a_exact, lo)


def _sigmoid(x):
    return jax.nn.sigmoid(x)


def _silu(x):
    return x * jax.nn.sigmoid(x)


def _gelu(x):
    return jax.nn.gelu(x)


GELU_C0 = math.sqrt(2.0 / math.pi)
GELU_C1 = GELU_C0 * 0.044715
BF16_ROWS = 16
F32_ROWS = 8


def _gelu_bf16(x):
    hx = 0.5 * x
    return hx + hx * jnp.tanh(x * (GELU_C0 + GELU_C1 * (x * x)))


def _row_tile_bf16(row, n_rows):
    one = jnp.broadcast_to(row, (BF16_ROWS, row.shape[1])).astype(BF16)
    return jnp.concatenate([one] * (n_rows // BF16_ROWS), axis=0)


def _log_sigmoid(x):
    return jnp.minimum(x, 0.0) - jnp.log(1.0 + jnp.exp(-jnp.abs(x)))


def _softplus(x):
    return jnp.maximum(x, 0.0) + jnp.log(1.0 + jnp.exp(-jnp.abs(x)))


def _rms_mod(x, gain, sc, sh):
    ms = jnp.mean(x * x, axis=-1, keepdims=True)
    return x * lax.rsqrt(ms + EPS) * gain * (1.0 + sc) + sh


def _adaln_kernel(c_ref, w_ref, b_ref, o_ref):
    o_ref[...] = _mm_hi(_silu(c_ref[...]), w_ref[...]) + b_ref[...]


def _adaln(cond, w_mod, b_mod):
    n_l = w_mod.shape[0]
    rows = cond.shape[0]
    tn = 1536
    return pl.pallas_call(
        _adaln_kernel,
        grid=(n_l, N_MOD * D_MODEL // tn),
        in_specs=[
            pl.BlockSpec((rows, D_MODEL), lambda l, j: (0, 0)),
            pl.BlockSpec((None, D_MODEL, tn), lambda l, j: (l, 0, j)),
            pl.BlockSpec((None, 1, tn), lambda l, j: (l, 0, j)),
        ],
        out_specs=pl.BlockSpec((None, rows, tn), lambda l, j: (l, 0, j)),
        out_shape=jax.ShapeDtypeStruct((n_l, rows, N_MOD * D_MODEL), F32),
        compiler_params=_cparams(("parallel", "parallel")),
        name="adaln",
    )(cond, w_mod, b_mod.reshape(n_l, 1, N_MOD * D_MODEL))


def _premix_kernel(x_ref, sh_ref, sc_ref, g_ref, w_ref, zg_ref, zm_ref, h_scr):
    j = pl.program_id(1)

    @pl.when(j == 0)
    def _():
        h_scr[...] = _rms_mod(x_ref[...], g_ref[...], sc_ref[...], sh_ref[...]).astype(BF16)

    z = _mm(h_scr[...], w_ref[j])

    @pl.when(j < Z_GATE_W // Z_TN)
    def _():
        zg_ref[...] = z.astype(BF16)

    @pl.when(j >= Z_GATE_W // Z_TN)
    def _():
        zm_ref[...] = z


def _premix(x, mod3, gain, w, row_of_tile, tm):
    nt = x.shape[0]
    n_gate = Z_GATE_W // Z_TN
    return pl.pallas_call(
        _premix_kernel,
        grid=(nt // tm, Z_W // Z_TN),
        in_specs=[
            pl.BlockSpec((tm, D_MODEL), lambda i, j: (i, 0)),
            pl.BlockSpec((None, 1, D_MODEL), lambda i, j: (row_of_tile(i), 0, 0)),
            pl.BlockSpec((None, 1, D_MODEL), lambda i, j: (row_of_tile(i), 0, 1)),
            pl.BlockSpec((1, D_MODEL), lambda i, j: (0, 0)),
            pl.BlockSpec((Z_W // Z_TN, D_MODEL, Z_TN), lambda i, j: (0, 0, 0), pipeline_mode=pl.Buffered(1)),
        ],
        out_specs=[
            pl.BlockSpec((tm, Z_TN), lambda i, j: (i, jnp.minimum(j, n_gate - 1))),
            pl.BlockSpec((tm, Z_TN), lambda i, j: (i, jnp.maximum(j - n_gate, 0))),
        ],
        out_shape=[
            jax.ShapeDtypeStruct((nt, Z_GATE_W), BF16),
            jax.ShapeDtypeStruct((nt, Z_MIX_W), F32),
        ],
        scratch_shapes=[pltpu.VMEM((tm, D_MODEL), BF16)],
        compiler_params=_cparams(("parallel", "arbitrary")),
        name="premix",
    )(x, mod3, mod3, gain, w)


def _gla_kernel(z_ref, wd_ref, bd_ref, s0_ref, e_ref, ind_ref, m_ref, sfin_ref,
                la_scr, of_scr, st_scr, p_scr, cum_scr, k_scr, v_scr):
    t_len = z_ref.shape[0]
    c = GLA_CHUNK
    n_chunks = t_len // c
    hk = GLA_HEADS * GLA_DK
    hv = GLA_HEADS * GLA_DV
    scale = GLA_DK ** -0.5

    pre = _mm_hi(z_ref[:, 768:896], wd_ref[...]) + bd_ref[...]
    la_scr[...] = _log_sigmoid(pre) * (1.0 / GLA_TAU)

    ri = lax.broadcasted_iota(jnp.int32, (c, c), 0)
    ci = lax.broadcasted_iota(jnp.int32, (c, c), 1)
    tri_lo = (ri >= ci).astype(BF16)
    tri_up = (ri <= ci).astype(BF16)
    row = lax.broadcasted_iota(jnp.int32, (c, hk), 0)
    bd_mask = (lax.broadcasted_iota(jnp.int32, (hv, hk), 0) // GLA_DV
               == lax.broadcasted_iota(jnp.int32, (hv, hk), 1) // GLA_DK).astype(F32)

    def chunk(base, la, tri, reverse, slot):
        cum = _mm_exact_lhs(tri, la)
        q = z_ref[pl.ds(base, c), 0:128] * scale
        k = z_ref[pl.ds(base, c), 128:256]
        v = z_ref[pl.ds(base, c), 256:512]
        edge = cum[0:1, :] if reverse else cum[c - 1:c, :]
        st = st_scr[slot]
        o = _nt((q * jnp.exp(cum)).astype(BF16), st.astype(BF16))
        ke = k * jnp.exp(edge - cum)
        cum_scr[slot] = cum
        k_scr[slot] = k
        v_scr[slot] = v
        for j in range(c):
            d = jnp.minimum(cum - cum_scr[slot, j:j + 1, :], 0.0)
            p = q * k_scr[slot, j:j + 1, :] * jnp.exp(d)
            keep = (row <= j) if reverse else (row >= j)
            p_scr[slot, j * c:(j + 1) * c, :] = jnp.where(keep, p, 0.0).astype(BF16)
        pe = _mm(p_scr[slot], e_ref[...])
        for j in range(c):
            o = o + pe[j * c:(j + 1) * c, :] * v_scr[slot, j:j + 1, :]
        st_scr[slot] = st * jnp.exp(edge) + bd_mask * _tn(v.astype(BF16), ke.astype(BF16))
        return o

    st_scr[0] = s0_ref[0]
    st_scr[1] = s0_ref[1]

    def sweep(i, carry):
        bf = pl.multiple_of(i * c, c)
        bb = pl.multiple_of((n_chunks - 1 - i) * c, c)
        of_scr[pl.ds(bf, c), :] = chunk(bf, la_scr[pl.ds(bf, c), 0:128], tri_lo, False, 0)
        m_ref[pl.ds(bb, c), :] = chunk(bb, la_scr[pl.ds(bb, c), 128:256], tri_up, True, 1)
        return carry

    lax.fori_loop(0, n_chunks, sweep, 0)
    sfin_ref[0] = st_scr[0]
    sfin_ref[1] = st_scr[1]

    fb = 8 * c

    def finish(i, carry):
        base = pl.multiple_of(i * fb, fb)
        o = of_scr[pl.ds(base, fb), :] + m_ref[pl.ds(base, fb), :]
        ms = _mm_exact_rhs(o * o, ind_ref[...])
        g = z_ref[pl.ds(base, fb), 512:768]
        m_ref[pl.ds(base, fb), :] = o * lax.rsqrt(ms + EPS) * _silu(g)
        return carry

    lax.fori_loop(0, t_len // fb, finish, 0)


def _gla(z, row_off, n_b, t_len, wd, bd, s0t, e_mat, ind):
    hk, hv = GLA_HEADS * GLA_DK, GLA_HEADS * GLA_DV
    c = GLA_CHUNK
    return pl.pallas_call(
        _gla_kernel,
        grid=(n_b,),
        in_specs=[
            pl.BlockSpec((t_len, 1024), lambda b: (row_off + b, Z_GLA // 1024)),
            pl.BlockSpec((128, 256), lambda b: (0, 0)),
            pl.BlockSpec((1, 256), lambda b: (0, 0)),
            pl.BlockSpec((None, 2, hv, hk), lambda b: (b, 0, 0, 0)),
            pl.BlockSpec((hk, hv), lambda b: (0, 0)),
            pl.BlockSpec((hv, hv), lambda b: (0, 0)),
        ],
        out_specs=[
            pl.BlockSpec((t_len, BR_W), lambda b: (b, 0)),
            pl.BlockSpec((None, 2, hv, hk), lambda b: (b, 0, 0, 0)),
        ],
        out_shape=[
            jax.ShapeDtypeStruct((n_b * t_len, BR_W), F32),
            jax.ShapeDtypeStruct((n_b, 2, hv, hk), F32),
        ],
        scratch_shapes=[
            pltpu.VMEM((t_len, 256), F32),
            pltpu.VMEM((t_len, hv), F32),
            pltpu.VMEM((2, hv, hk), F32),
            pltpu.VMEM((2, c * c, hk), BF16),
            pltpu.VMEM((2, c, hk), F32),
            pltpu.VMEM((2, c, hk), F32),
            pltpu.VMEM((2, c, hv), F32),
        ],
        compiler_params=_cparams(("parallel",)),
        name="gla",
    )(z, wd, bd, s0t, e_mat, ind)


def _ret_kernel(z_ref, cos_ref, sin_ref, swap_ref, lgl_ref, lgs_ref, s0_ref, ind_ref, m_ref, sfin_ref,
                qk_scr, of_scr, st_scr, *, rope):
    t_len = z_ref.shape[0]
    c = RET_CHUNK
    n_chunks = t_len // c
    hk = RET_HEADS * RET_DK
    hv = RET_HEADS * RET_DV
    scale = RET_DK ** -0.5

    lgf = lgl_ref[0:1, :]
    lgb = lgl_ref[1:2, :]
    pos = lax.broadcasted_iota(jnp.int32, (c, hk), 0).astype(F32)
    wq_f = jnp.exp(lgf * (pos + 1.0))
    wk_f = jnp.exp(lgf * (c - 1.0 - pos))
    wq_b = jnp.exp(lgb * (c - pos))
    wk_b = jnp.exp(lgb * pos)
    dec_f = jnp.exp(lgf * float(c))
    dec_b = jnp.exp(lgb * float(c))

    ii = lax.broadcasted_iota(jnp.int32, (c, c), 0)
    jj = lax.broadcasted_iota(jnp.int32, (c, c), 1)
    rel = (ii - jj).astype(F32)
    dms = []
    for h in range(RET_HEADS):
        d_f = jnp.where(ii >= jj, jnp.exp(lgs_ref[0, h] * jnp.maximum(rel, 0.0)), 0.0)
        d_b = jnp.where(jj >= ii, jnp.exp(lgs_ref[1, h] * jnp.maximum(-rel, 0.0)), 0.0)
        dms.append(d_f + d_b)
    dmat = jnp.concatenate(dms, axis=1)

    ek_mask = (lax.broadcasted_iota(jnp.int32, (RET_HEADS * c, hk), 0) // c
               == lax.broadcasted_iota(jnp.int32, (RET_HEADS * c, hk), 1) // RET_DK).astype(F32)
    ev_mask = (lax.broadcasted_iota(jnp.int32, (RET_HEADS * c, hv), 0) // c
               == lax.broadcasted_iota(jnp.int32, (RET_HEADS * c, hv), 1) // RET_DV).astype(F32)
    bd_mask = (lax.broadcasted_iota(jnp.int32, (hv, hk), 0) // RET_DV
               == lax.broadcasted_iota(jnp.int32, (hv, hk), 1) // RET_DK).astype(F32)

    st_scr[...] = s0_ref[0]

    def fwd(i, carry):
        base = pl.multiple_of(i * c, c)
        q = z_ref[pl.ds(base, c), 0:128] * scale
        k = z_ref[pl.ds(base, c), 128:256]
        v = z_ref[pl.ds(base, c), 256:512]
        if rope:
            cs = cos_ref[pl.ds(base, c), :]
            sn = sin_ref[pl.ds(base, c), :]
            q = q * cs + _mm_exact_rhs(q, swap_ref[...]) * sn
            k = k * cs + _mm_exact_rhs(k, swap_ref[...]) * sn
        qk_scr[pl.ds(base, c), 0:128] = q
        qk_scr[pl.ds(base, c), 128:256] = k
        kexp = (jnp.concatenate([k] * RET_HEADS, axis=0) * ek_mask).astype(BF16)
        vexp = (jnp.concatenate([v] * RET_HEADS, axis=0) * ev_mask).astype(BF16)
        sc = _nt(q.astype(BF16), kexp) * dmat
        o = _mm(sc.astype(BF16), vexp)
        st = st_scr[...]
        o = o + _nt((q * wq_f).astype(BF16), st.astype(BF16))
        of_scr[pl.ds(base, c), :] = o
        st_scr[...] = st * dec_f + bd_mask * _tn(v.astype(BF16), (k * wk_f).astype(BF16))
        return carry

    lax.fori_loop(0, n_chunks, fwd, 0)
    sfin_ref[0] = st_scr[...]
    st_scr[...] = s0_ref[1]

    def bwd(i, carry):
        base = pl.multiple_of((n_chunks - 1 - i) * c, c)
        q = qk_scr[pl.ds(base, c), 0:128]
        k = qk_scr[pl.ds(base, c), 128:256]
        v = z_ref[pl.ds(base, c), 256:512]
        st = st_scr[...]
        o = of_scr[pl.ds(base, c), :] + _nt((q * wq_b).astype(BF16), st.astype(BF16))
        st_scr[...] = st * dec_b + bd_mask * _tn(v.astype(BF16), (k * wk_b).astype(BF16))
        ms = _mm_exact_rhs(o * o, ind_ref[...])
        g = z_ref[pl.ds(base, c), 512:768]
        m_ref[pl.ds(base, c), :] = o * lax.rsqrt(ms + EPS) * _silu(g)
        return carry

    lax.fori_loop(0, n_chunks, bwd, 0)
    sfin_ref[1] = st_scr[...]


def _ret(z, row_off, n_b, t_len, cos_t, sin_t, swap, lgl, lgs, s0t, ind, rope):
    hk, hv = RET_HEADS * RET_DK, RET_HEADS * RET_DV
    return pl.pallas_call(
        functools.partial(_ret_kernel, rope=rope),
        grid=(n_b,),
        in_specs=[
            pl.BlockSpec((t_len, 1024), lambda b: (row_off + b, Z_RET // 1024)),
            pl.BlockSpec((t_len, hk), lambda b: (0, 0)),
            pl.BlockSpec((t_len, hk), lambda b: (0, 0)),
            pl.BlockSpec((hk, hk), lambda b: (0, 0)),
            pl.BlockSpec((2, hk), lambda b: (0, 0)),
            pl.BlockSpec(memory_space=pltpu.SMEM),
            pl.BlockSpec((None, 2, hv, hk), lambda b: (b, 0, 0, 0)),
            pl.BlockSpec((hv, hv), lambda b: (0, 0)),
        ],
        out_specs=[
            pl.BlockSpec((t_len, BR_W), lambda b: (b, 0)),
            pl.BlockSpec((None, 2, hv, hk), lambda b: (b, 0, 0, 0)),
        ],
        out_shape=[
            jax.ShapeDtypeStruct((n_b * t_len, BR_W), F32),
            jax.ShapeDtypeStruct((n_b, 2, hv, hk), F32),
        ],
        scratch_shapes=[
            pltpu.VMEM((t_len, 2 * hk), F32),
            pltpu.VMEM((t_len, hv), F32),
            pltpu.VMEM((hv, hk), F32),
        ],
        compiler_params=_cparams(("parallel",)),
        name="ret",
    )(z, cos_t, sin_t, swap, lgl, lgs, s0t, ind)


def _s5_disc_kernel(are_ref, aim_ref, ldt_ref, coef_ref, pf_ref, pb_ref):
    tb = pf_ref.shape[0]
    re = jnp.minimum(are_ref[...], S5_RE_MAX)
    im = aim_ref[...]
    dt = jnp.exp(ldt_ref[...])
    er = jnp.exp(re * dt)
    lbr = er * jnp.cos(im * dt)
    lbi = er * jnp.sin(im * dt)
    den = re * re + im * im
    nr = lbr - 1.0
    coef_ref[:, 0:S5_CH] = (nr * re + lbi * im) / den
    coef_ref[:, S5_CH:] = (lbi * re - nr * im) / den
    t = lax.broadcasted_iota(jnp.int32, (tb, S5_CH), 0).astype(F32)
    nf = t + 1.0
    nb = float(tb) - t
    mf = jnp.exp(nf * (re[0:1] * dt[0:1]))
    pf_ref[:, 0:S5_CH] = mf * jnp.cos(nf * (im[0:1] * dt[0:1]))
    pf_ref[:, S5_CH:] = mf * jnp.sin(nf * (im[0:1] * dt[0:1]))
    mb = jnp.exp(nb * (re[1:2] * dt[1:2]))
    pb_ref[:, 0:S5_CH] = mb * jnp.cos(nb * (im[1:2] * dt[1:2]))
    pb_ref[:, S5_CH:] = mb * jnp.sin(nb * (im[1:2] * dt[1:2]))


def _s5_disc(are, aim, ldt, tb):
    return pl.pallas_call(
        _s5_disc_kernel,
        out_shape=[
            jax.ShapeDtypeStruct((2, 2 * S5_CH), F32),
            jax.ShapeDtypeStruct((tb, 2 * S5_CH), F32),
            jax.ShapeDtypeStruct((tb, 2 * S5_CH), F32),
        ],
        compiler_params=pltpu.CompilerParams(vmem_limit_bytes=VMEM_LIMIT),
        name="s5_disc",
    )(are, aim, ldt)


def _s5_kernel(u_ref, bbd_ref, cre_ref, cim_ref, coef_ref, pf_ref, pb_ref, dsk_ref, wglu_ref, h0_ref,
               m_ref, hfin_ref, y_scr, bu_scr, hr_scr, hi_scr, car_scr):
    t_len = u_ref.shape[0]
    tb = SCAN_TB
    sub = pf_ref.shape[0]
    n_blocks = t_len // tb
    row_in = lax.broadcasted_iota(jnp.int32, (tb, LANES), 0) % sub
    steps = [1 << s for s in range(int(math.log2(sub)))]

    def block(base, dr_i, reverse):
        p_ref = pb_ref if reverse else pf_ref
        u = u_ref[pl.ds(base, tb), :]
        bu_scr[...] = _mm(u.astype(BF16), bbd_ref[...])
        for g in range(S5_CH // LANES):
            lo, hi = g * LANES, (g + 1) * LANES
            br = bu_scr[:, lo:hi]
            bi = bu_scr[:, S5_CH + lo:S5_CH + hi]
            cr = coef_ref[dr_i:dr_i + 1, lo:hi]
            ci = coef_ref[dr_i:dr_i + 1, S5_CH + lo:S5_CH + hi]
            hr = cr * br - ci * bi
            hi_ = cr * bi + ci * br
            for d in steps:
                if reverse:
                    pr = p_ref[sub - d:sub - d + 1, lo:hi]
                    pi = p_ref[sub - d:sub - d + 1, S5_CH + lo:S5_CH + hi]
                    keep = row_in < sub - d
                    sr = jnp.where(keep, pltpu.roll(hr, tb - d, 0), 0.0)
                    si = jnp.where(keep, pltpu.roll(hi_, tb - d, 0), 0.0)
                else:
                    pr = p_ref[d - 1:d, lo:hi]
                    pi = p_ref[d - 1:d, S5_CH + lo:S5_CH + hi]
                    keep = row_in >= d
                    sr = jnp.where(keep, pltpu.roll(hr, d, 0), 0.0)
                    si = jnp.where(keep, pltpu.roll(hi_, d, 0), 0.0)
                hr, hi_ = hr + pr * sr - pi * si, hi_ + pr * si + pi * sr
            car = car_scr[0:1, lo:hi]
            cai = car_scr[0:1, S5_CH + lo:S5_CH + hi]
            pwr = p_ref[:, lo:hi]
            pwi = p_ref[:, S5_CH + lo:S5_CH + hi]
            n_grp = tb // sub
            for v in (range(n_grp - 1, -1, -1) if reverse else range(n_grp)):
                gr = hr[v * sub:(v + 1) * sub, :] + pwr * car - pwi * cai
                gi = hi_[v * sub:(v + 1) * sub, :] + pwr * cai + pwi * car
                hr_scr[v * sub:(v + 1) * sub, lo:hi] = gr
                hi_scr[v * sub:(v + 1) * sub, lo:hi] = gi
                edge = 0 if reverse else sub - 1
                car, cai = gr[edge:edge + 1, :], gi[edge:edge + 1, :]
            car_scr[0:1, lo:hi] = car
            car_scr[0:1, S5_CH + lo:S5_CH + hi] = cai
        y = _mm(hr_scr[...].astype(BF16), cre_ref[...]) - _mm(hi_scr[...].astype(BF16), cim_ref[...])
        return u, y

    car_scr[0:1, :] = h0_ref[0:1, :]

    def fwd(i, carry):
        base = pl.multiple_of(i * tb, tb)
        _, y = block(base, 0, False)
        y_scr[pl.ds(base, tb), :] = y
        return carry

    lax.fori_loop(0, n_blocks, fwd, 0)
    hfin_ref[0:1, :] = car_scr[0:1, :]
    car_scr[0:1, :] = h0_ref[1:2, :]

    def bwd(i, carry):
        base = pl.multiple_of((n_blocks - 1 - i) * tb, tb)
        u, y = block(base, 1, True)
        y = _gelu(y_scr[pl.ds(base, tb), :] + y + dsk_ref[...] * u)
        gg = _mm(y.astype(BF16), wglu_ref[...])
        m_ref[pl.ds(base, tb), :] = gg[:, 0:BR_W] * _sigmoid(gg[:, BR_W:])
        return carry

    lax.fori_loop(0, n_blocks, bwd, 0)
    hfin_ref[1:2, :] = car_scr[0:1, :]


def _s5(z, row_off, n_b, t_len, bbd, cre, cim, coef, pf, pb, dsk, wglu, h0):
    tb = SCAN_TB
    sub = pf.shape[0]
    full = lambda shape: pl.BlockSpec(shape, lambda b: (0,) * len(shape))
    return pl.pallas_call(
        _s5_kernel,
        grid=(n_b,),
        in_specs=[
            pl.BlockSpec((t_len, BR_W), lambda b: (row_off + b, Z_S5 // BR_W)),
            full((BR_W, 2 * S5_CH)),
            full((S5_CH, BR_W)),
            full((S5_CH, BR_W)),
            full((2, 2 * S5_CH)),
            full((sub, 2 * S5_CH)),
            full((sub, 2 * S5_CH)),
            full((1, BR_W)),
            full((BR_W, 2 * BR_W)),
            pl.BlockSpec((None, 2, 2 * S5_CH), lambda b: (b, 0, 0)),
        ],
        out_specs=[
            pl.BlockSpec((t_len, BR_W), lambda b: (b, 0)),
            pl.BlockSpec((None, 2, 2 * S5_CH), lambda b: (b, 0, 0)),
        ],
        out_shape=[
            jax.ShapeDtypeStruct((n_b * t_len, BR_W), F32),
            jax.ShapeDtypeStruct((n_b, 2, 2 * S5_CH), F32),
        ],
        scratch_shapes=[
            pltpu.VMEM((t_len, BR_W), F32),
            pltpu.VMEM((tb, 2 * S5_CH), F32),
            pltpu.VMEM((tb, S5_CH), F32),
            pltpu.VMEM((tb, S5_CH), F32),
            pltpu.VMEM((8, 2 * S5_CH), F32),
        ],
        compiler_params=_cparams(("parallel",)),
        name="s5",
    )(z, bbd, cre, cim, coef, pf, pb, dsk, wglu, h0)


def _lru_kernel(z_ref, cw_ref, cb_ref, wg_ref, bg_ref, lam_ref, h0_ref, m_ref, hfin_ref,
                xc_scr, hf_scr, car_scr):
    t_len = z_ref.shape[0]
    tb = SCAN_TB
    n_blocks = t_len // tb
    steps = [1 << s for s in range(int(math.log2(tb)))]

    x = z_ref[:, 0:BR_W]
    trow = lax.broadcasted_iota(jnp.int32, (t_len, BR_W), 0)
    xm1 = jnp.where(trow >= 1, pltpu.roll(x, 1, 0), 0.0)
    xp1 = jnp.where(trow < t_len - 1, pltpu.roll(x, t_len - 1, 0), 0.0)
    xp2 = jnp.where(trow < t_len - 2, pltpu.roll(x, t_len - 2, 0), 0.0)
    xc_scr[...] = (cw_ref[0:1, :] * xm1 + cw_ref[1:2, :] * x + cw_ref[2:3, :] * xp1
                   + cw_ref[3:4, :] * xp2 + cb_ref[...])

    row = lax.broadcasted_iota(jnp.int32, (tb, BR_W), 0)
    sp = _softplus(-lam_ref[...])

    def block(base, dr_i, reverse):
        xc = xc_scr[pl.ds(base, tb), :]
        off = dr_i * 2 * BR_W
        gates = _mm(xc.astype(BF16), wg_ref[:, off:off + 2 * BR_W]) + bg_ref[:, off:off + 2 * BR_W]
        r = _sigmoid(gates[:, 0:BR_W])
        ig = _sigmoid(gates[:, BR_W:])
        log_a = -LRU_C * r * sp[dr_i:dr_i + 1, :]
        a = jnp.exp(log_a)
        th = jnp.tanh(log_a)
        b = jnp.sqrt(-2.0 * th / (1.0 - th)) * (ig * xc)
        for d in steps:
            if reverse:
                keep = row < tb - d
                a_s = jnp.where(keep, pltpu.roll(a, tb - d, 0), 1.0)
                b_s = jnp.where(keep, pltpu.roll(b, tb - d, 0), 0.0)
            else:
                keep = row >= d
                a_s = jnp.where(keep, pltpu.roll(a, d, 0), 1.0)
                b_s = jnp.where(keep, pltpu.roll(b, d, 0), 0.0)
            b = b + a * b_s
            a = a * a_s
        h = b + a * car_scr[dr_i:dr_i + 1, :]
        edge = 0 if reverse else tb - 1
        car_scr[dr_i:dr_i + 1, :] = h[edge:edge + 1, :]
        return h

    car_scr[0:2, :] = h0_ref[...]

    def fwd(i, carry):
        base = pl.multiple_of(i * tb, tb)
        hf_scr[pl.ds(base, tb), :] = block(base, 0, False)
        return carry

    lax.fori_loop(0, n_blocks, fwd, 0)

    def bwd(i, carry):
        base = pl.multiple_of((n_blocks - 1 - i) * tb, tb)
        h = block(base, 1, True) + hf_scr[pl.ds(base, tb), :]
        m_ref[pl.ds(base, tb), :] = h * _gelu(z_ref[pl.ds(base, tb), BR_W:2 * BR_W])
        return carry

    lax.fori_loop(0, n_blocks, bwd, 0)
    hfin_ref[...] = car_scr[0:2, :]


def _lru(z, row_off, n_b, t_len, cw, cb, wg, bg, lam, h0):
    full = lambda shape: pl.BlockSpec(shape, lambda b: (0,) * len(shape))
    return pl.pallas_call(
        _lru_kernel,
        grid=(n_b,),
        in_specs=[
            pl.BlockSpec((t_len, 2 * BR_W), lambda b: (row_off + b, Z_LRU // (2 * BR_W))),
            full((4, BR_W)),
            full((1, BR_W)),
            full((BR_W, 4 * BR_W)),
            full((1, 4 * BR_W)),
            full((2, BR_W)),
            pl.BlockSpec((None, 2, BR_W), lambda b: (b, 0, 0)),
        ],
        out_specs=[
            pl.BlockSpec((t_len, BR_W), lambda b: (b, 0)),
            pl.BlockSpec((None, 2, BR_W), lambda b: (b, 0, 0)),
        ],
        out_shape=[
            jax.ShapeDtypeStruct((n_b * t_len, BR_W), F32),
            jax.ShapeDtypeStruct((n_b, 2, BR_W), F32),
        ],
        scratch_shapes=[
            pltpu.VMEM((t_len, BR_W), F32),
            pltpu.VMEM((t_len, BR_W), F32),
            pltpu.VMEM((8, BR_W), F32),
        ],
        compiler_params=_cparams(("parallel",)),
        name="lru",
    )(z, cw, cb, wg, bg, lam, h0)


def _merge_kernel(ma_ref, mb_ref, mc_ref, md_ref, zg_ref, x_ref, g1_ref, sh2_ref, sc2_ref, gain_ref,
                  wb_ref, wo_ref, xo_ref, h2t_ref):
    acc = None
    for n, m_ref in enumerate((ma_ref, mb_ref, mc_ref, md_ref)):
        proj = _mm(m_ref[...].astype(BF16), wb_ref[n])
        term = _sigmoid(zg_ref[:, n * D_MODEL:(n + 1) * D_MODEL].astype(F32)) * proj
        acc = term if acc is None else acc + term
    xn = x_ref[...] + g1_ref[...] * _mm(acc.astype(BF16), wo_ref[...])
    xo_ref[...] = xn
    h2t_ref[...] = _rms_mod(xn, gain_ref[...], sc2_ref[...], sh2_ref[...]).T.astype(BF16)


def _merge(ms, z, x, mod3, gain, wb, wo, row_of_tile, tm):
    nt = x.shape[0]
    modspec = lambda k: pl.BlockSpec((None, 1, D_MODEL), lambda i: (row_of_tile(i), 0, k))
    return pl.pallas_call(
        _merge_kernel,
        grid=(nt // tm,),
        in_specs=[pl.BlockSpec((tm, BR_W), lambda i: (i, 0))] * 4 + [
            pl.BlockSpec((tm, 4 * D_MODEL), lambda i: (i, 0)),
            pl.BlockSpec((tm, D_MODEL), lambda i: (i, 0)),
            modspec(2), modspec(3), modspec(4),
            pl.BlockSpec((1, D_MODEL), lambda i: (0, 0)),
            pl.BlockSpec((4, BR_W, D_MODEL), lambda i: (0, 0, 0)),
            pl.BlockSpec((D_MODEL, D_MODEL), lambda i: (0, 0)),
        ],
        out_specs=[
            pl.BlockSpec((tm, D_MODEL), lambda i: (i, 0)),
            pl.BlockSpec((D_MODEL, tm), lambda i: (0, i)),
        ],
        out_shape=[
            jax.ShapeDtypeStruct((nt, D_MODEL), F32),
            jax.ShapeDtypeStruct((D_MODEL, nt), BF16),
        ],
        compiler_params=_cparams(("parallel",)),
        name="merge",
    )(*ms, z, x, mod3, mod3, mod3, gain, wb, wo)


def _oddeven_merge_sort_pairs(n):
    pairs = []
    p = 1
    while p < n:
        k = p
        while k >= 1:
            for j in range(k % p, n - k, 2 * k):
                for i in range(min(k, n - j - k)):
                    if (i + j) // (2 * p) == (i + j + k) // (2 * p):
                        pairs.append((i + j, i + j + k))
            k //= 2
        p *= 2
    return pairs


_SORT16 = _oddeven_merge_sort_pairs(PEER_KEYS // F32_ROWS)


def _peer_kernel(h2t_ref, x_ref, g2_ref, wqt_ref, kbt_ref, *rest, te, ts):
    n_sub = te // ts
    u_refs, vt_refs = rest[:n_sub], rest[n_sub:2 * n_sub]
    xo_ref, n_scr, r1_scr, a_scr, b_scr, sc_scr, v_scr, at_scr, wa_scr, yt_scr = rest[2 * n_sub:]
    tm = h2t_ref.shape[1]
    n_lt = tm // LANES
    ic = te // PEER_KEYS
    c_idx = pl.program_id(1)
    nk = PEER_KEYS

    @pl.when(c_idx == 0)
    def _route():
        qt = _mm(wqt_ref[...], h2t_ref[...])
        sct = _mm(kbt_ref[...], qt.astype(BF16))
        for lt in range(n_lt):
            sc_scr[lt] = sct[:, lt * LANES:(lt + 1) * LANES]
        r8 = lax.broadcasted_iota(jnp.int32, (8, LANES), 0)

        def per_tile(lt, carry):
            def per_head(h, vo):
                o0 = pl.multiple_of(h * (2 * nk), nk)
                o1 = pl.multiple_of(h * (2 * nk) + nk, nk)
                cols = [sc_scr[lt, pl.ds(pl.multiple_of(o0 + F32_ROWS * v, F32_ROWS), F32_ROWS), :]
                        for v in range(nk // F32_ROWS)]
                for ca, cb in _SORT16:
                    cols[ca], cols[cb] = jnp.maximum(cols[ca], cols[cb]), jnp.minimum(cols[ca], cols[cb])
                for r in range(PEER_TOPK):
                    m = jnp.max(cols[0], axis=0, keepdims=True)
                    v_scr[vo + r:vo + r + 1, :] = m
                    hit = cols[0] == m
                    for kk in range(PEER_TOPK - 1 - r):
                        cols[kk] = jnp.where(hit, cols[kk + 1], cols[kk])
                rank1 = jnp.full((nk, LANES), float(PEER_TOPK), F32)
                xs = sc_scr[lt, pl.ds(o1, nk), :]
                for r in range(PEER_TOPK):
                    m = jnp.max(xs, axis=0, keepdims=True)
                    v_scr[vo + PEER_TOPK + r:vo + PEER_TOPK + r + 1, :] = m
                    hit = xs == m
                    rank1 = jnp.where(hit, float(r), rank1)
                    xs = jnp.where(hit, NEG_INF, xs)
                v0 = v_scr[vo:vo + PEER_TOPK, :]
                v1 = v_scr[vo + PEER_TOPK:vo + 2 * PEER_TOPK, :]
                pieces = [v0[0:1, :] + v1]
                for r0 in range(1, 8):
                    pieces.append(jnp.where(r8 < PEER_TOPK // (r0 + 1), v0[r0:r0 + 1, :] + v1[0:8, :], NEG_INF))
                pieces.append(v0[8:16, :] + v1[0:1, :])
                cand = jnp.concatenate(pieces, axis=0)
                top = v0[0:1, :] + v1[0:1, :]
                zsum = jnp.zeros_like(top)
                tau = top
                for r in range(PEER_TOPK):
                    tau = jnp.max(cand, axis=0, keepdims=True)
                    zsum = zsum + jnp.exp(tau - top)
                    cand = jnp.where(cand == tau, NEG_INF, cand)
                ho = pl.multiple_of(h * nk, nk)
                s0 = sc_scr[lt, pl.ds(o0, nk), :]
                s1 = sc_scr[lt, pl.ds(o1, nk), :]
                crank = jnp.zeros((PEER_TOPK, LANES), F32)
                for r1 in range(PEER_TOPK):
                    crank = crank + jnp.where(v0 + v1[r1:r1 + 1, :] >= tau, 1.0, 0.0)
                cnt = jnp.zeros((nk, LANES), F32)
                half = PEER_TOPK // 2
                for m in range(1, half + 1):
                    u_m = jnp.min(jnp.where(crank >= float(m), v0, jnp.inf), axis=0, keepdims=True)
                    cnt = jnp.where(s0 >= u_m, float(m), cnt)
                cnt = jnp.where(s0 >= v0[0:1, :], crank[0:1, :], cnt)
                n_scr[lt, pl.ds(ho, nk), :] = cnt
                r1_scr[lt, pl.ds(ho, nk), :] = rank1.astype(BF16)
                a_scr[lt, pl.ds(ho, nk), :] = jnp.exp(s0 - v0[0:1, :])
                b_scr[lt, pl.ds(ho, nk), :] = (jnp.exp(s1 - v1[0:1, :]) / zsum).astype(BF16)

            def per_pair(hp, carry2):
                per_head(2 * hp, 0)
                per_head(2 * hp + 1, 2 * PEER_TOPK)
                return carry2

            return lax.fori_loop(0, PEER_HEADS // 2, per_pair, carry)

        lax.fori_loop(0, n_lt, per_tile, 0)
        yt_scr[...] = jnp.zeros_like(yt_scr)

    groups = ts // nk

    def a_stage(k):
        at_scr[k % 2] = _mm(u_refs[k][...], h2t_ref[...]).astype(BF16)

    def y_stage(k):
        yt_scr[...] += _mm(vt_refs[k][...], wa_scr[k % 2])

    def w_block(k, lt, ii, dep):
        pk = BF16_ROWS
        nv = nk // pk
        cols = slice(lt * LANES, (lt + 1) * LANES)
        acc = [None] * nv
        for h in range(PEER_HEADS):
            row = h * nk + c_idx * ic + k * groups + ii
            n_t = _row_tile_bf16(n_scr[lt, pl.ds(row, 1), :] + dep, pk)
            a_t = _row_tile_bf16(a_scr[lt, pl.ds(row, 1), :] + dep, pk)
            for jv in range(nv):
                r1 = r1_scr[lt, h * nk + jv * pk:h * nk + (jv + 1) * pk, :]
                b1 = b_scr[lt, h * nk + jv * pk:h * nk + (jv + 1) * pk, :]
                term = jnp.where(r1 < n_t, b1, 0.0) * a_t
                acc[jv] = term if acc[jv] is None else acc[jv] + term
        out = None
        for jv in range(nv):
            rows = slice(ii * nk + jv * pk, ii * nk + (jv + 1) * pk)
            out = acc[jv] * _gelu_bf16(at_scr[k % 2, rows, cols])
            wa_scr[k % 2, rows, cols] = out
        last = out[0:1, :].astype(F32)
        return jnp.where((last < 2.0) & (last > -2.0), last, 1.0) * 0.0

    dep = jnp.zeros((1, LANES), F32)
    a_stage(0)
    for k in range(n_sub):
        if k + 1 < n_sub:
            a_stage(k + 1)
        if k >= 1:
            y_stage(k - 1)
        for lt in range(n_lt):
            for ii in range(groups):
                dep = w_block(k, lt, ii, dep)
    y_stage(n_sub - 1)

    @pl.when(c_idx == pl.num_programs(1) - 1)
    def _fin():
        xo_ref[...] = x_ref[...] + g2_ref[...] * yt_scr[...].T


def _peer(h2t, x, mod3, wqt, kbt, u, vt, row_of_tile, tm, te, ts):
    nt = x.shape[0]
    n_lt = tm // LANES
    n_rt = PEER_HEADS * PEER_KEYS
    n_sub = te // ts
    u_specs = [pl.BlockSpec((ts, D_MODEL), lambda i, c, k=k: (c * n_sub + k, 0)) for k in range(n_sub)]
    vt_specs = [pl.BlockSpec((None, D_MODEL, ts), lambda i, c, k=k: (c * n_sub + k, 0, 0)) for k in range(n_sub)]
    return pl.pallas_call(
        functools.partial(_peer_kernel, te=te, ts=ts),
        grid=(nt // tm, PEER_EXPERTS // te),
        in_specs=[
            pl.BlockSpec((D_MODEL, tm), lambda i, c: (0, i), pipeline_mode=pl.Buffered(1)),
            pl.BlockSpec((tm, D_MODEL), lambda i, c: (i, 0), pipeline_mode=pl.Buffered(1)),
            pl.BlockSpec((None, 1, D_MODEL), lambda i, c: (row_of_tile(i), 0, 5)),
            pl.BlockSpec((D_MODEL, D_MODEL), lambda i, c: (0, 0), pipeline_mode=pl.Buffered(1)),
            pl.BlockSpec((2 * n_rt, D_MODEL), lambda i, c: (0, 0), pipeline_mode=pl.Buffered(1)),
        ] + u_specs + vt_specs,
        out_specs=pl.BlockSpec((tm, D_MODEL), lambda i, c: (i, 0)),
        out_shape=jax.ShapeDtypeStruct((nt, D_MODEL), F32),
        scratch_shapes=[
            pltpu.VMEM((n_lt, n_rt, LANES), F32),
            pltpu.VMEM((n_lt, n_rt, LANES), BF16),
            pltpu.VMEM((n_lt, n_rt, LANES), F32),
            pltpu.VMEM((n_lt, n_rt, LANES), BF16),
            pltpu.VMEM((n_lt, 2 * n_rt, LANES), F32),
            pltpu.VMEM((4 * PEER_TOPK, LANES), F32),
            pltpu.VMEM((2, ts, tm), BF16),
            pltpu.VMEM((2, ts, tm), BF16),
            pltpu.VMEM((D_MODEL, tm), F32),
        ],
        compiler_params=_cparams(("parallel", "arbitrary")),
        name="peer",
    )(h2t, x, mod3, wqt, kbt, *([u] * n_sub), *([vt] * n_sub))


def _final_kernel(x_ref, g_ref, o_ref):
    x = x_ref[...]
    o_ref[...] = x * lax.rsqrt(jnp.mean(x * x, axis=-1, keepdims=True) + EPS) * g_ref[...]


def _final_norm(x, gain, tm):
    nt = x.shape[0]
    return pl.pallas_call(
        _final_kernel,
        grid=(nt // tm,),
        in_specs=[pl.BlockSpec((tm, D_MODEL), lambda i: (i, 0)), pl.BlockSpec((1, D_MODEL), lambda i: (0, 0))],
        out_specs=pl.BlockSpec((tm, D_MODEL), lambda i: (i, 0)),
        out_shape=jax.ShapeDtypeStruct((nt, D_MODEL), F32),
        compiler_params=_cparams(("parallel",)),
        name="final_norm",
    )(x, gain)


def _block_diag(blocks):
    n, r, c = blocks.shape
    eye = jnp.eye(n, dtype=blocks.dtype)
    return jnp.einsum('nrc,nm->nrmc', blocks, eye).reshape(n * r, n * c)


def _state_to_bd_t(s):
    b, two, h, k, v = s.shape
    eye = jnp.eye(h, dtype=s.dtype)
    return jnp.einsum('bdhkv,hg->bdhvgk', s, eye).reshape(b, two, h * v, h * k)


def _bd_t_to_state(st, h, k, v):
    b = st.shape[0]
    return jnp.einsum('bdhvhk->bdhkv', st.reshape(b, 2, h, v, h, k))


def _rope_tables(t_len):
    rows = t_len // GRID_W
    row = jnp.repeat(jnp.arange(rows), GRID_W).astype(F32)
    col = jnp.tile(jnp.arange(GRID_W), rows).astype(F32)
    n_freq = RET_DK // 4
    inv_freq = ROPE_BASE ** (-jnp.arange(n_freq, dtype=F32) / n_freq)
    ang = jnp.concatenate([row[:, None] * inv_freq, col[:, None] * inv_freq], axis=-1)
    cos, sin = jnp.cos(ang), jnp.sin(ang)
    cos_h = jnp.concatenate([cos, cos], axis=-1)
    sin_h = jnp.concatenate([-sin, sin], axis=-1)
    return jnp.tile(cos_h, (1, RET_HEADS)), jnp.tile(sin_h, (1, RET_HEADS))


def _swap_matrix():
    lane = jnp.arange(RET_HEADS * RET_DK)
    half = RET_DK // 2
    src = jnp.where(lane % RET_DK < half, lane + half, lane - half)
    return (lane[:, None] == src[None, :]).astype(BF16)


def kernel(x_prompt, x_sample, c, state_gla, state_ret, state_s5, state_lru, c_ctx, w_mod, b_mod, norm_mix, norm_ffn, norm_final, w_in, gla_w_decay, gla_b_decay, ret_decay_logit, s5_a_re, s5_a_im, s5_log_dt, s5_b_re, s5_b_im, s5_c_re, s5_c_im, s5_d, s5_w_glu, lru_conv_w, lru_conv_b, lru_w_a, lru_b_a, lru_w_x, lru_b_x, lru_lambda, w_branch, w_out, peer_w_q, peer_keys, peer_u, peer_v):
    n_bp, t_p, _ = x_prompt.shape
    n_bs, t_s, _ = x_sample.shape
    depth = w_in.shape[0]
    ntp, nts = n_bp * t_p, n_bs * t_s
    tm = TOK_TM
    assert ntp % t_s == 0 and all(ntp % t == 0 and t_s % t == 0 for t in (TOK_TM, PRE_TM, PEER_TM))

    x = jnp.concatenate([x_prompt.reshape(ntp, D_MODEL), x_sample.reshape(nts, D_MODEL)], axis=0)

    n_rows = 8 * ((1 + n_bs + 7) // 8)
    cond = jnp.zeros((n_rows, D_MODEL), F32).at[0].set(c_ctx).at[1:1 + n_bs].set(c)
    mods = _adaln(cond, w_mod, b_mod)

    def make_row_of_tile(tile):
        def row_of_tile(i):
            return jnp.where(i < ntp // tile, 0, 1 + (i - ntp // tile) // (t_s // tile))
        return row_of_tile

    zpad = lambda n: jnp.zeros((depth, D_MODEL, n), F32)
    w_in_p = jnp.concatenate([w_in[:, :, 2336:6432], w_in[:, :, 0:800], zpad(224), w_in[:, :, 800:1568], zpad(256),
                              w_in[:, :, 1824:2336], w_in[:, :, 1568:1824], zpad(256)], axis=2).astype(BF16)
    assert w_in_p.shape[2] == Z_W
    w_in_p = w_in_p.reshape(depth, D_MODEL, Z_W // Z_TN, Z_TN).transpose(0, 2, 1, 3)

    hk, hv = GLA_HEADS * GLA_DK, GLA_HEADS * GLA_DV
    e_mat = (jnp.arange(hk)[:, None] // GLA_DK == jnp.arange(hv)[None, :] // GLA_DV).astype(BF16)
    ind = ((jnp.arange(hv)[:, None] // GLA_DV == jnp.arange(hv)[None, :] // GLA_DV).astype(F32) / GLA_DV).astype(BF16)
    swap = _swap_matrix()
    cos_s, sin_s = _rope_tables(t_s)
    cos_p, sin_p = jnp.ones((t_p, hk), F32), jnp.zeros((t_p, hk), F32)

    zeros_bd = jnp.zeros((n_bp, 2, hv, hk), F32)
    zeros_s5 = jnp.zeros((n_bp, 2, 2 * S5_CH), F32)
    zeros_lru = jnp.zeros((n_bp, 2, BR_W), F32)

    gla_l, ret_l, s5_l, lru_l = [], [], [], []
    for l in range(depth):
        mod3 = mods[l].reshape(n_rows, 1, N_MOD * D_MODEL)
        zg, z = _premix(x, mod3, norm_mix[l].reshape(1, D_MODEL), w_in_p[l], make_row_of_tile(PRE_TM), PRE_TM)

        wd = jnp.zeros((128, 256), F32)
        wd = wd.at[0:GLA_RANK, 0:hk].set(gla_w_decay[l, 0]).at[GLA_RANK:2 * GLA_RANK, hk:].set(gla_w_decay[l, 1])
        bd = gla_b_decay[l].reshape(1, 2 * hk)
        lg = jax.nn.log_sigmoid(ret_decay_logit[l].astype(F32))
        lgl = jnp.repeat(lg, RET_DK, axis=1)
        bre = _block_diag(jnp.swapaxes(s5_b_re[l], 1, 2))
        bim = _block_diag(jnp.swapaxes(s5_b_im[l], 1, 2))
        bbd = jnp.concatenate([bre, bim], axis=1).astype(BF16)
        cre = _block_diag(jnp.swapaxes(s5_c_re[l], 1, 2)).astype(BF16)
        cim = _block_diag(jnp.swapaxes(s5_c_im[l], 1, 2)).astype(BF16)
        coef, pf, pb = _s5_disc(s5_a_re[l].reshape(2, S5_CH), s5_a_im[l].reshape(2, S5_CH),
                                jnp.repeat(s5_log_dt[l], S5_STATE, axis=1), F32_ROWS)
        dsk = s5_d[l].reshape(1, BR_W)
        wglu = s5_w_glu[l].astype(BF16)
        wg = jnp.concatenate([_block_diag(lru_w_a[l, 0]), _block_diag(lru_w_x[l, 0]),
                              _block_diag(lru_w_a[l, 1]), _block_diag(lru_w_x[l, 1])], axis=1).astype(BF16)
        bg = jnp.concatenate([lru_b_a[l, 0], lru_b_x[l, 0], lru_b_a[l, 1], lru_b_x[l, 1]]).reshape(1, 4 * BR_W)
        cw = lru_conv_w[l]
        cb = lru_conv_b[l].reshape(1, BR_W)
        lam = lru_lambda[l]

        s5_h0 = state_s5[:, l].reshape(n_bs, 2, 2 * S5_CH)

        outs = []
        for (row_off, n_b, t_len, sg, sr, ss, sl, cs, sn, rope) in (
                (0, n_bp, t_p, zeros_bd, zeros_bd, zeros_s5, zeros_lru, cos_p, sin_p, False),
                (ntp // t_s, n_bs, t_s, _state_to_bd_t(state_gla[:, l]), _state_to_bd_t(state_ret[:, l]),
                 s5_h0, state_lru[:, l], cos_s, sin_s, True)):
            m_a, f_gla = _gla(z, row_off, n_b, t_len, wd, bd, sg, e_mat, ind)
            m_b, f_ret = _ret(z, row_off, n_b, t_len, cs, sn, swap, lgl, lg, sr, ind, rope)
            m_c, f_s5 = _s5(z, row_off, n_b, t_len, bbd, cre, cim, coef, pf, pb, dsk, wglu, ss)
            m_d, f_lru = _lru(z, row_off, n_b, t_len, cw, cb, wg, bg, lam, sl)
            outs.append(((m_a, m_b, m_c, m_d), (f_gla, f_ret, f_s5, f_lru)))

        ms = [jnp.concatenate([outs[0][0][n], outs[1][0][n]], axis=0) for n in range(4)]
        f_gla, f_ret, f_s5, f_lru = outs[0][1]
        gla_l.append(_bd_t_to_state(f_gla, GLA_HEADS, GLA_DK, GLA_DV))
        ret_l.append(_bd_t_to_state(f_ret, RET_HEADS, RET_DK, RET_DV))
        s5_l.append(f_s5.reshape(n_bp, 2, 2, S5_GROUPS, S5_STATE))
        lru_l.append(f_lru)

        x, h2t = _merge(ms, zg, x, mod3, norm_ffn[l].reshape(1, D_MODEL), w_branch[l].astype(BF16),
                        w_out[l].astype(BF16), make_row_of_tile(tm), tm)

        wqt = peer_w_q[l].T.astype(BF16)
        vt_l = peer_v[l].reshape(PEER_EXPERTS // PEER_TS, PEER_TS, D_MODEL).transpose(0, 2, 1).astype(BF16)
        kbt = _block_diag(peer_keys[l].reshape(2 * PEER_HEADS, PEER_KEYS, PEER_QDIM // 2)).astype(BF16)
        x = _peer(h2t, x, mod3, wqt, kbt, peer_u[l].astype(BF16), vt_l,
                  make_row_of_tile(PEER_TM), PEER_TM, PEER_TE, PEER_TS)

    y = _final_norm(x, norm_final.reshape(1, D_MODEL), tm)
    y_p = y[:ntp].reshape(n_bp, t_p, D_MODEL)
    y_s = y[ntp:].reshape(n_bs, t_s, D_MODEL)
    return (y_p, y_s, jnp.stack(gla_l, axis=1), jnp.stack(ret_l, axis=1),
            jnp.stack(s5_l, axis=1), jnp.stack(lru_l, axis=1))
```

```python
import functools
import math

import jax
import jax.numpy as jnp
from jax import lax
from jax.experimental import pallas as pl
from jax.experimental.pallas import tpu as pltpu

F32 = jnp.float32
BF16 = jnp.bfloat16
HI = lax.Precision.HIGHEST

D_MODEL = 1024
N_MOD = 6
EPS = 1e-6
BR_W = 256
GLA_HEADS, GLA_DK, GLA_DV, GLA_RANK, GLA_TAU, GLA_CHUNK = 4, 32, 64, 16, 16.0, 32
RET_HEADS, RET_DK, RET_DV, RET_CHUNK = 4, 32, 64, 64
ROPE_BASE = 10000.0
GRID_W = 64
S5_GROUP, S5_GROUPS, S5_STATE, S5_RE_MAX = 16, 16, 64, -1e-4
S5_CH = S5_GROUPS * S5_STATE
LRU_BLOCKS, LRU_BW, LRU_C = 4, 64, 8.0
PEER_HEADS, PEER_KEYS, PEER_TOPK, PEER_QDIM = 8, 128, 16, 128
PEER_EXPERTS = PEER_KEYS * PEER_KEYS

LANES = 128
VMEM_LIMIT = 58 * 1024 * 1024

Z_TN = 1024
Z_GATE_W = 4 * D_MODEL
Z_GLA, Z_RET, Z_LRU, Z_S5 = 0, 1024, 2048, 2560
Z_MIX_W = 3072
Z_W = Z_GATE_W + Z_MIX_W

SCAN_TB = 128
TOK_TM = 256
PRE_TM = 512
PEER_TM, PEER_TE, PEER_TS = 512, 2048, 512
NEG_INF = float("-inf")


def _cparams(sem):
    return pltpu.CompilerParams(dimension_semantics=sem, vmem_limit_bytes=VMEM_LIMIT)


def _nt(a, b):
    return lax.dot_general(a, b, (((1,), (1,)), ((), ())), preferred_element_type=F32)


def _tn(a, b):
    return lax.dot_general(a, b, (((0,), (0,)), ((), ())), preferred_element_type=F32)


def _mm(a, b):
    return jnp.dot(a, b, preferred_element_type=F32)


def _mm_hi(a, b):
    return jnp.dot(a, b, preferred_element_type=F32, precision=HI)


def _split_bf16(x):
    hi = x.astype(BF16)
    return hi, (x - hi.astype(F32)).astype(BF16)


def _mm_exact_rhs(a, b_exact):
    hi, lo = _split_bf16(a)
    return _mm(hi, b_exact) + _mm(lo, b_exact)


def _mm_exact_lhs(a_exact, b):
    hi, lo = _split_bf16(b)
    return _mm(a_exact, hi) + _mm(a_exact, lo)


def _sigmoid(x):
    return jax.nn.sigmoid(x)


def _silu(x):
    return x * jax.nn.sigmoid(x)


def _gelu(x):
    return jax.nn.gelu(x)


GELU_C0 = math.sqrt(2.0 / math.pi)
GELU_C1 = GELU_C0 * 0.044715
BF16_ROWS = 16
F32_ROWS = 8


def _gelu_tanh(x):
    hx = 0.5 * x
    return hx + hx * jnp.tanh(x * (GELU_C0 + GELU_C1 * (x * x)))


def _log_sigmoid(x):
    return jnp.minimum(x, 0.0) - jnp.log(1.0 + jnp.exp(-jnp.abs(x)))


def _softplus(x):
    return jnp.maximum(x, 0.0) + jnp.log(1.0 + jnp.exp(-jnp.abs(x)))


def _rms_mod(x, gain, sc, sh):
    ms = jnp.mean(x * x, axis=-1, keepdims=True)
    return x * lax.rsqrt(ms + EPS) * gain * (1.0 + sc) + sh


def _adaln_kernel(c_ref, w_ref, b_ref, o_ref):
    o_ref[...] = _mm_hi(_silu(c_ref[...]), w_ref[...]) + b_ref[...]


def _adaln(cond, w_mod, b_mod):
    n_l = w_mod.shape[0]
    rows = cond.shape[0]
    tn = 1536
    return pl.pallas_call(
        _adaln_kernel,
        grid=(n_l, N_MOD * D_MODEL // tn),
        in_specs=[
            pl.BlockSpec((rows, D_MODEL), lambda l, j: (0, 0)),
            pl.BlockSpec((None, D_MODEL, tn), lambda l, j: (l, 0, j)),
            pl.BlockSpec((None, 1, tn), lambda l, j: (l, 0, j)),
        ],
        out_specs=pl.BlockSpec((None, rows, tn), lambda l, j: (l, 0, j)),
        out_shape=jax.ShapeDtypeStruct((n_l, rows, N_MOD * D_MODEL), F32),
        compiler_params=_cparams(("parallel", "parallel")),
        name="adaln",
    )(cond, w_mod, b_mod.reshape(n_l, 1, N_MOD * D_MODEL))


def _premix_kernel(x_ref, sh_ref, sc_ref, g_ref, w_ref, zg_ref, zm_ref, h_scr):
    j = pl.program_id(1)

    @pl.when(j == 0)
    def _():
        h_scr[...] = _rms_mod(x_ref[...], g_ref[...], sc_ref[...], sh_ref[...]).astype(BF16)

    z = _mm(h_scr[...], w_ref[j])

    @pl.when(j < Z_GATE_W // Z_TN)
    def _():
        zg_ref[...] = z.astype(BF16)

    @pl.when(j >= Z_GATE_W // Z_TN)
    def _():
        zm_ref[...] = z


def _premix(x, mod3, gain, w, row_of_tile, tm):
    nt = x.shape[0]
    n_gate = Z_GATE_W // Z_TN
    return pl.pallas_call(
        _premix_kernel,
        grid=(nt // tm, Z_W // Z_TN),
        in_specs=[
            pl.BlockSpec((tm, D_MODEL), lambda i, j: (i, 0)),
            pl.BlockSpec((None, 1, D_MODEL), lambda i, j: (row_of_tile(i), 0, 0)),
            pl.BlockSpec((None, 1, D_MODEL), lambda i, j: (row_of_tile(i), 0, 1)),
            pl.BlockSpec((1, D_MODEL), lambda i, j: (0, 0)),
            pl.BlockSpec((Z_W // Z_TN, D_MODEL, Z_TN), lambda i, j: (0, 0, 0), pipeline_mode=pl.Buffered(1)),
        ],
        out_specs=[
            pl.BlockSpec((tm, Z_TN), lambda i, j: (i, jnp.minimum(j, n_gate - 1))),
            pl.BlockSpec((tm, Z_TN), lambda i, j: (i, jnp.maximum(j - n_gate, 0))),
        ],
        out_shape=[
            jax.ShapeDtypeStruct((nt, Z_GATE_W), BF16),
            jax.ShapeDtypeStruct((nt, Z_MIX_W), F32),
        ],
        scratch_shapes=[pltpu.VMEM((tm, D_MODEL), BF16)],
        compiler_params=_cparams(("parallel", "arbitrary")),
        name="premix",
    )(x, mod3, mod3, gain, w)


def _gla_kernel(z_ref, wd_ref, bd_ref, s0_ref, e_ref, ind_ref, m_ref, sfin_ref,
                la_scr, of_scr, st_scr, p_scr, cum_scr, k_scr, v_scr):
    t_len = z_ref.shape[0]
    c = GLA_CHUNK
    n_chunks = t_len // c
    hk = GLA_HEADS * GLA_DK
    hv = GLA_HEADS * GLA_DV
    scale = GLA_DK ** -0.5

    pre = _mm_hi(z_ref[:, 768:896], wd_ref[...]) + bd_ref[...]
    la_scr[...] = _log_sigmoid(pre) * (1.0 / GLA_TAU)

    ri = lax.broadcasted_iota(jnp.int32, (c, c), 0)
    ci = lax.broadcasted_iota(jnp.int32, (c, c), 1)
    tri_lo = (ri >= ci).astype(BF16)
    tri_up = (ri <= ci).astype(BF16)
    row = lax.broadcasted_iota(jnp.int32, (c, hk), 0)
    bd_mask = (lax.broadcasted_iota(jnp.int32, (hv, hk), 0) // GLA_DV
               == lax.broadcasted_iota(jnp.int32, (hv, hk), 1) // GLA_DK).astype(F32)

    def chunk(base, la, tri, reverse, slot):
        cum = _mm_exact_lhs(tri, la)
        q = z_ref[pl.ds(base, c), 0:128] * scale
        k = z_ref[pl.ds(base, c), 128:256]
        v = z_ref[pl.ds(base, c), 256:512]
        edge = cum[0:1, :] if reverse else cum[c - 1:c, :]
        st = st_scr[slot]
        o = _nt((q * jnp.exp(cum)).astype(BF16), st.astype(BF16))
        ke = k * jnp.exp(edge - cum)
        cum_scr[slot] = cum
        k_scr[slot] = k
        v_scr[slot] = v
        for j in range(c):
            d = jnp.minimum(cum - cum_scr[slot, j:j + 1, :], 0.0)
            p = q * k_scr[slot, j:j + 1, :] * jnp.exp(d)
            keep = (row <= j) if reverse else (row >= j)
            p_scr[slot, j * c:(j + 1) * c, :] = jnp.where(keep, p, 0.0).astype(BF16)
        pe = _mm(p_scr[slot], e_ref[...])
        for j in range(c):
            o = o + pe[j * c:(j + 1) * c, :] * v_scr[slot, j:j + 1, :]
        st_scr[slot] = st * jnp.exp(edge) + bd_mask * _tn(v.astype(BF16), ke.astype(BF16))
        return o

    st_scr[0] = s0_ref[0]
    st_scr[1] = s0_ref[1]

    def sweep(i, carry):
        bf = pl.multiple_of(i * c, c)
        bb = pl.multiple_of((n_chunks - 1 - i) * c, c)
        of_scr[pl.ds(bf, c), :] = chunk(bf, la_scr[pl.ds(bf, c), 0:128], tri_lo, False, 0)
        m_ref[pl.ds(bb, c), :] = chunk(bb, la_scr[pl.ds(bb, c), 128:256], tri_up, True, 1)
        return carry

    lax.fori_loop(0, n_chunks, sweep, 0)
    sfin_ref[0] = st_scr[0]
    sfin_ref[1] = st_scr[1]

    fb = 8 * c

    def finish(i, carry):
        base = pl.multiple_of(i * fb, fb)
        o = of_scr[pl.ds(base, fb), :] + m_ref[pl.ds(base, fb), :]
        ms = _mm_exact_rhs(o * o, ind_ref[...])
        g = z_ref[pl.ds(base, fb), 512:768]
        m_ref[pl.ds(base, fb), :] = o * lax.rsqrt(ms + EPS) * _silu(g)
        return carry

    lax.fori_loop(0, t_len // fb, finish, 0)


def _gla(z, row_off, n_b, t_len, wd, bd, s0t, e_mat, ind):
    hk, hv = GLA_HEADS * GLA_DK, GLA_HEADS * GLA_DV
    c = GLA_CHUNK
    return pl.pallas_call(
        _gla_kernel,
        grid=(n_b,),
        in_specs=[
            pl.BlockSpec((t_len, 1024), lambda b: (row_off + b, Z_GLA // 1024)),
            pl.BlockSpec((128, 256), lambda b: (0, 0)),
            pl.BlockSpec((1, 256), lambda b: (0, 0)),
            pl.BlockSpec((None, 2, hv, hk), lambda b: (b, 0, 0, 0)),
            pl.BlockSpec((hk, hv), lambda b: (0, 0)),
            pl.BlockSpec((hv, hv), lambda b: (0, 0)),
        ],
        out_specs=[
            pl.BlockSpec((t_len, BR_W), lambda b: (b, 0)),
            pl.BlockSpec((None, 2, hv, hk), lambda b: (b, 0, 0, 0)),
        ],
        out_shape=[
            jax.ShapeDtypeStruct((n_b * t_len, BR_W), F32),
            jax.ShapeDtypeStruct((n_b, 2, hv, hk), F32),
        ],
        scratch_shapes=[
            pltpu.VMEM((t_len, 256), F32),
            pltpu.VMEM((t_len, hv), F32),
            pltpu.VMEM((2, hv, hk), F32),
            pltpu.VMEM((2, c * c, hk), BF16),
            pltpu.VMEM((2, c, hk), F32),
            pltpu.VMEM((2, c, hk), F32),
            pltpu.VMEM((2, c, hv), F32),
        ],
        compiler_params=_cparams(("parallel",)),
        name="gla",
    )(z, wd, bd, s0t, e_mat, ind)


def _ret_kernel(z_ref, cos_ref, sin_ref, swap_ref, lgl_ref, lgs_ref, s0_ref, ind_ref, m_ref, sfin_ref,
                qk_scr, of_scr, st_scr, *, rope):
    t_len = z_ref.shape[0]
    c = RET_CHUNK
    n_chunks = t_len // c
    hk = RET_HEADS * RET_DK
    hv = RET_HEADS * RET_DV
    scale = RET_DK ** -0.5

    lgf = lgl_ref[0:1, :]
    lgb = lgl_ref[1:2, :]
    pos = lax.broadcasted_iota(jnp.int32, (c, hk), 0).astype(F32)
    wq_f = jnp.exp(lgf * (pos + 1.0))
    wk_f = jnp.exp(lgf * (c - 1.0 - pos))
    wq_b = jnp.exp(lgb * (c - pos))
    wk_b = jnp.exp(lgb * pos)
    dec_f = jnp.exp(lgf * float(c))
    dec_b = jnp.exp(lgb * float(c))

    ii = lax.broadcasted_iota(jnp.int32, (c, c), 0)
    jj = lax.broadcasted_iota(jnp.int32, (c, c), 1)
    rel = (ii - jj).astype(F32)
    dms = []
    for h in range(RET_HEADS):
        d_f = jnp.where(ii >= jj, jnp.exp(lgs_ref[0, h] * jnp.maximum(rel, 0.0)), 0.0)
        d_b = jnp.where(jj >= ii, jnp.exp(lgs_ref[1, h] * jnp.maximum(-rel, 0.0)), 0.0)
        dms.append(d_f + d_b)
    dmat = jnp.concatenate(dms, axis=1)

    ek_mask = (lax.broadcasted_iota(jnp.int32, (RET_HEADS * c, hk), 0) // c
               == lax.broadcasted_iota(jnp.int32, (RET_HEADS * c, hk), 1) // RET_DK).astype(F32)
    ev_mask = (lax.broadcasted_iota(jnp.int32, (RET_HEADS * c, hv), 0) // c
               == lax.broadcasted_iota(jnp.int32, (RET_HEADS * c, hv), 1) // RET_DV).astype(F32)
    bd_mask = (lax.broadcasted_iota(jnp.int32, (hv, hk), 0) // RET_DV
               == lax.broadcasted_iota(jnp.int32, (hv, hk), 1) // RET_DK).astype(F32)

    st_scr[...] = s0_ref[0]

    def fwd(i, carry):
        base = pl.multiple_of(i * c, c)
        q = z_ref[pl.ds(base, c), 0:128] * scale
        k = z_ref[pl.ds(base, c), 128:256]
        v = z_ref[pl.ds(base, c), 256:512]
        if rope:
            cs = cos_ref[pl.ds(base, c), :]
            sn = sin_ref[pl.ds(base, c), :]
            q = q * cs + _mm_exact_rhs(q, swap_ref[...]) * sn
            k = k * cs + _mm_exact_rhs(k, swap_ref[...]) * sn
        qk_scr[pl.ds(base, c), 0:128] = q
        qk_scr[pl.ds(base, c), 128:256] = k
        kexp = (jnp.concatenate([k] * RET_HEADS, axis=0) * ek_mask).astype(BF16)
        vexp = (jnp.concatenate([v] * RET_HEADS, axis=0) * ev_mask).astype(BF16)
        sc = _nt(q.astype(BF16), kexp) * dmat
        o = _mm(sc.astype(BF16), vexp)
        st = st_scr[...]
        o = o + _nt((q * wq_f).astype(BF16), st.astype(BF16))
        of_scr[pl.ds(base, c), :] = o
        st_scr[...] = st * dec_f + bd_mask * _tn(v.astype(BF16), (k * wk_f).astype(BF16))
        return carry

    lax.fori_loop(0, n_chunks, fwd, 0)
    sfin_ref[0] = st_scr[...]
    st_scr[...] = s0_ref[1]

    def bwd(i, carry):
        base = pl.multiple_of((n_chunks - 1 - i) * c, c)
        q = qk_scr[pl.ds(base, c), 0:128]
        k = qk_scr[pl.ds(base, c), 128:256]
        v = z_ref[pl.ds(base, c), 256:512]
        st = st_scr[...]
        o = of_scr[pl.ds(base, c), :] + _nt((q * wq_b).astype(BF16), st.astype(BF16))
        st_scr[...] = st * dec_b + bd_mask * _tn(v.astype(BF16), (k * wk_b).astype(BF16))
        ms = _mm_exact_rhs(o * o, ind_ref[...])
        g = z_ref[pl.ds(base, c), 512:768]
        m_ref[pl.ds(base, c), :] = o * lax.rsqrt(ms + EPS) * _silu(g)
        return carry

    lax.fori_loop(0, n_chunks, bwd, 0)
    sfin_ref[1] = st_scr[...]


def _ret(z, row_off, n_b, t_len, cos_t, sin_t, swap, lgl, lgs, s0t, ind, rope):
    hk, hv = RET_HEADS * RET_DK, RET_HEADS * RET_DV
    return pl.pallas_call(
        functools.partial(_ret_kernel, rope=rope),
        grid=(n_b,),
        in_specs=[
            pl.BlockSpec((t_len, 1024), lambda b: (row_off + b, Z_RET // 1024)),
            pl.BlockSpec((t_len, hk), lambda b: (0, 0)),
            pl.BlockSpec((t_len, hk), lambda b: (0, 0)),
            pl.BlockSpec((hk, hk), lambda b: (0, 0)),
            pl.BlockSpec((2, hk), lambda b: (0, 0)),
            pl.BlockSpec(memory_space=pltpu.SMEM),
            pl.BlockSpec((None, 2, hv, hk), lambda b: (b, 0, 0, 0)),
            pl.BlockSpec((hv, hv), lambda b: (0, 0)),
        ],
        out_specs=[
            pl.BlockSpec((t_len, BR_W), lambda b: (b, 0)),
            pl.BlockSpec((None, 2, hv, hk), lambda b: (b, 0, 0, 0)),
        ],
        out_shape=[
            jax.ShapeDtypeStruct((n_b * t_len, BR_W), F32),
            jax.ShapeDtypeStruct((n_b, 2, hv, hk), F32),
        ],
        scratch_shapes=[
            pltpu.VMEM((t_len, 2 * hk), F32),
            pltpu.VMEM((t_len, hv), F32),
            pltpu.VMEM((hv, hk), F32),
        ],
        compiler_params=_cparams(("parallel",)),
        name="ret",
    )(z, cos_t, sin_t, swap, lgl, lgs, s0t, ind)


def _s5_disc_kernel(are_ref, aim_ref, ldt_ref, coef_ref, pf_ref, pb_ref):
    tb = pf_ref.shape[0]
    re = jnp.minimum(are_ref[...], S5_RE_MAX)
    im = aim_ref[...]
    dt = jnp.exp(ldt_ref[...])
    er = jnp.exp(re * dt)
    lbr = er * jnp.cos(im * dt)
    lbi = er * jnp.sin(im * dt)
    den = re * re + im * im
    nr = lbr - 1.0
    coef_ref[:, 0:S5_CH] = (nr * re + lbi * im) / den
    coef_ref[:, S5_CH:] = (lbi * re - nr * im) / den
    t = lax.broadcasted_iota(jnp.int32, (tb, S5_CH), 0).astype(F32)
    nf = t + 1.0
    nb = float(tb) - t
    mf = jnp.exp(nf * (re[0:1] * dt[0:1]))
    pf_ref[:, 0:S5_CH] = mf * jnp.cos(nf * (im[0:1] * dt[0:1]))
    pf_ref[:, S5_CH:] = mf * jnp.sin(nf * (im[0:1] * dt[0:1]))
    mb = jnp.exp(nb * (re[1:2] * dt[1:2]))
    pb_ref[:, 0:S5_CH] = mb * jnp.cos(nb * (im[1:2] * dt[1:2]))
    pb_ref[:, S5_CH:] = mb * jnp.sin(nb * (im[1:2] * dt[1:2]))


def _s5_disc(are, aim, ldt, tb):
    return pl.pallas_call(
        _s5_disc_kernel,
        out_shape=[
            jax.ShapeDtypeStruct((2, 2 * S5_CH), F32),
            jax.ShapeDtypeStruct((tb, 2 * S5_CH), F32),
            jax.ShapeDtypeStruct((tb, 2 * S5_CH), F32),
        ],
        compiler_params=pltpu.CompilerParams(vmem_limit_bytes=VMEM_LIMIT),
        name="s5_disc",
    )(are, aim, ldt)


def _s5_kernel(u_ref, bbd_ref, cre_ref, cim_ref, coef_ref, pf_ref, pb_ref, dsk_ref, wglu_ref, h0_ref,
               m_ref, hfin_ref, y_scr, bu_scr, hr_scr, hi_scr, car_scr):
    t_len = u_ref.shape[0]
    tb = SCAN_TB
    sub = pf_ref.shape[0]
    n_blocks = t_len // tb
    row_in = lax.broadcasted_iota(jnp.int32, (tb, LANES), 0) % sub
    steps = [1 << s for s in range(int(math.log2(sub)))]

    def block(base, dr_i, reverse):
        p_ref = pb_ref if reverse else pf_ref
        u = u_ref[pl.ds(base, tb), :]
        bu_scr[...] = _mm(u.astype(BF16), bbd_ref[...])
        for g in range(S5_CH // LANES):
            lo, hi = g * LANES, (g + 1) * LANES
            br = bu_scr[:, lo:hi]
            bi = bu_scr[:, S5_CH + lo:S5_CH + hi]
            cr = coef_ref[dr_i:dr_i + 1, lo:hi]
            ci = coef_ref[dr_i:dr_i + 1, S5_CH + lo:S5_CH + hi]
            hr = cr * br - ci * bi
            hi_ = cr * bi + ci * br
            for d in steps:
                if reverse:
                    pr = p_ref[sub - d:sub - d + 1, lo:hi]
                    pi = p_ref[sub - d:sub - d + 1, S5_CH + lo:S5_CH + hi]
                    keep = row_in < sub - d
                    sr = jnp.where(keep, pltpu.roll(hr, tb - d, 0), 0.0)
                    si = jnp.where(keep, pltpu.roll(hi_, tb - d, 0), 0.0)
                else:
                    pr = p_ref[d - 1:d, lo:hi]
                    pi = p_ref[d - 1:d, S5_CH + lo:S5_CH + hi]
                    keep = row_in >= d
                    sr = jnp.where(keep, pltpu.roll(hr, d, 0), 0.0)
                    si = jnp.where(keep, pltpu.roll(hi_, d, 0), 0.0)
                hr, hi_ = hr + pr * sr - pi * si, hi_ + pr * si + pi * sr
            car = car_scr[0:1, lo:hi]
            cai = car_scr[0:1, S5_CH + lo:S5_CH + hi]
            pwr = p_ref[:, lo:hi]
            pwi = p_ref[:, S5_CH + lo:S5_CH + hi]
            n_grp = tb // sub
            for v in (range(n_grp - 1, -1, -1) if reverse else range(n_grp)):
                gr = hr[v * sub:(v + 1) * sub, :] + pwr * car - pwi * cai
                gi = hi_[v * sub:(v + 1) * sub, :] + pwr * cai + pwi * car
                hr_scr[v * sub:(v + 1) * sub, lo:hi] = gr
                hi_scr[v * sub:(v + 1) * sub, lo:hi] = gi
                edge = 0 if reverse else sub - 1
                car, cai = gr[edge:edge + 1, :], gi[edge:edge + 1, :]
            car_scr[0:1, lo:hi] = car
            car_scr[0:1, S5_CH + lo:S5_CH + hi] = cai
        y = _mm(hr_scr[...].astype(BF16), cre_ref[...]) - _mm(hi_scr[...].astype(BF16), cim_ref[...])
        return u, y

    car_scr[0:1, :] = h0_ref[0:1, :]

    def fwd(i, carry):
        base = pl.multiple_of(i * tb, tb)
        _, y = block(base, 0, False)
        y_scr[pl.ds(base, tb), :] = y
        return carry

    lax.fori_loop(0, n_blocks, fwd, 0)
    hfin_ref[0:1, :] = car_scr[0:1, :]
    car_scr[0:1, :] = h0_ref[1:2, :]

    def bwd(i, carry):
        base = pl.multiple_of((n_blocks - 1 - i) * tb, tb)
        u, y = block(base, 1, True)
        y = _gelu(y_scr[pl.ds(base, tb), :] + y + dsk_ref[...] * u)
        gg = _mm(y.astype(BF16), wglu_ref[...])
        m_ref[pl.ds(base, tb), :] = gg[:, 0:BR_W] * _sigmoid(gg[:, BR_W:])
        return carry

    lax.fori_loop(0, n_blocks, bwd, 0)
    hfin_ref[1:2, :] = car_scr[0:1, :]


def _s5(z, row_off, n_b, t_len, bbd, cre, cim, coef, pf, pb, dsk, wglu, h0):
    tb = SCAN_TB
    sub = pf.shape[0]
    full = lambda shape: pl.BlockSpec(shape, lambda b: (0,) * len(shape))
    return pl.pallas_call(
        _s5_kernel,
        grid=(n_b,),
        in_specs=[
            pl.BlockSpec((t_len, BR_W), lambda b: (row_off + b, Z_S5 // BR_W)),
            full((BR_W, 2 * S5_CH)),
            full((S5_CH, BR_W)),
            full((S5_CH, BR_W)),
            full((2, 2 * S5_CH)),
            full((sub, 2 * S5_CH)),
            full((sub, 2 * S5_CH)),
            full((1, BR_W)),
            full((BR_W, 2 * BR_W)),
            pl.BlockSpec((None, 2, 2 * S5_CH), lambda b: (b, 0, 0)),
        ],
        out_specs=[
            pl.BlockSpec((t_len, BR_W), lambda b: (b, 0)),
            pl.BlockSpec((None, 2, 2 * S5_CH), lambda b: (b, 0, 0)),
        ],
        out_shape=[
            jax.ShapeDtypeStruct((n_b * t_len, BR_W), F32),
            jax.ShapeDtypeStruct((n_b, 2, 2 * S5_CH), F32),
        ],
        scratch_shapes=[
            pltpu.VMEM((t_len, BR_W), F32),
            pltpu.VMEM((tb, 2 * S5_CH), F32),
            pltpu.VMEM((tb, S5_CH), F32),
            pltpu.VMEM((tb, S5_CH), F32),
            pltpu.VMEM((8, 2 * S5_CH), F32),
        ],
        compiler_params=_cparams(("parallel",)),
        name="s5",
    )(z, bbd, cre, cim, coef, pf, pb, dsk, wglu, h0)


def _lru_kernel(z_ref, cw_ref, cb_ref, wg_ref, bg_ref, lam_ref, h0_ref, m_ref, hfin_ref,
                xc_scr, hf_scr, car_scr):
    t_len = z_ref.shape[0]
    tb = SCAN_TB
    n_blocks = t_len // tb
    steps = [1 << s for s in range(int(math.log2(tb)))]

    x = z_ref[:, 0:BR_W]
    trow = lax.broadcasted_iota(jnp.int32, (t_len, BR_W), 0)
    xm1 = jnp.where(trow >= 1, pltpu.roll(x, 1, 0), 0.0)
    xp1 = jnp.where(trow < t_len - 1, pltpu.roll(x, t_len - 1, 0), 0.0)
    xp2 = jnp.where(trow < t_len - 2, pltpu.roll(x, t_len - 2, 0), 0.0)
    xc_scr[...] = (cw_ref[0:1, :] * xm1 + cw_ref[1:2, :] * x + cw_ref[2:3, :] * xp1
                   + cw_ref[3:4, :] * xp2 + cb_ref[...])

    row = lax.broadcasted_iota(jnp.int32, (tb, BR_W), 0)
    sp = _softplus(-lam_ref[...])

    def block(base, dr_i, reverse):
        xc = xc_scr[pl.ds(base, tb), :]
        off = dr_i * 2 * BR_W
        gates = _mm(xc.astype(BF16), wg_ref[:, off:off + 2 * BR_W]) + bg_ref[:, off:off + 2 * BR_W]
        r = _sigmoid(gates[:, 0:BR_W])
        ig = _sigmoid(gates[:, BR_W:])
        log_a = -LRU_C * r * sp[dr_i:dr_i + 1, :]
        a = jnp.exp(log_a)
        th = jnp.tanh(log_a)
        b = jnp.sqrt(-2.0 * th / (1.0 - th)) * (ig * xc)
        for d in steps:
            if reverse:
                keep = row < tb - d
                a_s = jnp.where(keep, pltpu.roll(a, tb - d, 0), 1.0)
                b_s = jnp.where(keep, pltpu.roll(b, tb - d, 0), 0.0)
            else:
                keep = row >= d
                a_s = jnp.where(keep, pltpu.roll(a, d, 0), 1.0)
                b_s = jnp.where(keep, pltpu.roll(b, d, 0), 0.0)
            b = b + a * b_s
            a = a * a_s
        h = b + a * car_scr[dr_i:dr_i + 1, :]
        edge = 0 if reverse else tb - 1
        car_scr[dr_i:dr_i + 1, :] = h[edge:edge + 1, :]
        return h

    car_scr[0:2, :] = h0_ref[...]

    def fwd(i, carry):
        base = pl.multiple_of(i * tb, tb)
        hf_scr[pl.ds(base, tb), :] = block(base, 0, False)
        return carry

    lax.fori_loop(0, n_blocks, fwd, 0)

    def bwd(i, carry):
        base = pl.multiple_of((n_blocks - 1 - i) * tb, tb)
        h = block(base, 1, True) + hf_scr[pl.ds(base, tb), :]
        m_ref[pl.ds(base, tb), :] = h * _gelu(z_ref[pl.ds(base, tb), BR_W:2 * BR_W])
        return carry

    lax.fori_loop(0, n_blocks, bwd, 0)
    hfin_ref[...] = car_scr[0:2, :]


def _lru(z, row_off, n_b, t_len, cw, cb, wg, bg, lam, h0):
    full = lambda shape: pl.BlockSpec(shape, lambda b: (0,) * len(shape))
    return pl.pallas_call(
        _lru_kernel,
        grid=(n_b,),
        in_specs=[
            pl.BlockSpec((t_len, 2 * BR_W), lambda b: (row_off + b, Z_LRU // (2 * BR_W))),
            full((4, BR_W)),
            full((1, BR_W)),
            full((BR_W, 4 * BR_W)),
            full((1, 4 * BR_W)),
            full((2, BR_W)),
            pl.BlockSpec((None, 2, BR_W), lambda b: (b, 0, 0)),
        ],
        out_specs=[
            pl.BlockSpec((t_len, BR_W), lambda b: (b, 0)),
            pl.BlockSpec((None, 2, BR_W), lambda b: (b, 0, 0)),
        ],
        out_shape=[
            jax.ShapeDtypeStruct((n_b * t_len, BR_W), F32),
            jax.ShapeDtypeStruct((n_b, 2, BR_W), F32),
        ],
        scratch_shapes=[
            pltpu.VMEM((t_len, BR_W), F32),
            pltpu.VMEM((t_len, BR_W), F32),
            pltpu.VMEM((8, BR_W), F32),
        ],
        compiler_params=_cparams(("parallel",)),
        name="lru",
    )(z, cw, cb, wg, bg, lam, h0)


def _merge_kernel(ma_ref, mb_ref, mc_ref, md_ref, zg_ref, x_ref, g1_ref, sh2_ref, sc2_ref, gain_ref,
                  wb_ref, wo_ref, xo_ref, h2t_ref):
    acc = None
    for n, m_ref in enumerate((ma_ref, mb_ref, mc_ref, md_ref)):
        proj = _mm(m_ref[...].astype(BF16), wb_ref[n])
        term = _sigmoid(zg_ref[:, n * D_MODEL:(n + 1) * D_MODEL].astype(F32)) * proj
        acc = term if acc is None else acc + term
    xn = x_ref[...] + g1_ref[...] * _mm(acc.astype(BF16), wo_ref[...])
    xo_ref[...] = xn
    h2t_ref[...] = _rms_mod(xn, gain_ref[...], sc2_ref[...], sh2_ref[...]).T.astype(BF16)


def _merge(ms, z, x, mod3, gain, wb, wo, row_of_tile, tm):
    nt = x.shape[0]
    modspec = lambda k: pl.BlockSpec((None, 1, D_MODEL), lambda i: (row_of_tile(i), 0, k))
    return pl.pallas_call(
        _merge_kernel,
        grid=(nt // tm,),
        in_specs=[pl.BlockSpec((tm, BR_W), lambda i: (i, 0))] * 4 + [
            pl.BlockSpec((tm, 4 * D_MODEL), lambda i: (i, 0)),
            pl.BlockSpec((tm, D_MODEL), lambda i: (i, 0)),
            modspec(2), modspec(3), modspec(4),
            pl.BlockSpec((1, D_MODEL), lambda i: (0, 0)),
            pl.BlockSpec((4, BR_W, D_MODEL), lambda i: (0, 0, 0)),
            pl.BlockSpec((D_MODEL, D_MODEL), lambda i: (0, 0)),
        ],
        out_specs=[
            pl.BlockSpec((tm, D_MODEL), lambda i: (i, 0)),
            pl.BlockSpec((D_MODEL, tm), lambda i: (0, i)),
        ],
        out_shape=[
            jax.ShapeDtypeStruct((nt, D_MODEL), F32),
            jax.ShapeDtypeStruct((D_MODEL, nt), BF16),
        ],
        compiler_params=_cparams(("parallel",)),
        name="merge",
    )(*ms, z, x, mod3, mod3, mod3, gain, wb, wo)


def _oddeven_merge_sort_pairs(n):
    pairs = []
    p = 1
    while p < n:
        k = p
        while k >= 1:
            for j in range(k % p, n - k, 2 * k):
                for i in range(min(k, n - j - k)):
                    if (i + j) // (2 * p) == (i + j + k) // (2 * p):
                        pairs.append((i + j, i + j + k))
            k //= 2
        p *= 2
    return pairs


_SORT16 = _oddeven_merge_sort_pairs(PEER_KEYS // F32_ROWS)


def _peer_kernel(h2t_ref, x_ref, g2_ref, wqt_ref, kbt_ref, *rest, te, ts):
    n_sub = te // ts
    u_refs, vt_refs = rest[:n_sub], rest[n_sub:2 * n_sub]
    xo_ref, n_scr, r1_scr, a_scr, b_scr, sc_scr, v_scr, at_scr, wa_scr, yt_scr = rest[2 * n_sub:]
    tm = h2t_ref.shape[1]
    n_lt = tm // LANES
    ic = te // PEER_KEYS
    c_idx = pl.program_id(1)
    nk = PEER_KEYS

    @pl.when(c_idx == 0)
    def _route():
        qt = _mm(wqt_ref[...], h2t_ref[...])
        sct = _mm(kbt_ref[...], qt.astype(BF16))
        for lt in range(n_lt):
            sc_scr[lt] = sct[:, lt * LANES:(lt + 1) * LANES]
        r8 = lax.broadcasted_iota(jnp.int32, (8, LANES), 0)

        def per_tile(lt, carry):
            def per_head(h, vo):
                o0 = pl.multiple_of(h * (2 * nk), nk)
                o1 = pl.multiple_of(h * (2 * nk) + nk, nk)
                cols = [sc_scr[lt, pl.ds(pl.multiple_of(o0 + F32_ROWS * v, F32_ROWS), F32_ROWS), :]
                        for v in range(nk // F32_ROWS)]
                for ca, cb in _SORT16:
                    cols[ca], cols[cb] = jnp.maximum(cols[ca], cols[cb]), jnp.minimum(cols[ca], cols[cb])
                for r in range(PEER_TOPK):
                    m = jnp.max(cols[0], axis=0, keepdims=True)
                    v_scr[vo + r:vo + r + 1, :] = m
                    hit = cols[0] == m
                    for kk in range(PEER_TOPK - 1 - r):
                        cols[kk] = jnp.where(hit, cols[kk + 1], cols[kk])
                rank1 = jnp.full((nk, LANES), float(PEER_TOPK), F32)
                xs = sc_scr[lt, pl.ds(o1, nk), :]
                for r in range(PEER_TOPK):
                    m = jnp.max(xs, axis=0, keepdims=True)
                    v_scr[vo + PEER_TOPK + r:vo + PEER_TOPK + r + 1, :] = m
                    hit = xs == m
                    rank1 = jnp.where(hit, float(r), rank1)
                    xs = jnp.where(hit, NEG_INF, xs)
                v0 = v_scr[vo:vo + PEER_TOPK, :]
                v1 = v_scr[vo + PEER_TOPK:vo + 2 * PEER_TOPK, :]
                pieces = [v0[0:1, :] + v1]
                for r0 in range(1, 8):
                    pieces.append(jnp.where(r8 < PEER_TOPK // (r0 + 1), v0[r0:r0 + 1, :] + v1[0:8, :], NEG_INF))
                pieces.append(v0[8:16, :] + v1[0:1, :])
                cand = jnp.concatenate(pieces, axis=0)
                top = v0[0:1, :] + v1[0:1, :]
                zsum = jnp.zeros_like(top)
                tau = top
                for r in range(PEER_TOPK):
                    tau = jnp.max(cand, axis=0, keepdims=True)
                    zsum = zsum + jnp.exp(tau - top)
                    cand = jnp.where(cand == tau, NEG_INF, cand)
                ho = pl.multiple_of(h * nk, nk)
                s0 = sc_scr[lt, pl.ds(o0, nk), :]
                s1 = sc_scr[lt, pl.ds(o1, nk), :]
                crank = jnp.zeros((PEER_TOPK, LANES), F32)
                for r1 in range(PEER_TOPK):
                    crank = crank + jnp.where(v0 + v1[r1:r1 + 1, :] >= tau, 1.0, 0.0)
                cnt = jnp.zeros((nk, LANES), F32)
                half = PEER_TOPK // 2
                for m in range(1, half + 1):
                    u_m = jnp.min(jnp.where(crank >= float(m), v0, jnp.inf), axis=0, keepdims=True)
                    cnt = jnp.where(s0 >= u_m, float(m), cnt)
                cnt = jnp.where(s0 >= v0[0:1, :], crank[0:1, :], cnt)
                n_scr[lt, pl.ds(ho, nk), :] = cnt
                r1_scr[lt, pl.ds(ho, nk), :] = rank1
                a_scr[lt, pl.ds(ho, nk), :] = jnp.exp(s0 - v0[0:1, :])
                b_scr[lt, pl.ds(ho, nk), :] = jnp.exp(s1 - v1[0:1, :]) / zsum

            def per_pair(hp, carry2):
                per_head(2 * hp, 0)
                per_head(2 * hp + 1, 2 * PEER_TOPK)
                return carry2

            return lax.fori_loop(0, PEER_HEADS // 2, per_pair, carry)

        lax.fori_loop(0, n_lt, per_tile, 0)
        yt_scr[...] = jnp.zeros_like(yt_scr)

    groups = ts // nk

    def a_stage(k):
        at_scr[k % 2] = _mm(u_refs[k][...], h2t_ref[...])

    def y_stage(k):
        yt_scr[...] += _mm(vt_refs[k][...], wa_scr[k % 2])

    def w_block(k, lt, ii, dep):
        pk = BF16_ROWS
        nv = nk // pk
        cols = slice(lt * LANES, (lt + 1) * LANES)
        acc = [None] * nv
        for h in range(PEER_HEADS):
            row = h * nk + c_idx * ic + k * groups + ii
            n_t = jnp.broadcast_to(n_scr[lt, pl.ds(row, 1), :] + dep, (pk, LANES))
            a_t = jnp.broadcast_to(a_scr[lt, pl.ds(row, 1), :] + dep, (pk, LANES))
            for jv in range(nv):
                r1 = r1_scr[lt, h * nk + jv * pk:h * nk + (jv + 1) * pk, :]
                b1 = b_scr[lt, h * nk + jv * pk:h * nk + (jv + 1) * pk, :]
                term = jnp.where(r1 < n_t, b1, 0.0) * a_t
                acc[jv] = term if acc[jv] is None else acc[jv] + term
        out = None
        for jv in range(nv):
            rows = slice(ii * nk + jv * pk, ii * nk + (jv + 1) * pk)
            out = acc[jv] * _gelu_tanh(at_scr[k % 2, rows, cols])
            wa_scr[k % 2, rows, cols] = out.astype(BF16)
        last = out[0:1, :]
        return jnp.where((last < 2.0) & (last > -2.0), last, 1.0) * 0.0

    dep = jnp.zeros((1, LANES), F32)
    a_stage(0)
    for k in range(n_sub):
        if k + 1 < n_sub:
            a_stage(k + 1)
        if k >= 1:
            y_stage(k - 1)
        for lt in range(n_lt):
            for ii in range(groups):
                dep = w_block(k, lt, ii, dep)
    y_stage(n_sub - 1)

    @pl.when(c_idx == pl.num_programs(1) - 1)
    def _fin():
        xo_ref[...] = x_ref[...] + g2_ref[...] * yt_scr[...].T


def _peer(h2t, x, mod3, wqt, kbt, u, vt, row_of_tile, tm, te, ts):
    nt = x.shape[0]
    n_lt = tm // LANES
    n_rt = PEER_HEADS * PEER_KEYS
    n_sub = te // ts
    u_specs = [pl.BlockSpec((ts, D_MODEL), lambda i, c, k=k: (c * n_sub + k, 0)) for k in range(n_sub)]
    vt_specs = [pl.BlockSpec((None, D_MODEL, ts), lambda i, c, k=k: (c * n_sub + k, 0, 0)) for k in range(n_sub)]
    return pl.pallas_call(
        functools.partial(_peer_kernel, te=te, ts=ts),
        grid=(nt // tm, PEER_EXPERTS // te),
        in_specs=[
            pl.BlockSpec((D_MODEL, tm), lambda i, c: (0, i), pipeline_mode=pl.Buffered(1)),
            pl.BlockSpec((tm, D_MODEL), lambda i, c: (i, 0), pipeline_mode=pl.Buffered(1)),
            pl.BlockSpec((None, 1, D_MODEL), lambda i, c: (row_of_tile(i), 0, 5)),
            pl.BlockSpec((D_MODEL, D_MODEL), lambda i, c: (0, 0), pipeline_mode=pl.Buffered(1)),
            pl.BlockSpec((2 * n_rt, D_MODEL), lambda i, c: (0, 0), pipeline_mode=pl.Buffered(1)),
        ] + u_specs + vt_specs,
        out_specs=pl.BlockSpec((tm, D_MODEL), lambda i, c: (i, 0), pipeline_mode=pl.Buffered(1)),
        out_shape=jax.ShapeDtypeStruct((nt, D_MODEL), F32),
        scratch_shapes=[
            pltpu.VMEM((n_lt, n_rt, LANES), F32),
            pltpu.VMEM((n_lt, n_rt, LANES), F32),
            pltpu.VMEM((n_lt, n_rt, LANES), F32),
            pltpu.VMEM((n_lt, n_rt, LANES), F32),
            pltpu.VMEM((n_lt, 2 * n_rt, LANES), F32),
            pltpu.VMEM((4 * PEER_TOPK, LANES), F32),
            pltpu.VMEM((2, ts, tm), F32),
            pltpu.VMEM((2, ts, tm), BF16),
            pltpu.VMEM((D_MODEL, tm), F32),
        ],
        compiler_params=_cparams(("parallel", "arbitrary")),
        name="peer",
    )(h2t, x, mod3, wqt, kbt, *([u] * n_sub), *([vt] * n_sub))


def _final_kernel(x_ref, g_ref, o_ref):
    x = x_ref[...]
    o_ref[...] = x * lax.rsqrt(jnp.mean(x * x, axis=-1, keepdims=True) + EPS) * g_ref[...]


def _final_norm(x, gain, tm):
    nt = x.shape[0]
    return pl.pallas_call(
        _final_kernel,
        grid=(nt // tm,),
        in_specs=[pl.BlockSpec((tm, D_MODEL), lambda i: (i, 0)), pl.BlockSpec((1, D_MODEL), lambda i: (0, 0))],
        out_specs=pl.BlockSpec((tm, D_MODEL), lambda i: (i, 0)),
        out_shape=jax.ShapeDtypeStruct((nt, D_MODEL), F32),
        compiler_params=_cparams(("parallel",)),
        name="final_norm",
    )(x, gain)


def _block_diag(blocks):
    n, r, c = blocks.shape
    eye = jnp.eye(n, dtype=blocks.dtype)
    return jnp.einsum('nrc,nm->nrmc', blocks, eye).reshape(n * r, n * c)


def _state_to_bd_t(s):
    b, two, h, k, v = s.shape
    eye = jnp.eye(h, dtype=s.dtype)
    return jnp.einsum('bdhkv,hg->bdhvgk', s, eye).reshape(b, two, h * v, h * k)


def _bd_t_to_state(st, h, k, v):
    b = st.shape[0]
    return jnp.einsum('bdhvhk->bdhkv', st.reshape(b, 2, h, v, h, k))


def _rope_tables(t_len):
    rows = t_len // GRID_W
    row = jnp.repeat(jnp.arange(rows), GRID_W).astype(F32)
    col = jnp.tile(jnp.arange(GRID_W), rows).astype(F32)
    n_freq = RET_DK // 4
    inv_freq = ROPE_BASE ** (-jnp.arange(n_freq, dtype=F32) / n_freq)
    ang = jnp.concatenate([row[:, None] * inv_freq, col[:, None] * inv_freq], axis=-1)
    cos, sin = jnp.cos(ang), jnp.sin(ang)
    cos_h = jnp.concatenate([cos, cos], axis=-1)
    sin_h = jnp.concatenate([-sin, sin], axis=-1)
    return jnp.tile(cos_h, (1, RET_HEADS)), jnp.tile(sin_h, (1, RET_HEADS))


def _swap_matrix():
    lane = jnp.arange(RET_HEADS * RET_DK)
    half = RET_DK // 2
    src = jnp.where(lane % RET_DK < half, lane + half, lane - half)
    return (lane[:, None] == src[None, :]).astype(BF16)


def kernel(x_prompt, x_sample, c, state_gla, state_ret, state_s5, state_lru, c_ctx, w_mod, b_mod, norm_mix, norm_ffn, norm_final, w_in, gla_w_decay, gla_b_decay, ret_decay_logit, s5_a_re, s5_a_im, s5_log_dt, s5_b_re, s5_b_im, s5_c_re, s5_c_im, s5_d, s5_w_glu, lru_conv_w, lru_conv_b, lru_w_a, lru_b_a, lru_w_x, lru_b_x, lru_lambda, w_branch, w_out, peer_w_q, peer_keys, peer_u, peer_v):
    n_bp, t_p, _ = x_prompt.shape
    n_bs, t_s, _ = x_sample.shape
    depth = w_in.shape[0]
    ntp, nts = n_bp * t_p, n_bs * t_s
    tm = TOK_TM
    assert ntp % t_s == 0 and all(ntp % t == 0 and t_s % t == 0 for t in (TOK_TM, PRE_TM, PEER_TM))

    x = jnp.concatenate([x_prompt.reshape(ntp, D_MODEL), x_sample.reshape(nts, D_MODEL)], axis=0)

    n_rows = 8 * ((1 + n_bs + 7) // 8)
    cond = jnp.zeros((n_rows, D_MODEL), F32).at[0].set(c_ctx).at[1:1 + n_bs].set(c)
    mods = _adaln(cond, w_mod, b_mod)

    def make_row_of_tile(tile):
        def row_of_tile(i):
            return jnp.where(i < ntp // tile, 0, 1 + (i - ntp // tile) // (t_s // tile))
        return row_of_tile

    zpad = lambda n: jnp.zeros((depth, D_MODEL, n), F32)
    w_in_p = jnp.concatenate([w_in[:, :, 2336:6432], w_in[:, :, 0:800], zpad(224), w_in[:, :, 800:1568], zpad(256),
                              w_in[:, :, 1824:2336], w_in[:, :, 1568:1824], zpad(256)], axis=2).astype(BF16)
    assert w_in_p.shape[2] == Z_W
    w_in_p = w_in_p.reshape(depth, D_MODEL, Z_W // Z_TN, Z_TN).transpose(0, 2, 1, 3)

    hk, hv = GLA_HEADS * GLA_DK, GLA_HEADS * GLA_DV
    e_mat = (jnp.arange(hk)[:, None] // GLA_DK == jnp.arange(hv)[None, :] // GLA_DV).astype(BF16)
    ind = ((jnp.arange(hv)[:, None] // GLA_DV == jnp.arange(hv)[None, :] // GLA_DV).astype(F32) / GLA_DV).astype(BF16)
    swap = _swap_matrix()
    cos_s, sin_s = _rope_tables(t_s)
    cos_p, sin_p = jnp.ones((t_p, hk), F32), jnp.zeros((t_p, hk), F32)

    zeros_bd = jnp.zeros((n_bp, 2, hv, hk), F32)
    zeros_s5 = jnp.zeros((n_bp, 2, 2 * S5_CH), F32)
    zeros_lru = jnp.zeros((n_bp, 2, BR_W), F32)

    gla_l, ret_l, s5_l, lru_l = [], [], [], []
    for l in range(depth):
        mod3 = mods[l].reshape(n_rows, 1, N_MOD * D_MODEL)
        zg, z = _premix(x, mod3, norm_mix[l].reshape(1, D_MODEL), w_in_p[l], make_row_of_tile(PRE_TM), PRE_TM)

        wd = jnp.zeros((128, 256), F32)
        wd = wd.at[0:GLA_RANK, 0:hk].set(gla_w_decay[l, 0]).at[GLA_RANK:2 * GLA_RANK, hk:].set(gla_w_decay[l, 1])
        bd = gla_b_decay[l].reshape(1, 2 * hk)
        lg = jax.nn.log_sigmoid(ret_decay_logit[l].astype(F32))
        lgl = jnp.repeat(lg, RET_DK, axis=1)
        bre = _block_diag(jnp.swapaxes(s5_b_re[l], 1, 2))
        bim = _block_diag(jnp.swapaxes(s5_b_im[l], 1, 2))
        bbd = jnp.concatenate([bre, bim], axis=1).astype(BF16)
        cre = _block_diag(jnp.swapaxes(s5_c_re[l], 1, 2)).astype(BF16)
        cim = _block_diag(jnp.swapaxes(s5_c_im[l], 1, 2)).astype(BF16)
        coef, pf, pb = _s5_disc(s5_a_re[l].reshape(2, S5_CH), s5_a_im[l].reshape(2, S5_CH),
                                jnp.repeat(s5_log_dt[l], S5_STATE, axis=1), F32_ROWS)
        dsk = s5_d[l].reshape(1, BR_W)
        wglu = s5_w_glu[l].astype(BF16)
        wg = jnp.concatenate([_block_diag(lru_w_a[l, 0]), _block_diag(lru_w_x[l, 0]),
                              _block_diag(lru_w_a[l, 1]), _block_diag(lru_w_x[l, 1])], axis=1).astype(BF16)
        bg = jnp.concatenate([lru_b_a[l, 0], lru_b_x[l, 0], lru_b_a[l, 1], lru_b_x[l, 1]]).reshape(1, 4 * BR_W)
        cw = lru_conv_w[l]
        cb = lru_conv_b[l].reshape(1, BR_W)
        lam = lru_lambda[l]

        s5_h0 = state_s5[:, l].reshape(n_bs, 2, 2 * S5_CH)

        outs = []
        for (row_off, n_b, t_len, sg, sr, ss, sl, cs, sn, rope) in (
                (0, n_bp, t_p, zeros_bd, zeros_bd, zeros_s5, zeros_lru, cos_p, sin_p, False),
                (ntp // t_s, n_bs, t_s, _state_to_bd_t(state_gla[:, l]), _state_to_bd_t(state_ret[:, l]),
                 s5_h0, state_lru[:, l], cos_s, sin_s, True)):
            m_a, f_gla = _gla(z, row_off, n_b, t_len, wd, bd, sg, e_mat, ind)
            m_b, f_ret = _ret(z, row_off, n_b, t_len, cs, sn, swap, lgl, lg, sr, ind, rope)
            m_c, f_s5 = _s5(z, row_off, n_b, t_len, bbd, cre, cim, coef, pf, pb, dsk, wglu, ss)
            m_d, f_lru = _lru(z, row_off, n_b, t_len, cw, cb, wg, bg, lam, sl)
            outs.append(((m_a, m_b, m_c, m_d), (f_gla, f_ret, f_s5, f_lru)))

        ms = [jnp.concatenate([outs[0][0][n], outs[1][0][n]], axis=0) for n in range(4)]
        f_gla, f_ret, f_s5, f_lru = outs[0][1]
        gla_l.append(_bd_t_to_state(f_gla, GLA_HEADS, GLA_DK, GLA_DV))
        ret_l.append(_bd_t_to_state(f_ret, RET_HEADS, RET_DK, RET_DV))
        s5_l.append(f_s5.reshape(n_bp, 2, 2, S5_GROUPS, S5_STATE))
        lru_l.append(f_lru)

        x, h2t = _merge(ms, zg, x, mod3, norm_ffn[l].reshape(1, D_MODEL), w_branch[l].astype(BF16),
                        w_out[l].astype(BF16), make_row_of_tile(tm), tm)

        wqt = peer_w_q[l].T.astype(BF16)
        vt_l = peer_v[l].reshape(PEER_EXPERTS // PEER_TS, PEER_TS, D_MODEL).transpose(0, 2, 1).astype(BF16)
        kbt = _block_diag(peer_keys[l].reshape(2 * PEER_HEADS, PEER_KEYS, PEER_QDIM // 2)).astype(BF16)
        x = _peer(h2t, x, mod3, wqt, kbt, peer_u[l].astype(BF16), vt_l,
                  make_row_of_tile(PEER_TM), PEER_TM, PEER_TE, PEER_TS)

    y = _final_norm(x, norm_final.reshape(1, D_MODEL), tm)
    y_p = y[:ntp].reshape(n_bp, t_p, D_MODEL)
    y_s = y[ntp:].reshape(n_bs, t_s, D_MODEL)
    return (y_p, y_s, jnp.stack(gla_l, axis=1), jnp.stack(ret_l, axis=1),
            jnp.stack(s5_l, axis=1), jnp.stack(lru_l, axis=1))
```

```python
import functools
import math

import jax
import jax.numpy as jnp
from jax import lax
from jax.experimental import pallas as pl
from jax.experimental.pallas import tpu as pltpu

F32 = jnp.float32
BF16 = jnp.bfloat16
HI = lax.Precision.HIGHEST

D_MODEL = 1024
N_MOD = 6
EPS = 1e-6
BR_W = 256
GLA_HEADS, GLA_DK, GLA_DV, GLA_RANK, GLA_TAU, GLA_CHUNK = 4, 32, 64, 16, 16.0, 32
RET_HEADS, RET_DK, RET_DV, RET_CHUNK = 4, 32, 64, 64
ROPE_BASE = 10000.0
GRID_W = 64
S5_GROUP, S5_GROUPS, S5_STATE, S5_RE_MAX = 16, 16, 64, -1e-4
S5_CH = S5_GROUPS * S5_STATE
LRU_BLOCKS, LRU_BW, LRU_C = 4, 64, 8.0
PEER_HEADS, PEER_KEYS, PEER_TOPK, PEER_QDIM = 8, 128, 16, 128
PEER_EXPERTS = PEER_KEYS * PEER_KEYS

LANES = 128
VMEM_LIMIT = 58 * 1024 * 1024

Z_TN = 1024
Z_GATE_W = 4 * D_MODEL
Z_GLA, Z_RET, Z_LRU, Z_S5 = 0, 1024, 2048, 2560
Z_MIX_W = 3072
Z_W = Z_GATE_W + Z_MIX_W

SCAN_TB = 128
TOK_TM = 256
PRE_TM = 512
PEER_TM, PEER_TE, PEER_TS = 512, 2048, 512
NEG_INF = float("-inf")


def _cparams(sem):
    return pltpu.CompilerParams(dimension_semantics=sem, vmem_limit_bytes=VMEM_LIMIT)


def _nt(a, b):
    return lax.dot_general(a, b, (((1,), (1,)), ((), ())), preferred_element_type=F32)


def _tn(a, b):
    return lax.dot_general(a, b, (((0,), (0,)), ((), ())), preferred_element_type=F32)


def _mm(a, b):
    return jnp.dot(a, b, preferred_element_type=F32)


def _mm_hi(a, b):
    return jnp.dot(a, b, preferred_element_type=F32, precision=HI)


def _split_bf16(x):
    hi = x.astype(BF16)
    return hi, (x - hi.astype(F32)).astype(BF16)


def _mm_exact_rhs(a, b_exact):
    hi, lo = _split_bf16(a)
    return _mm(hi, b_exact) + _mm(lo, b_exact)


def _mm_exact_lhs(a_exact, b):
    hi, lo = _split_bf16(b)
    return _mm(a_exact, hi) + _mm(a_exact, lo)


def _sigmoid(x):
    return jax.nn.sigmoid(x)


def _silu(x):
    return x * jax.nn.sigmoid(x)


def _gelu(x):
    return jax.nn.gelu(x)


GELU_C0 = math.sqrt(2.0 / math.pi)
GELU_C1 = GELU_C0 * 0.044715
BF16_ROWS = 16
F32_ROWS = 8


def _gelu_tanh(x):
    k = -2.0 / math.log(2.0)
    return x / (1.0 + jnp.exp2(x * (k * GELU_C0 + (k * GELU_C1) * (x * x))))


def _log_sigmoid(x):
    return jnp.minimum(x, 0.0) - jnp.log(1.0 + jnp.exp(-jnp.abs(x)))


def _softplus(x):
    return jnp.maximum(x, 0.0) + jnp.log(1.0 + jnp.exp(-jnp.abs(x)))


def _rms_mod(x, gain, sc, sh):
    ms = jnp.mean(x * x, axis=-1, keepdims=True)
    return x * lax.rsqrt(ms + EPS) * gain * (1.0 + sc) + sh


def _adaln_kernel(c_ref, w_ref, b_ref, o_ref):
    o_ref[...] = _mm_hi(_silu(c_ref[...]), w_ref[...]) + b_ref[...]


def _adaln(cond, w_mod, b_mod):
    n_l = w_mod.shape[0]
    rows = cond.shape[0]
    tn = 1536
    return pl.pallas_call(
        _adaln_kernel,
        grid=(n_l, N_MOD * D_MODEL // tn),
        in_specs=[
            pl.BlockSpec((rows, D_MODEL), lambda l, j: (0, 0)),
            pl.BlockSpec((None, D_MODEL, tn), lambda l, j: (l, 0, j)),
            pl.BlockSpec((None, 1, tn), lambda l, j: (l, 0, j)),
        ],
        out_specs=pl.BlockSpec((None, rows, tn), lambda l, j: (l, 0, j)),
        out_shape=jax.ShapeDtypeStruct((n_l, rows, N_MOD * D_MODEL), F32),
        compiler_params=_cparams(("parallel", "parallel")),
        name="adaln",
    )(cond, w_mod, b_mod.reshape(n_l, 1, N_MOD * D_MODEL))


def _premix_kernel(x_ref, sh_ref, sc_ref, g_ref, w_ref, zg_ref, zm_ref, h_scr):
    j = pl.program_id(1)

    @pl.when(j == 0)
    def _():
        h_scr[...] = _rms_mod(x_ref[...], g_ref[...], sc_ref[...], sh_ref[...]).astype(BF16)

    z = _mm(h_scr[...], w_ref[j])

    @pl.when(j < Z_GATE_W // Z_TN)
    def _():
        zg_ref[...] = z.astype(BF16)

    @pl.when(j >= Z_GATE_W // Z_TN)
    def _():
        zm_ref[...] = z


def _premix(x, mod3, gain, w, row_of_tile, tm):
    nt = x.shape[0]
    n_gate = Z_GATE_W // Z_TN
    return pl.pallas_call(
        _premix_kernel,
        grid=(nt // tm, Z_W // Z_TN),
        in_specs=[
            pl.BlockSpec((tm, D_MODEL), lambda i, j: (i, 0)),
            pl.BlockSpec((None, 1, D_MODEL), lambda i, j: (row_of_tile(i), 0, 0)),
            pl.BlockSpec((None, 1, D_MODEL), lambda i, j: (row_of_tile(i), 0, 1)),
            pl.BlockSpec((1, D_MODEL), lambda i, j: (0, 0)),
            pl.BlockSpec((Z_W // Z_TN, D_MODEL, Z_TN), lambda i, j: (0, 0, 0), pipeline_mode=pl.Buffered(1)),
        ],
        out_specs=[
            pl.BlockSpec((tm, Z_TN), lambda i, j: (i, jnp.minimum(j, n_gate - 1))),
            pl.BlockSpec((tm, Z_TN), lambda i, j: (i, jnp.maximum(j - n_gate, 0))),
        ],
        out_shape=[
            jax.ShapeDtypeStruct((nt, Z_GATE_W), BF16),
            jax.ShapeDtypeStruct((nt, Z_MIX_W), F32),
        ],
        scratch_shapes=[pltpu.VMEM((tm, D_MODEL), BF16)],
        compiler_params=_cparams(("parallel", "arbitrary")),
        name="premix",
    )(x, mod3, mod3, gain, w)


def _gla_kernel(z_ref, wd_ref, bd_ref, s0_ref, e_ref, ind_ref, m_ref, sfin_ref,
                la_scr, of_scr, st_scr, p_scr, cum_scr, k_scr, v_scr):
    t_len = z_ref.shape[0]
    c = GLA_CHUNK
    n_chunks = t_len // c
    hk = GLA_HEADS * GLA_DK
    hv = GLA_HEADS * GLA_DV
    scale = GLA_DK ** -0.5

    pre = _mm_hi(z_ref[:, 768:896], wd_ref[...]) + bd_ref[...]
    la_scr[...] = _log_sigmoid(pre) * (1.0 / GLA_TAU)

    ri = lax.broadcasted_iota(jnp.int32, (c, c), 0)
    ci = lax.broadcasted_iota(jnp.int32, (c, c), 1)
    tri_lo = (ri >= ci).astype(BF16)
    tri_up = (ri <= ci).astype(BF16)
    row = lax.broadcasted_iota(jnp.int32, (c, hk), 0)
    bd_mask = (lax.broadcasted_iota(jnp.int32, (hv, hk), 0) // GLA_DV
               == lax.broadcasted_iota(jnp.int32, (hv, hk), 1) // GLA_DK).astype(F32)

    def chunk(base, la, tri, reverse, slot):
        cum = _mm_exact_lhs(tri, la)
        q = z_ref[pl.ds(base, c), 0:128] * scale
        k = z_ref[pl.ds(base, c), 128:256]
        v = z_ref[pl.ds(base, c), 256:512]
        edge = cum[0:1, :] if reverse else cum[c - 1:c, :]
        st = st_scr[slot]
        o = _nt((q * jnp.exp(cum)).astype(BF16), st.astype(BF16))
        ke = k * jnp.exp(edge - cum)
        cum_scr[slot] = cum
        k_scr[slot] = k
        v_scr[slot] = v
        for j in range(c):
            d = jnp.minimum(cum - cum_scr[slot, j:j + 1, :], 0.0)
            p = q * k_scr[slot, j:j + 1, :] * jnp.exp(d)
            keep = (row <= j) if reverse else (row >= j)
            p_scr[slot, j * c:(j + 1) * c, :] = jnp.where(keep, p, 0.0).astype(BF16)
        pe = _mm(p_scr[slot], e_ref[...])
        for j in range(c):
            o = o + pe[j * c:(j + 1) * c, :] * v_scr[slot, j:j + 1, :]
        st_scr[slot] = st * jnp.exp(edge) + bd_mask * _tn(v.astype(BF16), ke.astype(BF16))
        return o

    st_scr[0] = s0_ref[0]
    st_scr[1] = s0_ref[1]

    def sweep(i, carry):
        bf = pl.multiple_of(i * c, c)
        bb = pl.multiple_of((n_chunks - 1 - i) * c, c)
        of_scr[pl.ds(bf, c), :] = chunk(bf, la_scr[pl.ds(bf, c), 0:128], tri_lo, False, 0)
        m_ref[pl.ds(bb, c), :] = chunk(bb, la_scr[pl.ds(bb, c), 128:256], tri_up, True, 1)
        return carry

    lax.fori_loop(0, n_chunks, sweep, 0)
    sfin_ref[0] = st_scr[0]
    sfin_ref[1] = st_scr[1]

    fb = 8 * c

    def finish(i, carry):
        base = pl.multiple_of(i * fb, fb)
        o = of_scr[pl.ds(base, fb), :] + m_ref[pl.ds(base, fb), :]
        ms = _mm_exact_rhs(o * o, ind_ref[...])
        g = z_ref[pl.ds(base, fb), 512:768]
        m_ref[pl.ds(base, fb), :] = o * lax.rsqrt(ms + EPS) * _silu(g)
        return carry

    lax.fori_loop(0, t_len // fb, finish, 0)


def _gla(z, row_off, n_b, t_len, wd, bd, s0t, e_mat, ind):
    hk, hv = GLA_HEADS * GLA_DK, GLA_HEADS * GLA_DV
    c = GLA_CHUNK
    return pl.pallas_call(
        _gla_kernel,
        grid=(n_b,),
        in_specs=[
            pl.BlockSpec((t_len, 1024), lambda b: (row_off + b, Z_GLA // 1024)),
            pl.BlockSpec((128, 256), lambda b: (0, 0)),
            pl.BlockSpec((1, 256), lambda b: (0, 0)),
            pl.BlockSpec((None, 2, hv, hk), lambda b: (b, 0, 0, 0)),
            pl.BlockSpec((hk, hv), lambda b: (0, 0)),
            pl.BlockSpec((hv, hv), lambda b: (0, 0)),
        ],
        out_specs=[
            pl.BlockSpec((t_len, BR_W), lambda b: (b, 0)),
            pl.BlockSpec((None, 2, hv, hk), lambda b: (b, 0, 0, 0)),
        ],
        out_shape=[
            jax.ShapeDtypeStruct((n_b * t_len, BR_W), F32),
            jax.ShapeDtypeStruct((n_b, 2, hv, hk), F32),
        ],
        scratch_shapes=[
            pltpu.VMEM((t_len, 256), F32),
            pltpu.VMEM((t_len, hv), F32),
            pltpu.VMEM((2, hv, hk), F32),
            pltpu.VMEM((2, c * c, hk), BF16),
            pltpu.VMEM((2, c, hk), F32),
            pltpu.VMEM((2, c, hk), F32),
            pltpu.VMEM((2, c, hv), F32),
        ],
        compiler_params=_cparams(("parallel",)),
        name="gla",
    )(z, wd, bd, s0t, e_mat, ind)


def _ret_kernel(z_ref, cos_ref, sin_ref, swap_ref, lgl_ref, lgs_ref, s0_ref, ind_ref, m_ref, sfin_ref,
                qk_scr, of_scr, st_scr, *, rope):
    t_len = z_ref.shape[0]
    c = RET_CHUNK
    n_chunks = t_len // c
    hk = RET_HEADS * RET_DK
    hv = RET_HEADS * RET_DV
    scale = RET_DK ** -0.5

    lgf = lgl_ref[0:1, :]
    lgb = lgl_ref[1:2, :]
    pos = lax.broadcasted_iota(jnp.int32, (c, hk), 0).astype(F32)
    wq_f = jnp.exp(lgf * (pos + 1.0))
    wk_f = jnp.exp(lgf * (c - 1.0 - pos))
    wq_b = jnp.exp(lgb * (c - pos))
    wk_b = jnp.exp(lgb * pos)
    dec_f = jnp.exp(lgf * float(c))
    dec_b = jnp.exp(lgb * float(c))

    ii = lax.broadcasted_iota(jnp.int32, (c, c), 0)
    jj = lax.broadcasted_iota(jnp.int32, (c, c), 1)
    rel = (ii - jj).astype(F32)
    dms = []
    for h in range(RET_HEADS):
        d_f = jnp.where(ii >= jj, jnp.exp(lgs_ref[0, h] * jnp.maximum(rel, 0.0)), 0.0)
        d_b = jnp.where(jj >= ii, jnp.exp(lgs_ref[1, h] * jnp.maximum(-rel, 0.0)), 0.0)
        dms.append(d_f + d_b)
    dmat = jnp.concatenate(dms, axis=1)

    ek_mask = (lax.broadcasted_iota(jnp.int32, (RET_HEADS * c, hk), 0) // c
               == lax.broadcasted_iota(jnp.int32, (RET_HEADS * c, hk), 1) // RET_DK).astype(F32)
    ev_mask = (lax.broadcasted_iota(jnp.int32, (RET_HEADS * c, hv), 0) // c
               == lax.broadcasted_iota(jnp.int32, (RET_HEADS * c, hv), 1) // RET_DV).astype(F32)
    bd_mask = (lax.broadcasted_iota(jnp.int32, (hv, hk), 0) // RET_DV
               == lax.broadcasted_iota(jnp.int32, (hv, hk), 1) // RET_DK).astype(F32)

    st_scr[...] = s0_ref[0]

    def fwd(i, carry):
        base = pl.multiple_of(i * c, c)
        q = z_ref[pl.ds(base, c), 0:128] * scale
        k = z_ref[pl.ds(base, c), 128:256]
        v = z_ref[pl.ds(base, c), 256:512]
        if rope:
            cs = cos_ref[pl.ds(base, c), :]
            sn = sin_ref[pl.ds(base, c), :]
            q = q * cs + _mm_exact_rhs(q, swap_ref[...]) * sn
            k = k * cs + _mm_exact_rhs(k, swap_ref[...]) * sn
        qk_scr[pl.ds(base, c), 0:128] = q
        qk_scr[pl.ds(base, c), 128:256] = k
        kexp = (jnp.concatenate([k] * RET_HEADS, axis=0) * ek_mask).astype(BF16)
        vexp = (jnp.concatenate([v] * RET_HEADS, axis=0) * ev_mask).astype(BF16)
        sc = _nt(q.astype(BF16), kexp) * dmat
        o = _mm(sc.astype(BF16), vexp)
        st = st_scr[...]
        o = o + _nt((q * wq_f).astype(BF16), st.astype(BF16))
        of_scr[pl.ds(base, c), :] = o
        st_scr[...] = st * dec_f + bd_mask * _tn(v.astype(BF16), (k * wk_f).astype(BF16))
        return carry

    lax.fori_loop(0, n_chunks, fwd, 0)
    sfin_ref[0] = st_scr[...]
    st_scr[...] = s0_ref[1]

    def bwd(i, carry):
        base = pl.multiple_of((n_chunks - 1 - i) * c, c)
        q = qk_scr[pl.ds(base, c), 0:128]
        k = qk_scr[pl.ds(base, c), 128:256]
        v = z_ref[pl.ds(base, c), 256:512]
        st = st_scr[...]
        o = of_scr[pl.ds(base, c), :] + _nt((q * wq_b).astype(BF16), st.astype(BF16))
        st_scr[...] = st * dec_b + bd_mask * _tn(v.astype(BF16), (k * wk_b).astype(BF16))
        ms = _mm_exact_rhs(o * o, ind_ref[...])
        g = z_ref[pl.ds(base, c), 512:768]
        m_ref[pl.ds(base, c), :] = o * lax.rsqrt(ms + EPS) * _silu(g)
        return carry

    lax.fori_loop(0, n_chunks, bwd, 0)
    sfin_ref[1] = st_scr[...]


def _ret(z, row_off, n_b, t_len, cos_t, sin_t, swap, lgl, lgs, s0t, ind, rope):
    hk, hv = RET_HEADS * RET_DK, RET_HEADS * RET_DV
    return pl.pallas_call(
        functools.partial(_ret_kernel, rope=rope),
        grid=(n_b,),
        in_specs=[
            pl.BlockSpec((t_len, 1024), lambda b: (row_off + b, Z_RET // 1024)),
            pl.BlockSpec((t_len, hk), lambda b: (0, 0)),
            pl.BlockSpec((t_len, hk), lambda b: (0, 0)),
            pl.BlockSpec((hk, hk), lambda b: (0, 0)),
            pl.BlockSpec((2, hk), lambda b: (0, 0)),
            pl.BlockSpec(memory_space=pltpu.SMEM),
            pl.BlockSpec((None, 2, hv, hk), lambda b: (b, 0, 0, 0)),
            pl.BlockSpec((hv, hv), lambda b: (0, 0)),
        ],
        out_specs=[
            pl.BlockSpec((t_len, BR_W), lambda b: (b, 0)),
            pl.BlockSpec((None, 2, hv, hk), lambda b: (b, 0, 0, 0)),
        ],
        out_shape=[
            jax.ShapeDtypeStruct((n_b * t_len, BR_W), F32),
            jax.ShapeDtypeStruct((n_b, 2, hv, hk), F32),
        ],
        scratch_shapes=[
            pltpu.VMEM((t_len, 2 * hk), F32),
            pltpu.VMEM((t_len, hv), F32),
            pltpu.VMEM((hv, hk), F32),
        ],
        compiler_params=_cparams(("parallel",)),
        name="ret",
    )(z, cos_t, sin_t, swap, lgl, lgs, s0t, ind)


def _s5_disc_kernel(are_ref, aim_ref, ldt_ref, coef_ref, pf_ref, pb_ref):
    tb = pf_ref.shape[0]
    re = jnp.minimum(are_ref[...], S5_RE_MAX)
    im = aim_ref[...]
    dt = jnp.exp(ldt_ref[...])
    er = jnp.exp(re * dt)
    lbr = er * jnp.cos(im * dt)
    lbi = er * jnp.sin(im * dt)
    den = re * re + im * im
    nr = lbr - 1.0
    coef_ref[:, 0:S5_CH] = (nr * re + lbi * im) / den
    coef_ref[:, S5_CH:] = (lbi * re - nr * im) / den
    t = lax.broadcasted_iota(jnp.int32, (tb, S5_CH), 0).astype(F32)
    nf = t + 1.0
    nb = float(tb) - t
    mf = jnp.exp(nf * (re[0:1] * dt[0:1]))
    pf_ref[:, 0:S5_CH] = mf * jnp.cos(nf * (im[0:1] * dt[0:1]))
    pf_ref[:, S5_CH:] = mf * jnp.sin(nf * (im[0:1] * dt[0:1]))
    mb = jnp.exp(nb * (re[1:2] * dt[1:2]))
    pb_ref[:, 0:S5_CH] = mb * jnp.cos(nb * (im[1:2] * dt[1:2]))
    pb_ref[:, S5_CH:] = mb * jnp.sin(nb * (im[1:2] * dt[1:2]))


def _s5_disc(are, aim, ldt, tb):
    return pl.pallas_call(
        _s5_disc_kernel,
        out_shape=[
            jax.ShapeDtypeStruct((2, 2 * S5_CH), F32),
            jax.ShapeDtypeStruct((tb, 2 * S5_CH), F32),
            jax.ShapeDtypeStruct((tb, 2 * S5_CH), F32),
        ],
        compiler_params=pltpu.CompilerParams(vmem_limit_bytes=VMEM_LIMIT),
        name="s5_disc",
    )(are, aim, ldt)


def _s5_kernel(u_ref, bbd_ref, cre_ref, cim_ref, coef_ref, pf_ref, pb_ref, dsk_ref, wglu_ref, h0_ref,
               m_ref, hfin_ref, y_scr, bu_scr, hr_scr, hi_scr, car_scr):
    t_len = u_ref.shape[0]
    tb = SCAN_TB
    sub = pf_ref.shape[0]
    n_blocks = t_len // tb
    row_in = lax.broadcasted_iota(jnp.int32, (tb, LANES), 0) % sub
    steps = [1 << s for s in range(int(math.log2(sub)))]

    def block(base, dr_i, reverse):
        p_ref = pb_ref if reverse else pf_ref
        u = u_ref[pl.ds(base, tb), :]
        bu_scr[...] = _mm(u.astype(BF16), bbd_ref[...])
        for g in range(S5_CH // LANES):
            lo, hi = g * LANES, (g + 1) * LANES
            br = bu_scr[:, lo:hi]
            bi = bu_scr[:, S5_CH + lo:S5_CH + hi]
            cr = coef_ref[dr_i:dr_i + 1, lo:hi]
            ci = coef_ref[dr_i:dr_i + 1, S5_CH + lo:S5_CH + hi]
            hr = cr * br - ci * bi
            hi_ = cr * bi + ci * br
            for d in steps:
                if reverse:
                    pr = p_ref[sub - d:sub - d + 1, lo:hi]
                    pi = p_ref[sub - d:sub - d + 1, S5_CH + lo:S5_CH + hi]
                    keep = row_in < sub - d
                    sr = jnp.where(keep, pltpu.roll(hr, tb - d, 0), 0.0)
                    si = jnp.where(keep, pltpu.roll(hi_, tb - d, 0), 0.0)
                else:
                    pr = p_ref[d - 1:d, lo:hi]
                    pi = p_ref[d - 1:d, S5_CH + lo:S5_CH + hi]
                    keep = row_in >= d
                    sr = jnp.where(keep, pltpu.roll(hr, d, 0), 0.0)
                    si = jnp.where(keep, pltpu.roll(hi_, d, 0), 0.0)
                hr, hi_ = hr + pr * sr - pi * si, hi_ + pr * si + pi * sr
            car = car_scr[0:1, lo:hi]
            cai = car_scr[0:1, S5_CH + lo:S5_CH + hi]
            pwr = p_ref[:, lo:hi]
            pwi = p_ref[:, S5_CH + lo:S5_CH + hi]
            n_grp = tb // sub
            for v in (range(n_grp - 1, -1, -1) if reverse else range(n_grp)):
                gr = hr[v * sub:(v + 1) * sub, :] + pwr * car - pwi * cai
                gi = hi_[v * sub:(v + 1) * sub, :] + pwr * cai + pwi * car
                hr_scr[v * sub:(v + 1) * sub, lo:hi] = gr
                hi_scr[v * sub:(v + 1) * sub, lo:hi] = gi
                edge = 0 if reverse else sub - 1
                car, cai = gr[edge:edge + 1, :], gi[edge:edge + 1, :]
            car_scr[0:1, lo:hi] = car
            car_scr[0:1, S5_CH + lo:S5_CH + hi] = cai
        y = _mm(hr_scr[...].astype(BF16), cre_ref[...]) - _mm(hi_scr[...].astype(BF16), cim_ref[...])
        return u, y

    car_scr[0:1, :] = h0_ref[0:1, :]

    def fwd(i, carry):
        base = pl.multiple_of(i * tb, tb)
        _, y = block(base, 0, False)
        y_scr[pl.ds(base, tb), :] = y
        return carry

    lax.fori_loop(0, n_blocks, fwd, 0)
    hfin_ref[0:1, :] = car_scr[0:1, :]
    car_scr[0:1, :] = h0_ref[1:2, :]

    def bwd(i, carry):
        base = pl.multiple_of((n_blocks - 1 - i) * tb, tb)
        u, y = block(base, 1, True)
        y = _gelu(y_scr[pl.ds(base, tb), :] + y + dsk_ref[...] * u)
        gg = _mm(y.astype(BF16), wglu_ref[...])
        m_ref[pl.ds(base, tb), :] = gg[:, 0:BR_W] * _sigmoid(gg[:, BR_W:])
        return carry

    lax.fori_loop(0, n_blocks, bwd, 0)
    hfin_ref[1:2, :] = car_scr[0:1, :]


def _s5(z, row_off, n_b, t_len, bbd, cre, cim, coef, pf, pb, dsk, wglu, h0):
    tb = SCAN_TB
    sub = pf.shape[0]
    full = lambda shape: pl.BlockSpec(shape, lambda b: (0,) * len(shape))
    return pl.pallas_call(
        _s5_kernel,
        grid=(n_b,),
        in_specs=[
            pl.BlockSpec((t_len, BR_W), lambda b: (row_off + b, Z_S5 // BR_W)),
            full((BR_W, 2 * S5_CH)),
            full((S5_CH, BR_W)),
            full((S5_CH, BR_W)),
            full((2, 2 * S5_CH)),
            full((sub, 2 * S5_CH)),
            full((sub, 2 * S5_CH)),
            full((1, BR_W)),
            full((BR_W, 2 * BR_W)),
            pl.BlockSpec((None, 2, 2 * S5_CH), lambda b: (b, 0, 0)),
        ],
        out_specs=[
            pl.BlockSpec((t_len, BR_W), lambda b: (b, 0)),
            pl.BlockSpec((None, 2, 2 * S5_CH), lambda b: (b, 0, 0)),
        ],
        out_shape=[
            jax.ShapeDtypeStruct((n_b * t_len, BR_W), F32),
            jax.ShapeDtypeStruct((n_b, 2, 2 * S5_CH), F32),
        ],
        scratch_shapes=[
            pltpu.VMEM((t_len, BR_W), F32),
            pltpu.VMEM((tb, 2 * S5_CH), F32),
            pltpu.VMEM((tb, S5_CH), F32),
            pltpu.VMEM((tb, S5_CH), F32),
            pltpu.VMEM((8, 2 * S5_CH), F32),
        ],
        compiler_params=_cparams(("parallel",)),
        name="s5",
    )(z, bbd, cre, cim, coef, pf, pb, dsk, wglu, h0)


def _lru_kernel(z_ref, cw_ref, cb_ref, wg_ref, bg_ref, lam_ref, h0_ref, m_ref, hfin_ref,
                xc_scr, hf_scr, car_scr):
    t_len = z_ref.shape[0]
    tb = SCAN_TB
    n_blocks = t_len // tb
    steps = [1 << s for s in range(int(math.log2(tb)))]

    x = z_ref[:, 0:BR_W]
    trow = lax.broadcasted_iota(jnp.int32, (t_len, BR_W), 0)
    xm1 = jnp.where(trow >= 1, pltpu.roll(x, 1, 0), 0.0)
    xp1 = jnp.where(trow < t_len - 1, pltpu.roll(x, t_len - 1, 0), 0.0)
    xp2 = jnp.where(trow < t_len - 2, pltpu.roll(x, t_len - 2, 0), 0.0)
    xc_scr[...] = (cw_ref[0:1, :] * xm1 + cw_ref[1:2, :] * x + cw_ref[2:3, :] * xp1
                   + cw_ref[3:4, :] * xp2 + cb_ref[...])

    row = lax.broadcasted_iota(jnp.int32, (tb, BR_W), 0)
    sp = _softplus(-lam_ref[...])

    def block(base, dr_i, reverse):
        xc = xc_scr[pl.ds(base, tb), :]
        off = dr_i * 2 * BR_W
        gates = _mm(xc.astype(BF16), wg_ref[:, off:off + 2 * BR_W]) + bg_ref[:, off:off + 2 * BR_W]
        r = _sigmoid(gates[:, 0:BR_W])
        ig = _sigmoid(gates[:, BR_W:])
        log_a = -LRU_C * r * sp[dr_i:dr_i + 1, :]
        a = jnp.exp(log_a)
        th = jnp.tanh(log_a)
        b = jnp.sqrt(-2.0 * th / (1.0 - th)) * (ig * xc)
        for d in steps:
            if reverse:
                keep = row < tb - d
                a_s = jnp.where(keep, pltpu.roll(a, tb - d, 0), 1.0)
                b_s = jnp.where(keep, pltpu.roll(b, tb - d, 0), 0.0)
            else:
                keep = row >= d
                a_s = jnp.where(keep, pltpu.roll(a, d, 0), 1.0)
                b_s = jnp.where(keep, pltpu.roll(b, d, 0), 0.0)
            b = b + a * b_s
            a = a * a_s
        h = b + a * car_scr[dr_i:dr_i + 1, :]
        edge = 0 if reverse else tb - 1
        car_scr[dr_i:dr_i + 1, :] = h[edge:edge + 1, :]
        return h

    car_scr[0:2, :] = h0_ref[...]

    def fwd(i, carry):
        base = pl.multiple_of(i * tb, tb)
        hf_scr[pl.ds(base, tb), :] = block(base, 0, False)
        return carry

    lax.fori_loop(0, n_blocks, fwd, 0)

    def bwd(i, carry):
        base = pl.multiple_of((n_blocks - 1 - i) * tb, tb)
        h = block(base, 1, True) + hf_scr[pl.ds(base, tb), :]
        m_ref[pl.ds(base, tb), :] = h * _gelu(z_ref[pl.ds(base, tb), BR_W:2 * BR_W])
        return carry

    lax.fori_loop(0, n_blocks, bwd, 0)
    hfin_ref[...] = car_scr[0:2, :]


def _lru(z, row_off, n_b, t_len, cw, cb, wg, bg, lam, h0):
    full = lambda shape: pl.BlockSpec(shape, lambda b: (0,) * len(shape))
    return pl.pallas_call(
        _lru_kernel,
        grid=(n_b,),
        in_specs=[
            pl.BlockSpec((t_len, 2 * BR_W), lambda b: (row_off + b, Z_LRU // (2 * BR_W))),
            full((4, BR_W)),
            full((1, BR_W)),
            full((BR_W, 4 * BR_W)),
            full((1, 4 * BR_W)),
            full((2, BR_W)),
            pl.BlockSpec((None, 2, BR_W), lambda b: (b, 0, 0)),
        ],
        out_specs=[
            pl.BlockSpec((t_len, BR_W), lambda b: (b, 0)),
            pl.BlockSpec((None, 2, BR_W), lambda b: (b, 0, 0)),
        ],
        out_shape=[
            jax.ShapeDtypeStruct((n_b * t_len, BR_W), F32),
            jax.ShapeDtypeStruct((n_b, 2, BR_W), F32),
        ],
        scratch_shapes=[
            pltpu.VMEM((t_len, BR_W), F32),
            pltpu.VMEM((t_len, BR_W), F32),
            pltpu.VMEM((8, BR_W), F32),
        ],
        compiler_params=_cparams(("parallel",)),
        name="lru",
    )(z, cw, cb, wg, bg, lam, h0)


def _merge_kernel(ma_ref, mb_ref, mc_ref, md_ref, zg_ref, x_ref, g1_ref, sh2_ref, sc2_ref, gain_ref,
                  wb_ref, wo_ref, xo_ref, h2t_ref):
    acc = None
    for n, m_ref in enumerate((ma_ref, mb_ref, mc_ref, md_ref)):
        proj = _mm(m_ref[...].astype(BF16), wb_ref[n])
        term = _sigmoid(zg_ref[:, n * D_MODEL:(n + 1) * D_MODEL].astype(F32)) * proj
        acc = term if acc is None else acc + term
    xn = x_ref[...] + g1_ref[...] * _mm(acc.astype(BF16), wo_ref[...])
    xo_ref[...] = xn
    h2t_ref[...] = _rms_mod(xn, gain_ref[...], sc2_ref[...], sh2_ref[...]).T.astype(BF16)


def _merge(ms, z, x, mod3, gain, wb, wo, row_of_tile, tm):
    nt = x.shape[0]
    modspec = lambda k: pl.BlockSpec((None, 1, D_MODEL), lambda i: (row_of_tile(i), 0, k))
    return pl.pallas_call(
        _merge_kernel,
        grid=(nt // tm,),
        in_specs=[pl.BlockSpec((tm, BR_W), lambda i: (i, 0))] * 4 + [
            pl.BlockSpec((tm, 4 * D_MODEL), lambda i: (i, 0)),
            pl.BlockSpec((tm, D_MODEL), lambda i: (i, 0)),
            modspec(2), modspec(3), modspec(4),
            pl.BlockSpec((1, D_MODEL), lambda i: (0, 0)),
            pl.BlockSpec((4, BR_W, D_MODEL), lambda i: (0, 0, 0)),
            pl.BlockSpec((D_MODEL, D_MODEL), lambda i: (0, 0)),
        ],
        out_specs=[
            pl.BlockSpec((tm, D_MODEL), lambda i: (i, 0)),
            pl.BlockSpec((D_MODEL, tm), lambda i: (0, i)),
        ],
        out_shape=[
            jax.ShapeDtypeStruct((nt, D_MODEL), F32),
            jax.ShapeDtypeStruct((D_MODEL, nt), BF16),
        ],
        compiler_params=_cparams(("parallel",)),
        name="merge",
    )(*ms, z, x, mod3, mod3, mod3, gain, wb, wo)


def _oddeven_merge_sort_pairs(n):
    pairs = []
    p = 1
    while p < n:
        k = p
        while k >= 1:
            for j in range(k % p, n - k, 2 * k):
                for i in range(min(k, n - j - k)):
                    if (i + j) // (2 * p) == (i + j + k) // (2 * p):
                        pairs.append((i + j, i + j + k))
            k //= 2
        p *= 2
    return pairs


_SORT16 = _oddeven_merge_sort_pairs(PEER_KEYS // F32_ROWS)


def _peer_kernel(h2t_ref, x_ref, g2_ref, wqt_ref, kbt_ref, *rest, te, ts):
    n_sub = te // ts
    u_refs, vt_refs = rest[:n_sub], rest[n_sub:2 * n_sub]
    xo_ref, n_scr, r1_scr, a_scr, b_scr, sc_scr, v_scr, at_scr, wa_scr, yt_scr = rest[2 * n_sub:]
    tm = h2t_ref.shape[1]
    n_lt = tm // LANES
    ic = te // PEER_KEYS
    c_idx = pl.program_id(1)
    nk = PEER_KEYS

    @pl.when(c_idx == 0)
    def _route():
        qt = _mm(wqt_ref[...], h2t_ref[...])
        qtb = qt.astype(BF16)
        qd = PEER_QDIM // 2
        for hs in range(2 * PEER_HEADS):
            sct = _mm(kbt_ref[hs], qtb[hs * qd:(hs + 1) * qd, :])
            for lt in range(n_lt):
                sc_scr[lt, hs * nk:(hs + 1) * nk, :] = sct[:, lt * LANES:(lt + 1) * LANES]
        r8 = lax.broadcasted_iota(jnp.int32, (8, LANES), 0)

        def per_tile(lt, carry):
            def per_head(h, vo):
                o0 = pl.multiple_of(h * (2 * nk), nk)
                o1 = pl.multiple_of(h * (2 * nk) + nk, nk)
                cols = [sc_scr[lt, pl.ds(pl.multiple_of(o0 + F32_ROWS * v, F32_ROWS), F32_ROWS), :]
                        for v in range(nk // F32_ROWS)]
                for ca, cb in _SORT16:
                    cols[ca], cols[cb] = jnp.maximum(cols[ca], cols[cb]), jnp.minimum(cols[ca], cols[cb])
                for r in range(PEER_TOPK):
                    m = jnp.max(cols[0], axis=0, keepdims=True)
                    v_scr[vo + r:vo + r + 1, :] = m
                    hit = cols[0] == m
                    for kk in range(PEER_TOPK - 1 - r):
                        cols[kk] = jnp.where(hit, cols[kk + 1], cols[kk])
                rank1 = jnp.full((nk, LANES), float(PEER_TOPK), F32)
                xs = sc_scr[lt, pl.ds(o1, nk), :]
                for r in range(PEER_TOPK):
                    m = jnp.max(xs, axis=0, keepdims=True)
                    v_scr[vo + PEER_TOPK + r:vo + PEER_TOPK + r + 1, :] = m
                    hit = xs == m
                    rank1 = jnp.where(hit, float(r), rank1)
                    xs = jnp.where(hit, NEG_INF, xs)
                v0 = v_scr[vo:vo + PEER_TOPK, :]
                v1 = v_scr[vo + PEER_TOPK:vo + 2 * PEER_TOPK, :]
                pieces = [v0[0:1, :] + v1]
                for r0 in range(1, 8):
                    pieces.append(jnp.where(r8 < PEER_TOPK // (r0 + 1), v0[r0:r0 + 1, :] + v1[0:8, :], NEG_INF))
                pieces.append(v0[8:16, :] + v1[0:1, :])
                cand = jnp.concatenate(pieces, axis=0)
                top = v0[0:1, :] + v1[0:1, :]
                zsum = jnp.zeros_like(top)
                tau = top
                for r in range(PEER_TOPK):
                    tau = jnp.max(cand, axis=0, keepdims=True)
                    zsum = zsum + jnp.exp(tau - top)
                    cand = jnp.where(cand == tau, NEG_INF, cand)
                ho = pl.multiple_of(h * nk, nk)
                s0 = sc_scr[lt, pl.ds(o0, nk), :]
                s1 = sc_scr[lt, pl.ds(o1, nk), :]
                crank = jnp.zeros((PEER_TOPK, LANES), F32)
                for r1 in range(PEER_TOPK):
                    crank = crank + jnp.where(v0 + v1[r1:r1 + 1, :] >= tau, 1.0, 0.0)
                cnt = jnp.zeros((nk, LANES), F32)
                half = PEER_TOPK // 2
                for m in range(1, half + 1):
                    u_m = jnp.min(jnp.where(crank >= float(m), v0, jnp.inf), axis=0, keepdims=True)
                    cnt = jnp.where(s0 >= u_m, float(m), cnt)
                cnt = jnp.where(s0 >= v0[0:1, :], crank[0:1, :], cnt)
                n_scr[lt, pl.ds(ho, nk), :] = cnt
                r1_scr[lt, pl.ds(ho, nk), :] = rank1
                a_scr[lt, pl.ds(ho, nk), :] = jnp.exp(s0 - v0[0:1, :])
                b_scr[lt, pl.ds(ho, nk), :] = jnp.exp(s1 - v1[0:1, :]) / zsum

            def per_pair(hp, carry2):
                per_head(2 * hp, 0)
                per_head(2 * hp + 1, 2 * PEER_TOPK)
                return carry2

            return lax.fori_loop(0, PEER_HEADS // 2, per_pair, carry)

        lax.fori_loop(0, n_lt, per_tile, 0)
        yt_scr[...] = jnp.zeros_like(yt_scr)

    groups = ts // nk

    def a_stage(k):
        at_scr[k % 2] = _mm(u_refs[k][...], h2t_ref[...])

    def y_stage(k):
        yt_scr[...] += _mm(vt_refs[k][...], wa_scr[k % 2])

    def w_block(k, lt, ii, dep):
        pk = BF16_ROWS
        nv = nk // pk
        cols = slice(lt * LANES, (lt + 1) * LANES)
        acc = [None] * nv
        for h in range(PEER_HEADS):
            row = h * nk + c_idx * ic + k * groups + ii
            n_t = jnp.broadcast_to(n_scr[lt, pl.ds(row, 1), :] + dep, (pk, LANES))
            a_t = jnp.broadcast_to(a_scr[lt, pl.ds(row, 1), :] + dep, (pk, LANES))
            for jv in range(nv):
                r1 = r1_scr[lt, h * nk + jv * pk:h * nk + (jv + 1) * pk, :]
                b1 = b_scr[lt, h * nk + jv * pk:h * nk + (jv + 1) * pk, :]
                term = jnp.where(r1 < n_t, b1, 0.0) * a_t
                acc[jv] = term if acc[jv] is None else acc[jv] + term
        out = None
        for jv in range(nv):
            rows = slice(ii * nk + jv * pk, ii * nk + (jv + 1) * pk)
            out = acc[jv] * _gelu_tanh(at_scr[k % 2, rows, cols])
            wa_scr[k % 2, rows, cols] = out.astype(BF16)
        last = out[0:1, :]
        return jnp.where((last < 2.0) & (last > -2.0), last, 1.0) * 0.0

    dep = jnp.zeros((1, LANES), F32)
    a_stage(0)
    for k in range(n_sub):
        if k + 1 < n_sub:
            a_stage(k + 1)
        if k >= 1:
            y_stage(k - 1)
        for lt in range(n_lt):
            for ii in range(groups):
                dep = w_block(k, lt, ii, dep)
    y_stage(n_sub - 1)

    @pl.when(c_idx == pl.num_programs(1) - 1)
    def _fin():
        xo_ref[...] = x_ref[...] + g2_ref[...] * yt_scr[...].T


def _peer(h2t, x, mod3, wqt, kbt, u, vt, row_of_tile, tm, te, ts):
    nt = x.shape[0]
    n_lt = tm // LANES
    n_rt = PEER_HEADS * PEER_KEYS
    n_sub = te // ts
    u_specs = [pl.BlockSpec((ts, D_MODEL), lambda i, c, k=k: (c * n_sub + k, 0)) for k in range(n_sub)]
    vt_specs = [pl.BlockSpec((None, D_MODEL, ts), lambda i, c, k=k: (c * n_sub + k, 0, 0)) for k in range(n_sub)]
    return pl.pallas_call(
        functools.partial(_peer_kernel, te=te, ts=ts),
        grid=(nt // tm, PEER_EXPERTS // te),
        in_specs=[
            pl.BlockSpec((D_MODEL, tm), lambda i, c: (0, i), pipeline_mode=pl.Buffered(1)),
            pl.BlockSpec((tm, D_MODEL), lambda i, c: (i, 0), pipeline_mode=pl.Buffered(1)),
            pl.BlockSpec((None, 1, D_MODEL), lambda i, c: (row_of_tile(i), 0, 5)),
            pl.BlockSpec((D_MODEL, D_MODEL), lambda i, c: (0, 0), pipeline_mode=pl.Buffered(1)),
            pl.BlockSpec((2 * PEER_HEADS, PEER_KEYS, PEER_QDIM // 2), lambda i, c: (0, 0, 0),
                         pipeline_mode=pl.Buffered(1)),
        ] + u_specs + vt_specs,
        out_specs=pl.BlockSpec((tm, D_MODEL), lambda i, c: (i, 0), pipeline_mode=pl.Buffered(1)),
        out_shape=jax.ShapeDtypeStruct((nt, D_MODEL), F32),
        scratch_shapes=[
            pltpu.VMEM((n_lt, n_rt, LANES), F32),
            pltpu.VMEM((n_lt, n_rt, LANES), F32),
            pltpu.VMEM((n_lt, n_rt, LANES), F32),
            pltpu.VMEM((n_lt, n_rt, LANES), F32),
            pltpu.VMEM((n_lt, 2 * n_rt, LANES), F32),
            pltpu.VMEM((4 * PEER_TOPK, LANES), F32),
            pltpu.VMEM((2, ts, tm), F32),
            pltpu.VMEM((2, ts, tm), BF16),
            pltpu.VMEM((D_MODEL, tm), F32),
        ],
        compiler_params=_cparams(("parallel", "arbitrary")),
        name="peer",
    )(h2t, x, mod3, wqt, kbt, *([u] * n_sub), *([vt] * n_sub))


def _final_kernel(x_ref, g_ref, o_ref):
    x = x_ref[...]
    o_ref[...] = x * lax.rsqrt(jnp.mean(x * x, axis=-1, keepdims=True) + EPS) * g_ref[...]


def _final_norm(x, gain, tm):
    nt = x.shape[0]
    return pl.pallas_call(
        _final_kernel,
        grid=(nt // tm,),
        in_specs=[pl.BlockSpec((tm, D_MODEL), lambda i: (i, 0)), pl.BlockSpec((1, D_MODEL), lambda i: (0, 0))],
        out_specs=pl.BlockSpec((tm, D_MODEL), lambda i: (i, 0)),
        out_shape=jax.ShapeDtypeStruct((nt, D_MODEL), F32),
        compiler_params=_cparams(("parallel",)),
        name="final_norm",
    )(x, gain)


def _block_diag(blocks):
    n, r, c = blocks.shape
    eye = jnp.eye(n, dtype=blocks.dtype)
    return jnp.einsum('nrc,nm->nrmc', blocks, eye).reshape(n * r, n * c)


def _state_to_bd_t(s):
    b, two, h, k, v = s.shape
    eye = jnp.eye(h, dtype=s.dtype)
    return jnp.einsum('bdhkv,hg->bdhvgk', s, eye).reshape(b, two, h * v, h * k)


def _bd_t_to_state(st, h, k, v):
    b = st.shape[0]
    return jnp.einsum('bdhvhk->bdhkv', st.reshape(b, 2, h, v, h, k))


def _rope_tables(t_len):
    rows = t_len // GRID_W
    row = jnp.repeat(jnp.arange(rows), GRID_W).astype(F32)
    col = jnp.tile(jnp.arange(GRID_W), rows).astype(F32)
    n_freq = RET_DK // 4
    inv_freq = ROPE_BASE ** (-jnp.arange(n_freq, dtype=F32) / n_freq)
    ang = jnp.concatenate([row[:, None] * inv_freq, col[:, None] * inv_freq], axis=-1)
    cos, sin = jnp.cos(ang), jnp.sin(ang)
    cos_h = jnp.concatenate([cos, cos], axis=-1)
    sin_h = jnp.concatenate([-sin, sin], axis=-1)
    return jnp.tile(cos_h, (1, RET_HEADS)), jnp.tile(sin_h, (1, RET_HEADS))


def _swap_matrix():
    lane = jnp.arange(RET_HEADS * RET_DK)
    half = RET_DK // 2
    src = jnp.where(lane % RET_DK < half, lane + half, lane - half)
    return (lane[:, None] == src[None, :]).astype(BF16)


def kernel(x_prompt, x_sample, c, state_gla, state_ret, state_s5, state_lru, c_ctx, w_mod, b_mod, norm_mix, norm_ffn, norm_final, w_in, gla_w_decay, gla_b_decay, ret_decay_logit, s5_a_re, s5_a_im, s5_log_dt, s5_b_re, s5_b_im, s5_c_re, s5_c_im, s5_d, s5_w_glu, lru_conv_w, lru_conv_b, lru_w_a, lru_b_a, lru_w_x, lru_b_x, lru_lambda, w_branch, w_out, peer_w_q, peer_keys, peer_u, peer_v):
    n_bp, t_p, _ = x_prompt.shape
    n_bs, t_s, _ = x_sample.shape
    depth = w_in.shape[0]
    ntp, nts = n_bp * t_p, n_bs * t_s
    tm = TOK_TM
    assert ntp % t_s == 0 and all(ntp % t == 0 and t_s % t == 0 for t in (TOK_TM, PRE_TM, PEER_TM))

    x = jnp.concatenate([x_prompt.reshape(ntp, D_MODEL), x_sample.reshape(nts, D_MODEL)], axis=0)

    n_rows = 8 * ((1 + n_bs + 7) // 8)
    cond = jnp.zeros((n_rows, D_MODEL), F32).at[0].set(c_ctx).at[1:1 + n_bs].set(c)
    mods = _adaln(cond, w_mod, b_mod)

    def make_row_of_tile(tile):
        def row_of_tile(i):
            return jnp.where(i < ntp // tile, 0, 1 + (i - ntp // tile) // (t_s // tile))
        return row_of_tile

    zpad = lambda n: jnp.zeros((depth, D_MODEL, n), F32)
    w_in_p = jnp.concatenate([w_in[:, :, 2336:6432], w_in[:, :, 0:800], zpad(224), w_in[:, :, 800:1568], zpad(256),
                              w_in[:, :, 1824:2336], w_in[:, :, 1568:1824], zpad(256)], axis=2).astype(BF16)
    assert w_in_p.shape[2] == Z_W
    w_in_p = w_in_p.reshape(depth, D_MODEL, Z_W // Z_TN, Z_TN).transpose(0, 2, 1, 3)

    hk, hv = GLA_HEADS * GLA_DK, GLA_HEADS * GLA_DV
    e_mat = (jnp.arange(hk)[:, None] // GLA_DK == jnp.arange(hv)[None, :] // GLA_DV).astype(BF16)
    ind = ((jnp.arange(hv)[:, None] // GLA_DV == jnp.arange(hv)[None, :] // GLA_DV).astype(F32) / GLA_DV).astype(BF16)
    swap = _swap_matrix()
    cos_s, sin_s = _rope_tables(t_s)
    cos_p, sin_p = jnp.ones((t_p, hk), F32), jnp.zeros((t_p, hk), F32)

    zeros_bd = jnp.zeros((n_bp, 2, hv, hk), F32)
    zeros_s5 = jnp.zeros((n_bp, 2, 2 * S5_CH), F32)
    zeros_lru = jnp.zeros((n_bp, 2, BR_W), F32)

    gla_l, ret_l, s5_l, lru_l = [], [], [], []
    for l in range(depth):
        mod3 = mods[l].reshape(n_rows, 1, N_MOD * D_MODEL)
        zg, z = _premix(x, mod3, norm_mix[l].reshape(1, D_MODEL), w_in_p[l], make_row_of_tile(PRE_TM), PRE_TM)

        wd = jnp.zeros((128, 256), F32)
        wd = wd.at[0:GLA_RANK, 0:hk].set(gla_w_decay[l, 0]).at[GLA_RANK:2 * GLA_RANK, hk:].set(gla_w_decay[l, 1])
        bd = gla_b_decay[l].reshape(1, 2 * hk)
        lg = jax.nn.log_sigmoid(ret_decay_logit[l].astype(F32))
        lgl = jnp.repeat(lg, RET_DK, axis=1)
        bre = _block_diag(jnp.swapaxes(s5_b_re[l], 1, 2))
        bim = _block_diag(jnp.swapaxes(s5_b_im[l], 1, 2))
        bbd = jnp.concatenate([bre, bim], axis=1).astype(BF16)
        cre = _block_diag(jnp.swapaxes(s5_c_re[l], 1, 2)).astype(BF16)
        cim = _block_diag(jnp.swapaxes(s5_c_im[l], 1, 2)).astype(BF16)
        coef, pf, pb = _s5_disc(s5_a_re[l].reshape(2, S5_CH), s5_a_im[l].reshape(2, S5_CH),
                                jnp.repeat(s5_log_dt[l], S5_STATE, axis=1), F32_ROWS)
        dsk = s5_d[l].reshape(1, BR_W)
        wglu = s5_w_glu[l].astype(BF16)
        wg = jnp.concatenate([_block_diag(lru_w_a[l, 0]), _block_diag(lru_w_x[l, 0]),
                              _block_diag(lru_w_a[l, 1]), _block_diag(lru_w_x[l, 1])], axis=1).astype(BF16)
        bg = jnp.concatenate([lru_b_a[l, 0], lru_b_x[l, 0], lru_b_a[l, 1], lru_b_x[l, 1]]).reshape(1, 4 * BR_W)
        cw = lru_conv_w[l]
        cb = lru_conv_b[l].reshape(1, BR_W)
        lam = lru_lambda[l]

        s5_h0 = state_s5[:, l].reshape(n_bs, 2, 2 * S5_CH)

        outs = []
        for (row_off, n_b, t_len, sg, sr, ss, sl, cs, sn, rope) in (
                (0, n_bp, t_p, zeros_bd, zeros_bd, zeros_s5, zeros_lru, cos_p, sin_p, False),
                (ntp // t_s, n_bs, t_s, _state_to_bd_t(state_gla[:, l]), _state_to_bd_t(state_ret[:, l]),
                 s5_h0, state_lru[:, l], cos_s, sin_s, True)):
            m_a, f_gla = _gla(z, row_off, n_b, t_len, wd, bd, sg, e_mat, ind)
            m_b, f_ret = _ret(z, row_off, n_b, t_len, cs, sn, swap, lgl, lg, sr, ind, rope)
            m_c, f_s5 = _s5(z, row_off, n_b, t_len, bbd, cre, cim, coef, pf, pb, dsk, wglu, ss)
            m_d, f_lru = _lru(z, row_off, n_b, t_len, cw, cb, wg, bg, lam, sl)
            outs.append(((m_a, m_b, m_c, m_d), (f_gla, f_ret, f_s5, f_lru)))

        ms = [jnp.concatenate([outs[0][0][n], outs[1][0][n]], axis=0) for n in range(4)]
        f_gla, f_ret, f_s5, f_lru = outs[0][1]
        gla_l.append(_bd_t_to_state(f_gla, GLA_HEADS, GLA_DK, GLA_DV))
        ret_l.append(_bd_t_to_state(f_ret, RET_HEADS, RET_DK, RET_DV))
        s5_l.append(f_s5.reshape(n_bp, 2, 2, S5_GROUPS, S5_STATE))
        lru_l.append(f_lru)

        x, h2t = _merge(ms, zg, x, mod3, norm_ffn[l].reshape(1, D_MODEL), w_branch[l].astype(BF16),
                        w_out[l].astype(BF16), make_row_of_tile(tm), tm)

        wqt = peer_w_q[l].T.astype(BF16)
        vt_l = peer_v[l].reshape(PEER_EXPERTS // PEER_TS, PEER_TS, D_MODEL).transpose(0, 2, 1).astype(BF16)
        kbt = peer_keys[l].reshape(2 * PEER_HEADS, PEER_KEYS, PEER_QDIM // 2).astype(BF16)
        x = _peer(h2t, x, mod3, wqt, kbt, peer_u[l].astype(BF16), vt_l,
                  make_row_of_tile(PEER_TM), PEER_TM, PEER_TE, PEER_TS)

    y = _final_norm(x, norm_final.reshape(1, D_MODEL), tm)
    y_p = y[:ntp].reshape(n_bp, t_p, D_MODEL)
    y_s = y[ntp:].reshape(n_bs, t_s, D_MODEL)
    return (y_p, y_s, jnp.stack(gla_l, axis=1), jnp.stack(ret_l, axis=1),
            jnp.stack(s5_l, axis=1), jnp.stack(lru_l, axis=1))
```

```python
import functools
import math

import jax
import jax.numpy as jnp
from jax import lax
from jax.experimental import pallas as pl
from jax.experimental.pallas import tpu as pltpu

F32 = jnp.float32
BF16 = jnp.bfloat16
HI = lax.Precision.HIGHEST

D_MODEL = 1024
N_MOD = 6
EPS = 1e-6
BR_W = 256
GLA_HEADS, GLA_DK, GLA_DV, GLA_RANK, GLA_TAU, GLA_CHUNK = 4, 32, 64, 16, 16.0, 32
RET_HEADS, RET_DK, RET_DV, RET_CHUNK = 4, 32, 64, 64
ROPE_BASE = 10000.0
GRID_W = 64
S5_GROUP, S5_GROUPS, S5_STATE, S5_RE_MAX = 16, 16, 64, -1e-4
S5_CH = S5_GROUPS * S5_STATE
LRU_BLOCKS, LRU_BW, LRU_C = 4, 64, 8.0
PEER_HEADS, PEER_KEYS, PEER_TOPK, PEER_QDIM = 8, 128, 16, 128
PEER_EXPERTS = PEER_KEYS * PEER_KEYS

LANES = 128
VMEM_LIMIT = 58 * 1024 * 1024

Z_TN = 1024
Z_GATE_W = 4 * D_MODEL
Z_GLA, Z_RET, Z_LRU, Z_S5 = 0, 1024, 2048, 2560
Z_MIX_W = 3072
Z_W = Z_GATE_W + Z_MIX_W

SCAN_TB = 128
TOK_TM = 512
PRE_TM = 1024
PEER_TM, PEER_TE, PEER_TS = 512, 2048, 512
NEG_INF = float("-inf")


def _cparams(sem):
    return pltpu.CompilerParams(dimension_semantics=sem, vmem_limit_bytes=VMEM_LIMIT)


def _nt(a, b):
    return lax.dot_general(a, b, (((1,), (1,)), ((), ())), preferred_element_type=F32)


def _tn(a, b):
    return lax.dot_general(a, b, (((0,), (0,)), ((), ())), preferred_element_type=F32)


def _mm(a, b):
    return jnp.dot(a, b, preferred_element_type=F32)


def _mm_hi(a, b):
    return jnp.dot(a, b, preferred_element_type=F32, precision=HI)


def _split_bf16(x):
    hi = x.astype(BF16)
    return hi, (x - hi.astype(F32)).astype(BF16)


def _mm_exact_rhs(a, b_exact):
    hi, lo = _split_bf16(a)
    return _mm(hi, b_exact) + _mm(lo, b_exact)


def _mm_exact_lhs(a_exact, b):
    hi, lo = _split_bf16(b)
    return _mm(a_exact, hi) + _mm(a_exact, lo)


def _sigmoid(x):
    return jax.nn.sigmoid(x)


def _silu(x):
    return x * jax.nn.sigmoid(x)


def _gelu(x):
    return jax.nn.gelu(x)


GELU_C0 = math.sqrt(2.0 / math.pi)
GELU_C1 = GELU_C0 * 0.044715
BF16_ROWS = 16
F32_ROWS = 8


def _gelu_tanh(x):
    k = -2.0 / math.log(2.0)
    return x / (1.0 + jnp.exp2(x * (k * GELU_C0 + (k * GELU_C1) * (x * x))))


def _log_sigmoid(x):
    return jnp.minimum(x, 0.0) - jnp.log(1.0 + jnp.exp(-jnp.abs(x)))


def _softplus(x):
    return jnp.maximum(x, 0.0) + jnp.log(1.0 + jnp.exp(-jnp.abs(x)))


def _rms_mod(x, gain, sc, sh):
    ms = jnp.mean(x * x, axis=-1, keepdims=True)
    return x * lax.rsqrt(ms + EPS) * gain * (1.0 + sc) + sh


def _adaln_kernel(c_ref, w_ref, b_ref, o_ref):
    o_ref[...] = _mm_hi(_silu(c_ref[...]), w_ref[...]) + b_ref[...]


def _adaln(cond, w_mod, b_mod):
    n_l = w_mod.shape[0]
    rows = cond.shape[0]
    tn = 1536
    return pl.pallas_call(
        _adaln_kernel,
        grid=(n_l, N_MOD * D_MODEL // tn),
        in_specs=[
            pl.BlockSpec((rows, D_MODEL), lambda l, j: (0, 0)),
            pl.BlockSpec((None, D_MODEL, tn), lambda l, j: (l, 0, j)),
            pl.BlockSpec((None, 1, tn), lambda l, j: (l, 0, j)),
        ],
        out_specs=pl.BlockSpec((None, rows, tn), lambda l, j: (l, 0, j)),
        out_shape=jax.ShapeDtypeStruct((n_l, rows, N_MOD * D_MODEL), F32),
        compiler_params=_cparams(("parallel", "parallel")),
        name="adaln",
    )(cond, w_mod, b_mod.reshape(n_l, 1, N_MOD * D_MODEL))


def _premix_kernel(x_ref, sh_ref, sc_ref, g_ref, w_ref, zg_ref, zm_ref, h_scr):
    j = pl.program_id(1)

    @pl.when(j == 0)
    def _():
        h_scr[...] = _rms_mod(x_ref[...], g_ref[...], sc_ref[...], sh_ref[...]).astype(BF16)

    z = _mm(h_scr[...], w_ref[j])

    @pl.when(j < Z_GATE_W // Z_TN)
    def _():
        zg_ref[...] = z.astype(BF16)

    @pl.when(j >= Z_GATE_W // Z_TN)
    def _():
        zm_ref[...] = z


def _premix(x, mod3, gain, w, row_of_tile, tm):
    nt = x.shape[0]
    n_gate = Z_GATE_W // Z_TN
    return pl.pallas_call(
        _premix_kernel,
        grid=(nt // tm, Z_W // Z_TN),
        in_specs=[
            pl.BlockSpec((tm, D_MODEL), lambda i, j: (i, 0)),
            pl.BlockSpec((None, 1, D_MODEL), lambda i, j: (row_of_tile(i), 0, 0)),
            pl.BlockSpec((None, 1, D_MODEL), lambda i, j: (row_of_tile(i), 0, 1)),
            pl.BlockSpec((1, D_MODEL), lambda i, j: (0, 0)),
            pl.BlockSpec((Z_W // Z_TN, D_MODEL, Z_TN), lambda i, j: (0, 0, 0), pipeline_mode=pl.Buffered(1)),
        ],
        out_specs=[
            pl.BlockSpec((tm, Z_TN), lambda i, j: (i, jnp.minimum(j, n_gate - 1))),
            pl.BlockSpec((tm, Z_TN), lambda i, j: (i, jnp.maximum(j - n_gate, 0))),
        ],
        out_shape=[
            jax.ShapeDtypeStruct((nt, Z_GATE_W), BF16),
            jax.ShapeDtypeStruct((nt, Z_MIX_W), F32),
        ],
        scratch_shapes=[pltpu.VMEM((tm, D_MODEL), BF16)],
        compiler_params=_cparams(("parallel", "arbitrary")),
        name="premix",
    )(x, mod3, mod3, gain, w)


def _gla_kernel(z_ref, wd_ref, bd_ref, s0_ref, e_ref, ind_ref, m_ref, sfin_ref,
                la_scr, of_scr, st_scr, p_scr, cum_scr, k_scr, v_scr):
    t_len = z_ref.shape[0]
    c = GLA_CHUNK
    n_chunks = t_len // c
    hk = GLA_HEADS * GLA_DK
    hv = GLA_HEADS * GLA_DV
    scale = GLA_DK ** -0.5

    pre = _mm_hi(z_ref[:, 768:896], wd_ref[...]) + bd_ref[...]
    la_scr[...] = _log_sigmoid(pre) * (1.0 / GLA_TAU)

    ri = lax.broadcasted_iota(jnp.int32, (c, c), 0)
    ci = lax.broadcasted_iota(jnp.int32, (c, c), 1)
    tri_lo = (ri >= ci).astype(BF16)
    tri_up = (ri <= ci).astype(BF16)
    row = lax.broadcasted_iota(jnp.int32, (c, hk), 0)
    bd_mask = (lax.broadcasted_iota(jnp.int32, (hv, hk), 0) // GLA_DV
               == lax.broadcasted_iota(jnp.int32, (hv, hk), 1) // GLA_DK).astype(F32)

    def chunk(base, la, tri, reverse, slot):
        cum = _mm_exact_lhs(tri, la)
        q = z_ref[pl.ds(base, c), 0:128] * scale
        k = z_ref[pl.ds(base, c), 128:256]
        v = z_ref[pl.ds(base, c), 256:512]
        edge = cum[0:1, :] if reverse else cum[c - 1:c, :]
        st = st_scr[slot]
        o = _nt((q * jnp.exp(cum)).astype(BF16), st.astype(BF16))
        ke = k * jnp.exp(edge - cum)
        cum_scr[slot] = cum
        k_scr[slot] = k
        v_scr[slot] = v
        for j in range(c):
            d = jnp.minimum(cum - cum_scr[slot, j:j + 1, :], 0.0)
            p = q * k_scr[slot, j:j + 1, :] * jnp.exp(d)
            keep = (row <= j) if reverse else (row >= j)
            p_scr[slot, j * c:(j + 1) * c, :] = jnp.where(keep, p, 0.0).astype(BF16)
        pe = _mm(p_scr[slot], e_ref[...])
        for j in range(c):
            o = o + pe[j * c:(j + 1) * c, :] * v_scr[slot, j:j + 1, :]
        st_scr[slot] = st * jnp.exp(edge) + bd_mask * _tn(v.astype(BF16), ke.astype(BF16))
        return o

    st_scr[0] = s0_ref[0]
    st_scr[1] = s0_ref[1]

    def sweep(i, carry):
        bf = pl.multiple_of(i * c, c)
        bb = pl.multiple_of((n_chunks - 1 - i) * c, c)
        of_scr[pl.ds(bf, c), :] = chunk(bf, la_scr[pl.ds(bf, c), 0:128], tri_lo, False, 0)
        m_ref[pl.ds(bb, c), :] = chunk(bb, la_scr[pl.ds(bb, c), 128:256], tri_up, True, 1)
        return carry

    lax.fori_loop(0, n_chunks, sweep, 0)
    sfin_ref[0] = st_scr[0]
    sfin_ref[1] = st_scr[1]

    fb = 8 * c

    def finish(i, carry):
        base = pl.multiple_of(i * fb, fb)
        o = of_scr[pl.ds(base, fb), :] + m_ref[pl.ds(base, fb), :]
        ms = _mm_exact_rhs(o * o, ind_ref[...])
        g = z_ref[pl.ds(base, fb), 512:768]
        m_ref[pl.ds(base, fb), :] = o * lax.rsqrt(ms + EPS) * _silu(g)
        return carry

    lax.fori_loop(0, t_len // fb, finish, 0)


def _gla(z, row_off, n_b, t_len, wd, bd, s0t, e_mat, ind):
    hk, hv = GLA_HEADS * GLA_DK, GLA_HEADS * GLA_DV
    c = GLA_CHUNK
    return pl.pallas_call(
        _gla_kernel,
        grid=(n_b,),
        in_specs=[
            pl.BlockSpec((t_len, 1024), lambda b: (row_off + b, Z_GLA // 1024)),
            pl.BlockSpec((128, 256), lambda b: (0, 0)),
            pl.BlockSpec((1, 256), lambda b: (0, 0)),
            pl.BlockSpec((None, 2, hv, hk), lambda b: (b, 0, 0, 0)),
            pl.BlockSpec((hk, hv), lambda b: (0, 0)),
            pl.BlockSpec((hv, hv), lambda b: (0, 0)),
        ],
        out_specs=[
            pl.BlockSpec((t_len, BR_W), lambda b: (b, 0)),
            pl.BlockSpec((None, 2, hv, hk), lambda b: (b, 0, 0, 0)),
        ],
        out_shape=[
            jax.ShapeDtypeStruct((n_b * t_len, BR_W), F32),
            jax.ShapeDtypeStruct((n_b, 2, hv, hk), F32),
        ],
        scratch_shapes=[
            pltpu.VMEM((t_len, 256), F32),
            pltpu.VMEM((t_len, hv), F32),
            pltpu.VMEM((2, hv, hk), F32),
            pltpu.VMEM((2, c * c, hk), BF16),
            pltpu.VMEM((2, c, hk), F32),
            pltpu.VMEM((2, c, hk), F32),
            pltpu.VMEM((2, c, hv), F32),
        ],
        compiler_params=_cparams(("parallel",)),
        name="gla",
    )(z, wd, bd, s0t, e_mat, ind)


def _ret_kernel(z_ref, cos_ref, sin_ref, swap_ref, lgl_ref, lgs_ref, s0_ref, ind_ref, m_ref, sfin_ref,
                qk_scr, of_scr, st_scr, *, rope):
    t_len = z_ref.shape[0]
    c = RET_CHUNK
    n_chunks = t_len // c
    hk = RET_HEADS * RET_DK
    hv = RET_HEADS * RET_DV
    scale = RET_DK ** -0.5

    lgf = lgl_ref[0:1, :]
    lgb = lgl_ref[1:2, :]
    pos = lax.broadcasted_iota(jnp.int32, (c, hk), 0).astype(F32)
    wq_f = jnp.exp(lgf * (pos + 1.0))
    wk_f = jnp.exp(lgf * (c - 1.0 - pos))
    wq_b = jnp.exp(lgb * (c - pos))
    wk_b = jnp.exp(lgb * pos)
    dec_f = jnp.exp(lgf * float(c))
    dec_b = jnp.exp(lgb * float(c))

    ii = lax.broadcasted_iota(jnp.int32, (c, c), 0)
    jj = lax.broadcasted_iota(jnp.int32, (c, c), 1)
    rel = (ii - jj).astype(F32)
    dms = []
    for h in range(RET_HEADS):
        d_f = jnp.where(ii >= jj, jnp.exp(lgs_ref[0, h] * jnp.maximum(rel, 0.0)), 0.0)
        d_b = jnp.where(jj >= ii, jnp.exp(lgs_ref[1, h] * jnp.maximum(-rel, 0.0)), 0.0)
        dms.append(d_f + d_b)
    dmat = jnp.concatenate(dms, axis=1)

    ek_mask = (lax.broadcasted_iota(jnp.int32, (RET_HEADS * c, hk), 0) // c
               == lax.broadcasted_iota(jnp.int32, (RET_HEADS * c, hk), 1) // RET_DK).astype(F32)
    ev_mask = (lax.broadcasted_iota(jnp.int32, (RET_HEADS * c, hv), 0) // c
               == lax.broadcasted_iota(jnp.int32, (RET_HEADS * c, hv), 1) // RET_DV).astype(F32)
    bd_mask = (lax.broadcasted_iota(jnp.int32, (hv, hk), 0) // RET_DV
               == lax.broadcasted_iota(jnp.int32, (hv, hk), 1) // RET_DK).astype(F32)

    st_scr[...] = s0_ref[0]

    def fwd(i, carry):
        base = pl.multiple_of(i * c, c)
        q = z_ref[pl.ds(base, c), 0:128] * scale
        k = z_ref[pl.ds(base, c), 128:256]
        v = z_ref[pl.ds(base, c), 256:512]
        if rope:
            cs = cos_ref[pl.ds(base, c), :]
            sn = sin_ref[pl.ds(base, c), :]
            q = q * cs + _mm_exact_rhs(q, swap_ref[...]) * sn
            k = k * cs + _mm_exact_rhs(k, swap_ref[...]) * sn
        qk_scr[pl.ds(base, c), 0:128] = q
        qk_scr[pl.ds(base, c), 128:256] = k
        kexp = (jnp.concatenate([k] * RET_HEADS, axis=0) * ek_mask).astype(BF16)
        vexp = (jnp.concatenate([v] * RET_HEADS, axis=0) * ev_mask).astype(BF16)
        sc = _nt(q.astype(BF16), kexp) * dmat
        o = _mm(sc.astype(BF16), vexp)
        st = st_scr[...]
        o = o + _nt((q * wq_f).astype(BF16), st.astype(BF16))
        of_scr[pl.ds(base, c), :] = o
        st_scr[...] = st * dec_f + bd_mask * _tn(v.astype(BF16), (k * wk_f).astype(BF16))
        return carry

    lax.fori_loop(0, n_chunks, fwd, 0)
    sfin_ref[0] = st_scr[...]
    st_scr[...] = s0_ref[1]

    def bwd(i, carry):
        base = pl.multiple_of((n_chunks - 1 - i) * c, c)
        q = qk_scr[pl.ds(base, c), 0:128]
        k = qk_scr[pl.ds(base, c), 128:256]
        v = z_ref[pl.ds(base, c), 256:512]
        st = st_scr[...]
        o = of_scr[pl.ds(base, c), :] + _nt((q * wq_b).astype(BF16), st.astype(BF16))
        st_scr[...] = st * dec_b + bd_mask * _tn(v.astype(BF16), (k * wk_b).astype(BF16))
        ms = _mm_exact_rhs(o * o, ind_ref[...])
        g = z_ref[pl.ds(base, c), 512:768]
        m_ref[pl.ds(base, c), :] = o * lax.rsqrt(ms + EPS) * _silu(g)
        return carry

    lax.fori_loop(0, n_chunks, bwd, 0)
    sfin_ref[1] = st_scr[...]


def _ret(z, row_off, n_b, t_len, cos_t, sin_t, swap, lgl, lgs, s0t, ind, rope):
    hk, hv = RET_HEADS * RET_DK, RET_HEADS * RET_DV
    return pl.pallas_call(
        functools.partial(_ret_kernel, rope=rope),
        grid=(n_b,),
        in_specs=[
            pl.BlockSpec((t_len, 1024), lambda b: (row_off + b, Z_RET // 1024)),
            pl.BlockSpec((t_len, hk), lambda b: (0, 0)),
            pl.BlockSpec((t_len, hk), lambda b: (0, 0)),
            pl.BlockSpec((hk, hk), lambda b: (0, 0)),
            pl.BlockSpec((2, hk), lambda b: (0, 0)),
            pl.BlockSpec(memory_space=pltpu.SMEM),
            pl.BlockSpec((None, 2, hv, hk), lambda b: (b, 0, 0, 0)),
            pl.BlockSpec((hv, hv), lambda b: (0, 0)),
        ],
        out_specs=[
            pl.BlockSpec((t_len, BR_W), lambda b: (b, 0)),
            pl.BlockSpec((None, 2, hv, hk), lambda b: (b, 0, 0, 0)),
        ],
        out_shape=[
            jax.ShapeDtypeStruct((n_b * t_len, BR_W), F32),
            jax.ShapeDtypeStruct((n_b, 2, hv, hk), F32),
        ],
        scratch_shapes=[
            pltpu.VMEM((t_len, 2 * hk), F32),
            pltpu.VMEM((t_len, hv), F32),
            pltpu.VMEM((hv, hk), F32),
        ],
        compiler_params=_cparams(("parallel",)),
        name="ret",
    )(z, cos_t, sin_t, swap, lgl, lgs, s0t, ind)


def _s5_disc_kernel(are_ref, aim_ref, ldt_ref, coef_ref, pf_ref, pb_ref):
    tb = pf_ref.shape[0]
    re = jnp.minimum(are_ref[...], S5_RE_MAX)
    im = aim_ref[...]
    dt = jnp.exp(ldt_ref[...])
    er = jnp.exp(re * dt)
    lbr = er * jnp.cos(im * dt)
    lbi = er * jnp.sin(im * dt)
    den = re * re + im * im
    nr = lbr - 1.0
    coef_ref[:, 0:S5_CH] = (nr * re + lbi * im) / den
    coef_ref[:, S5_CH:] = (lbi * re - nr * im) / den
    t = lax.broadcasted_iota(jnp.int32, (tb, S5_CH), 0).astype(F32)
    nf = t + 1.0
    nb = float(tb) - t
    mf = jnp.exp(nf * (re[0:1] * dt[0:1]))
    pf_ref[:, 0:S5_CH] = mf * jnp.cos(nf * (im[0:1] * dt[0:1]))
    pf_ref[:, S5_CH:] = mf * jnp.sin(nf * (im[0:1] * dt[0:1]))
    mb = jnp.exp(nb * (re[1:2] * dt[1:2]))
    pb_ref[:, 0:S5_CH] = mb * jnp.cos(nb * (im[1:2] * dt[1:2]))
    pb_ref[:, S5_CH:] = mb * jnp.sin(nb * (im[1:2] * dt[1:2]))


def _s5_disc(are, aim, ldt, tb):
    return pl.pallas_call(
        _s5_disc_kernel,
        out_shape=[
            jax.ShapeDtypeStruct((2, 2 * S5_CH), F32),
            jax.ShapeDtypeStruct((tb, 2 * S5_CH), F32),
            jax.ShapeDtypeStruct((tb, 2 * S5_CH), F32),
        ],
        compiler_params=pltpu.CompilerParams(vmem_limit_bytes=VMEM_LIMIT),
        name="s5_disc",
    )(are, aim, ldt)


def _s5_kernel(u_ref, bbd_ref, cre_ref, cim_ref, coef_ref, pf_ref, pb_ref, dsk_ref, wglu_ref, h0_ref,
               m_ref, hfin_ref, y_scr, bu_scr, hr_scr, hi_scr, car_scr):
    t_len = u_ref.shape[0]
    tb = SCAN_TB
    sub = pf_ref.shape[0]
    n_blocks = t_len // tb
    row_in = lax.broadcasted_iota(jnp.int32, (tb, LANES), 0) % sub
    steps = [1 << s for s in range(int(math.log2(sub)))]

    def block(base, dr_i, reverse):
        p_ref = pb_ref if reverse else pf_ref
        u = u_ref[pl.ds(base, tb), :]
        bu_scr[...] = _mm(u.astype(BF16), bbd_ref[...])
        for g in range(S5_CH // LANES):
            lo, hi = g * LANES, (g + 1) * LANES
            br = bu_scr[:, lo:hi]
            bi = bu_scr[:, S5_CH + lo:S5_CH + hi]
            cr = coef_ref[dr_i:dr_i + 1, lo:hi]
            ci = coef_ref[dr_i:dr_i + 1, S5_CH + lo:S5_CH + hi]
            hr = cr * br - ci * bi
            hi_ = cr * bi + ci * br
            for d in steps:
                if reverse:
                    pr = p_ref[sub - d:sub - d + 1, lo:hi]
                    pi = p_ref[sub - d:sub - d + 1, S5_CH + lo:S5_CH + hi]
                    keep = row_in < sub - d
                    sr = jnp.where(keep, pltpu.roll(hr, tb - d, 0), 0.0)
                    si = jnp.where(keep, pltpu.roll(hi_, tb - d, 0), 0.0)
                else:
                    pr = p_ref[d - 1:d, lo:hi]
                    pi = p_ref[d - 1:d, S5_CH + lo:S5_CH + hi]
                    keep = row_in >= d
                    sr = jnp.where(keep, pltpu.roll(hr, d, 0), 0.0)
                    si = jnp.where(keep, pltpu.roll(hi_, d, 0), 0.0)
                hr, hi_ = hr + pr * sr - pi * si, hi_ + pr * si + pi * sr
            car = car_scr[0:1, lo:hi]
            cai = car_scr[0:1, S5_CH + lo:S5_CH + hi]
            pwr = p_ref[:, lo:hi]
            pwi = p_ref[:, S5_CH + lo:S5_CH + hi]
            n_grp = tb // sub
            for v in (range(n_grp - 1, -1, -1) if reverse else range(n_grp)):
                gr = hr[v * sub:(v + 1) * sub, :] + pwr * car - pwi * cai
                gi = hi_[v * sub:(v + 1) * sub, :] + pwr * cai + pwi * car
                hr_scr[v * sub:(v + 1) * sub, lo:hi] = gr
                hi_scr[v * sub:(v + 1) * sub, lo:hi] = gi
                edge = 0 if reverse else sub - 1
                car, cai = gr[edge:edge + 1, :], gi[edge:edge + 1, :]
            car_scr[0:1, lo:hi] = car
            car_scr[0:1, S5_CH + lo:S5_CH + hi] = cai
        y = _mm(hr_scr[...].astype(BF16), cre_ref[...]) - _mm(hi_scr[...].astype(BF16), cim_ref[...])
        return u, y

    car_scr[0:1, :] = h0_ref[0:1, :]

    def fwd(i, carry):
        base = pl.multiple_of(i * tb, tb)
        _, y = block(base, 0, False)
        y_scr[pl.ds(base, tb), :] = y
        return carry

    lax.fori_loop(0, n_blocks, fwd, 0)
    hfin_ref[0:1, :] = car_scr[0:1, :]
    car_scr[0:1, :] = h0_ref[1:2, :]

    def bwd(i, carry):
        base = pl.multiple_of((n_blocks - 1 - i) * tb, tb)
        u, y = block(base, 1, True)
        y = _gelu(y_scr[pl.ds(base, tb), :] + y + dsk_ref[...] * u)
        gg = _mm(y.astype(BF16), wglu_ref[...])
        m_ref[pl.ds(base, tb), :] = gg[:, 0:BR_W] * _sigmoid(gg[:, BR_W:])
        return carry

    lax.fori_loop(0, n_blocks, bwd, 0)
    hfin_ref[1:2, :] = car_scr[0:1, :]


def _s5(z, row_off, n_b, t_len, bbd, cre, cim, coef, pf, pb, dsk, wglu, h0):
    tb = SCAN_TB
    sub = pf.shape[0]
    full = lambda shape: pl.BlockSpec(shape, lambda b: (0,) * len(shape))
    return pl.pallas_call(
        _s5_kernel,
        grid=(n_b,),
        in_specs=[
            pl.BlockSpec((t_len, BR_W), lambda b: (row_off + b, Z_S5 // BR_W)),
            full((BR_W, 2 * S5_CH)),
            full((S5_CH, BR_W)),
            full((S5_CH, BR_W)),
            full((2, 2 * S5_CH)),
            full((sub, 2 * S5_CH)),
            full((sub, 2 * S5_CH)),
            full((1, BR_W)),
            full((BR_W, 2 * BR_W)),
            pl.BlockSpec((None, 2, 2 * S5_CH), lambda b: (b, 0, 0)),
        ],
        out_specs=[
            pl.BlockSpec((t_len, BR_W), lambda b: (b, 0)),
            pl.BlockSpec((None, 2, 2 * S5_CH), lambda b: (b, 0, 0)),
        ],
        out_shape=[
            jax.ShapeDtypeStruct((n_b * t_len, BR_W), F32),
            jax.ShapeDtypeStruct((n_b, 2, 2 * S5_CH), F32),
        ],
        scratch_shapes=[
            pltpu.VMEM((t_len, BR_W), F32),
            pltpu.VMEM((tb, 2 * S5_CH), F32),
            pltpu.VMEM((tb, S5_CH), F32),
            pltpu.VMEM((tb, S5_CH), F32),
            pltpu.VMEM((8, 2 * S5_CH), F32),
        ],
        compiler_params=_cparams(("parallel",)),
        name="s5",
    )(z, bbd, cre, cim, coef, pf, pb, dsk, wglu, h0)


def _lru_kernel(z_ref, cw_ref, cb_ref, wg_ref, bg_ref, lam_ref, h0_ref, m_ref, hfin_ref,
                xc_scr, hf_scr, car_scr):
    t_len = z_ref.shape[0]
    tb = SCAN_TB
    n_blocks = t_len // tb
    steps = [1 << s for s in range(int(math.log2(tb)))]

    x = z_ref[:, 0:BR_W]
    trow = lax.broadcasted_iota(jnp.int32, (t_len, BR_W), 0)
    xm1 = jnp.where(trow >= 1, pltpu.roll(x, 1, 0), 0.0)
    xp1 = jnp.where(trow < t_len - 1, pltpu.roll(x, t_len - 1, 0), 0.0)
    xp2 = jnp.where(trow < t_len - 2, pltpu.roll(x, t_len - 2, 0), 0.0)
    xc_scr[...] = (cw_ref[0:1, :] * xm1 + cw_ref[1:2, :] * x + cw_ref[2:3, :] * xp1
                   + cw_ref[3:4, :] * xp2 + cb_ref[...])

    row = lax.broadcasted_iota(jnp.int32, (tb, BR_W), 0)
    sp = _softplus(-lam_ref[...])

    def block(base, dr_i, reverse):
        xc = xc_scr[pl.ds(base, tb), :]
        off = dr_i * 2 * BR_W
        gates = _mm(xc.astype(BF16), wg_ref[:, off:off + 2 * BR_W]) + bg_ref[:, off:off + 2 * BR_W]
        r = _sigmoid(gates[:, 0:BR_W])
        ig = _sigmoid(gates[:, BR_W:])
        log_a = -LRU_C * r * sp[dr_i:dr_i + 1, :]
        a = jnp.exp(log_a)
        th = jnp.tanh(log_a)
        b = jnp.sqrt(-2.0 * th / (1.0 - th)) * (ig * xc)
        for d in steps:
            if reverse:
                keep = row < tb - d
                a_s = jnp.where(keep, pltpu.roll(a, tb - d, 0), 1.0)
                b_s = jnp.where(keep, pltpu.roll(b, tb - d, 0), 0.0)
            else:
                keep = row >= d
                a_s = jnp.where(keep, pltpu.roll(a, d, 0), 1.0)
                b_s = jnp.where(keep, pltpu.roll(b, d, 0), 0.0)
            b = b + a * b_s
            a = a * a_s
        h = b + a * car_scr[dr_i:dr_i + 1, :]
        edge = 0 if reverse else tb - 1
        car_scr[dr_i:dr_i + 1, :] = h[edge:edge + 1, :]
        return h

    car_scr[0:2, :] = h0_ref[...]

    def fwd(i, carry):
        base = pl.multiple_of(i * tb, tb)
        hf_scr[pl.ds(base, tb), :] = block(base, 0, False)
        return carry

    lax.fori_loop(0, n_blocks, fwd, 0)

    def bwd(i, carry):
        base = pl.multiple_of((n_blocks - 1 - i) * tb, tb)
        h = block(base, 1, True) + hf_scr[pl.ds(base, tb), :]
        m_ref[pl.ds(base, tb), :] = h * _gelu(z_ref[pl.ds(base, tb), BR_W:2 * BR_W])
        return carry

    lax.fori_loop(0, n_blocks, bwd, 0)
    hfin_ref[...] = car_scr[0:2, :]


def _lru(z, row_off, n_b, t_len, cw, cb, wg, bg, lam, h0):
    full = lambda shape: pl.BlockSpec(shape, lambda b: (0,) * len(shape))
    return pl.pallas_call(
        _lru_kernel,
        grid=(n_b,),
        in_specs=[
            pl.BlockSpec((t_len, 2 * BR_W), lambda b: (row_off + b, Z_LRU // (2 * BR_W))),
            full((4, BR_W)),
            full((1, BR_W)),
            full((BR_W, 4 * BR_W)),
            full((1, 4 * BR_W)),
            full((2, BR_W)),
            pl.BlockSpec((None, 2, BR_W), lambda b: (b, 0, 0)),
        ],
        out_specs=[
            pl.BlockSpec((t_len, BR_W), lambda b: (b, 0)),
            pl.BlockSpec((None, 2, BR_W), lambda b: (b, 0, 0)),
        ],
        out_shape=[
            jax.ShapeDtypeStruct((n_b * t_len, BR_W), F32),
            jax.ShapeDtypeStruct((n_b, 2, BR_W), F32),
        ],
        scratch_shapes=[
            pltpu.VMEM((t_len, BR_W), F32),
            pltpu.VMEM((t_len, BR_W), F32),
            pltpu.VMEM((8, BR_W), F32),
        ],
        compiler_params=_cparams(("parallel",)),
        name="lru",
    )(z, cw, cb, wg, bg, lam, h0)


def _merge_kernel(ma_ref, mb_ref, mc_ref, md_ref, zg_ref, x_ref, g1_ref, sh2_ref, sc2_ref, gain_ref,
                  wb_ref, wo_ref, xo_ref, h2t_ref):
    acc = None
    for n, m_ref in enumerate((ma_ref, mb_ref, mc_ref, md_ref)):
        proj = _mm(m_ref[...].astype(BF16), wb_ref[n])
        term = _sigmoid(zg_ref[:, n * D_MODEL:(n + 1) * D_MODEL].astype(F32)) * proj
        acc = term if acc is None else acc + term
    xn = x_ref[...] + g1_ref[...] * _mm(acc.astype(BF16), wo_ref[...])
    xo_ref[...] = xn
    h2t_ref[...] = _rms_mod(xn, gain_ref[...], sc2_ref[...], sh2_ref[...]).T.astype(BF16)


def _merge(ms, z, x, mod3, gain, wb, wo, row_of_tile, tm):
    nt = x.shape[0]
    modspec = lambda k: pl.BlockSpec((None, 1, D_MODEL), lambda i: (row_of_tile(i), 0, k))
    return pl.pallas_call(
        _merge_kernel,
        grid=(nt // tm,),
        in_specs=[pl.BlockSpec((tm, BR_W), lambda i: (i, 0))] * 4 + [
            pl.BlockSpec((tm, 4 * D_MODEL), lambda i: (i, 0)),
            pl.BlockSpec((tm, D_MODEL), lambda i: (i, 0)),
            modspec(2), modspec(3), modspec(4),
            pl.BlockSpec((1, D_MODEL), lambda i: (0, 0)),
            pl.BlockSpec((4, BR_W, D_MODEL), lambda i: (0, 0, 0)),
            pl.BlockSpec((D_MODEL, D_MODEL), lambda i: (0, 0)),
        ],
        out_specs=[
            pl.BlockSpec((tm, D_MODEL), lambda i: (i, 0)),
            pl.BlockSpec((D_MODEL, tm), lambda i: (0, i)),
        ],
        out_shape=[
            jax.ShapeDtypeStruct((nt, D_MODEL), F32),
            jax.ShapeDtypeStruct((D_MODEL, nt), BF16),
        ],
        compiler_params=_cparams(("parallel",)),
        name="merge",
    )(*ms, z, x, mod3, mod3, mod3, gain, wb, wo)


def _oddeven_merge_sort_pairs(n):
    pairs = []
    p = 1
    while p < n:
        k = p
        while k >= 1:
            for j in range(k % p, n - k, 2 * k):
                for i in range(min(k, n - j - k)):
                    if (i + j) // (2 * p) == (i + j + k) // (2 * p):
                        pairs.append((i + j, i + j + k))
            k //= 2
        p *= 2
    return pairs


_SORT16 = _oddeven_merge_sort_pairs(PEER_KEYS // F32_ROWS)


def _peer_kernel(h2t_ref, x_ref, g2_ref, wqt_ref, kbt_ref, *rest, te, ts):
    n_sub = te // ts
    u_refs, vt_refs = rest[:n_sub], rest[n_sub:2 * n_sub]
    xo_ref, n_scr, r1_scr, a_scr, b_scr, sc_scr, v_scr, at_scr, wa_scr, yt_scr = rest[2 * n_sub:]
    tm = h2t_ref.shape[1]
    n_lt = tm // LANES
    ic = te // PEER_KEYS
    c_idx = pl.program_id(1)
    nk = PEER_KEYS

    @pl.when(c_idx == 0)
    def _route():
        qt = _mm(wqt_ref[...], h2t_ref[...])
        qtb = qt.astype(BF16)
        qd = PEER_QDIM // 2
        for hs in range(2 * PEER_HEADS):
            sct = _mm(kbt_ref[hs], qtb[hs * qd:(hs + 1) * qd, :])
            for lt in range(n_lt):
                sc_scr[lt, hs * nk:(hs + 1) * nk, :] = sct[:, lt * LANES:(lt + 1) * LANES]
        r8 = lax.broadcasted_iota(jnp.int32, (8, LANES), 0)

        def per_tile(lt, carry):
            def per_head(h, vo):
                o0 = pl.multiple_of(h * (2 * nk), nk)
                o1 = pl.multiple_of(h * (2 * nk) + nk, nk)
                cols = [sc_scr[lt, pl.ds(pl.multiple_of(o0 + F32_ROWS * v, F32_ROWS), F32_ROWS), :]
                        for v in range(nk // F32_ROWS)]
                for ca, cb in _SORT16:
                    cols[ca], cols[cb] = jnp.maximum(cols[ca], cols[cb]), jnp.minimum(cols[ca], cols[cb])
                for r in range(PEER_TOPK):
                    m = jnp.max(cols[0], axis=0, keepdims=True)
                    v_scr[vo + r:vo + r + 1, :] = m
                    hit = cols[0] == m
                    for kk in range(PEER_TOPK - 1 - r):
                        cols[kk] = jnp.where(hit, cols[kk + 1], cols[kk])
                rank1 = jnp.full((nk, LANES), float(PEER_TOPK), F32)
                xs = sc_scr[lt, pl.ds(o1, nk), :]
                for r in range(PEER_TOPK):
                    m = jnp.max(xs, axis=0, keepdims=True)
                    v_scr[vo + PEER_TOPK + r:vo + PEER_TOPK + r + 1, :] = m
                    hit = xs == m
                    rank1 = jnp.where(hit, float(r), rank1)
                    xs = jnp.where(hit, NEG_INF, xs)
                v0 = v_scr[vo:vo + PEER_TOPK, :]
                v1 = v_scr[vo + PEER_TOPK:vo + 2 * PEER_TOPK, :]
                pieces = [v0[0:1, :] + v1]
                for r0 in range(1, 8):
                    pieces.append(jnp.where(r8 < PEER_TOPK // (r0 + 1), v0[r0:r0 + 1, :] + v1[0:8, :], NEG_INF))
                pieces.append(v0[8:16, :] + v1[0:1, :])
                cand = jnp.concatenate(pieces, axis=0)
                top = v0[0:1, :] + v1[0:1, :]
                zsum = jnp.zeros_like(top)
                tau = top
                for r in range(PEER_TOPK):
                    tau = jnp.max(cand, axis=0, keepdims=True)
                    zsum = zsum + jnp.exp(tau - top)
                    cand = jnp.where(cand == tau, NEG_INF, cand)
                ho = pl.multiple_of(h * nk, nk)
                s0 = sc_scr[lt, pl.ds(o0, nk), :]
                s1 = sc_scr[lt, pl.ds(o1, nk), :]
                crank = jnp.zeros((PEER_TOPK, LANES), F32)
                for r1 in range(PEER_TOPK):
                    crank = crank + jnp.where(v0 + v1[r1:r1 + 1, :] >= tau, 1.0, 0.0)
                cnt = jnp.zeros((nk, LANES), F32)
                half = PEER_TOPK // 2
                for m in range(1, half + 1):
                    u_m = jnp.min(jnp.where(crank >= float(m), v0, jnp.inf), axis=0, keepdims=True)
                    cnt = jnp.where(s0 >= u_m, float(m), cnt)
                cnt = jnp.where(s0 >= v0[0:1, :], crank[0:1, :], cnt)
                n_scr[lt, pl.ds(ho, nk), :] = cnt
                r1_scr[lt, pl.ds(ho, nk), :] = rank1
                a_scr[lt, pl.ds(ho, nk), :] = jnp.exp(s0 - v0[0:1, :])
                b_scr[lt, pl.ds(ho, nk), :] = jnp.exp(s1 - v1[0:1, :]) / zsum

            def per_pair(hp, carry2):
                per_head(2 * hp, 0)
                per_head(2 * hp + 1, 2 * PEER_TOPK)
                return carry2

            return lax.fori_loop(0, PEER_HEADS // 2, per_pair, carry)

        lax.fori_loop(0, n_lt, per_tile, 0)
        yt_scr[...] = jnp.zeros_like(yt_scr)

    groups = ts // nk

    def a_stage(k):
        at_scr[k % 2] = _mm(u_refs[k][...], h2t_ref[...])

    def y_stage(k):
        yt_scr[...] += _mm(vt_refs[k][...], wa_scr[k % 2])

    def w_block(k, lt, ii, dep):
        pk = BF16_ROWS
        nv = nk // pk
        cols = slice(lt * LANES, (lt + 1) * LANES)
        acc = [None] * nv
        for h in range(PEER_HEADS):
            row = h * nk + c_idx * ic + k * groups + ii
            n_t = jnp.broadcast_to(n_scr[lt, pl.ds(row, 1), :] + dep, (pk, LANES))
            a_t = jnp.broadcast_to(a_scr[lt, pl.ds(row, 1), :] + dep, (pk, LANES))
            for jv in range(nv):
                r1 = r1_scr[lt, h * nk + jv * pk:h * nk + (jv + 1) * pk, :]
                b1 = b_scr[lt, h * nk + jv * pk:h * nk + (jv + 1) * pk, :]
                term = jnp.where(r1 < n_t, b1, 0.0) * a_t
                acc[jv] = term if acc[jv] is None else acc[jv] + term
        out = None
        for jv in range(nv):
            rows = slice(ii * nk + jv * pk, ii * nk + (jv + 1) * pk)
            out = acc[jv] * _gelu_tanh(at_scr[k % 2, rows, cols])
            wa_scr[k % 2, rows, cols] = out.astype(BF16)
        last = out[0:1, :]
        return jnp.where((last < 2.0) & (last > -2.0), last, 1.0) * 0.0

    dep = jnp.zeros((1, LANES), F32)
    a_stage(0)
    for k in range(n_sub):
        if k + 1 < n_sub:
            a_stage(k + 1)
        if k >= 1:
            y_stage(k - 1)
        for lt in range(n_lt):
            for ii in range(groups):
                dep = w_block(k, lt, ii, dep)
    y_stage(n_sub - 1)

    @pl.when(c_idx == pl.num_programs(1) - 1)
    def _fin():
        xo_ref[...] = x_ref[...] + g2_ref[...] * yt_scr[...].T


def _peer(h2t, x, mod3, wqt, kbt, u, vt, row_of_tile, tm, te, ts):
    nt = x.shape[0]
    n_lt = tm // LANES
    n_rt = PEER_HEADS * PEER_KEYS
    n_sub = te // ts
    u_specs = [pl.BlockSpec((ts, D_MODEL), lambda i, c, k=k: (c * n_sub + k, 0)) for k in range(n_sub)]
    vt_specs = [pl.BlockSpec((None, D_MODEL, ts), lambda i, c, k=k: (c * n_sub + k, 0, 0)) for k in range(n_sub)]
    return pl.pallas_call(
        functools.partial(_peer_kernel, te=te, ts=ts),
        grid=(nt // tm, PEER_EXPERTS // te),
        in_specs=[
            pl.BlockSpec((D_MODEL, tm), lambda i, c: (0, i), pipeline_mode=pl.Buffered(1)),
            pl.BlockSpec((tm, D_MODEL), lambda i, c: (i, 0), pipeline_mode=pl.Buffered(1)),
            pl.BlockSpec((None, 1, D_MODEL), lambda i, c: (row_of_tile(i), 0, 5)),
            pl.BlockSpec((D_MODEL, D_MODEL), lambda i, c: (0, 0), pipeline_mode=pl.Buffered(1)),
            pl.BlockSpec((2 * PEER_HEADS, PEER_KEYS, PEER_QDIM // 2), lambda i, c: (0, 0, 0),
                         pipeline_mode=pl.Buffered(1)),
        ] + u_specs + vt_specs,
        out_specs=pl.BlockSpec((tm, D_MODEL), lambda i, c: (i, 0), pipeline_mode=pl.Buffered(1)),
        out_shape=jax.ShapeDtypeStruct((nt, D_MODEL), F32),
        scratch_shapes=[
            pltpu.VMEM((n_lt, n_rt, LANES), F32),
            pltpu.VMEM((n_lt, n_rt, LANES), F32),
            pltpu.VMEM((n_lt, n_rt, LANES), F32),
            pltpu.VMEM((n_lt, n_rt, LANES), F32),
            pltpu.VMEM((n_lt, 2 * n_rt, LANES), F32),
            pltpu.VMEM((4 * PEER_TOPK, LANES), F32),
            pltpu.VMEM((2, ts, tm), F32),
            pltpu.VMEM((2, ts, tm), BF16),
            pltpu.VMEM((D_MODEL, tm), F32),
        ],
        compiler_params=_cparams(("parallel", "arbitrary")),
        name="peer",
    )(h2t, x, mod3, wqt, kbt, *([u] * n_sub), *([vt] * n_sub))


def _final_kernel(x_ref, g_ref, o_ref):
    x = x_ref[...]
    o_ref[...] = x * lax.rsqrt(jnp.mean(x * x, axis=-1, keepdims=True) + EPS) * g_ref[...]


def _final_norm(x, gain, tm):
    nt = x.shape[0]
    return pl.pallas_call(
        _final_kernel,
        grid=(nt // tm,),
        in_specs=[pl.BlockSpec((tm, D_MODEL), lambda i: (i, 0)), pl.BlockSpec((1, D_MODEL), lambda i: (0, 0))],
        out_specs=pl.BlockSpec((tm, D_MODEL), lambda i: (i, 0)),
        out_shape=jax.ShapeDtypeStruct((nt, D_MODEL), F32),
        compiler_params=_cparams(("parallel",)),
        name="final_norm",
    )(x, gain)


def _block_diag(blocks):
    n, r, c = blocks.shape
    eye = jnp.eye(n, dtype=blocks.dtype)
    return jnp.einsum('nrc,nm->nrmc', blocks, eye).reshape(n * r, n * c)


def _state_to_bd_t(s):
    b, two, h, k, v = s.shape
    eye = jnp.eye(h, dtype=s.dtype)
    return jnp.einsum('bdhkv,hg->bdhvgk', s, eye).reshape(b, two, h * v, h * k)


def _bd_t_to_state(st, h, k, v):
    b = st.shape[0]
    return jnp.einsum('bdhvhk->bdhkv', st.reshape(b, 2, h, v, h, k))


def _rope_tables(t_len):
    rows = t_len // GRID_W
    row = jnp.repeat(jnp.arange(rows), GRID_W).astype(F32)
    col = jnp.tile(jnp.arange(GRID_W), rows).astype(F32)
    n_freq = RET_DK // 4
    inv_freq = ROPE_BASE ** (-jnp.arange(n_freq, dtype=F32) / n_freq)
    ang = jnp.concatenate([row[:, None] * inv_freq, col[:, None] * inv_freq], axis=-1)
    cos, sin = jnp.cos(ang), jnp.sin(ang)
    cos_h = jnp.concatenate([cos, cos], axis=-1)
    sin_h = jnp.concatenate([-sin, sin], axis=-1)
    return jnp.tile(cos_h, (1, RET_HEADS)), jnp.tile(sin_h, (1, RET_HEADS))


def _swap_matrix():
    lane = jnp.arange(RET_HEADS * RET_DK)
    half = RET_DK // 2
    src = jnp.where(lane % RET_DK < half, lane + half, lane - half)
    return (lane[:, None] == src[None, :]).astype(BF16)


def kernel(x_prompt, x_sample, c, state_gla, state_ret, state_s5, state_lru, c_ctx, w_mod, b_mod, norm_mix, norm_ffn, norm_final, w_in, gla_w_decay, gla_b_decay, ret_decay_logit, s5_a_re, s5_a_im, s5_log_dt, s5_b_re, s5_b_im, s5_c_re, s5_c_im, s5_d, s5_w_glu, lru_conv_w, lru_conv_b, lru_w_a, lru_b_a, lru_w_x, lru_b_x, lru_lambda, w_branch, w_out, peer_w_q, peer_keys, peer_u, peer_v):
    n_bp, t_p, _ = x_prompt.shape
    n_bs, t_s, _ = x_sample.shape
    depth = w_in.shape[0]
    ntp, nts = n_bp * t_p, n_bs * t_s
    tm = TOK_TM
    assert ntp % t_s == 0 and all(ntp % t == 0 and t_s % t == 0 for t in (TOK_TM, PRE_TM, PEER_TM))

    x = jnp.concatenate([x_prompt.reshape(ntp, D_MODEL), x_sample.reshape(nts, D_MODEL)], axis=0)

    n_rows = 8 * ((1 + n_bs + 7) // 8)
    cond = jnp.zeros((n_rows, D_MODEL), F32).at[0].set(c_ctx).at[1:1 + n_bs].set(c)
    mods = _adaln(cond, w_mod, b_mod)

    def make_row_of_tile(tile):
        def row_of_tile(i):
            return jnp.where(i < ntp // tile, 0, 1 + (i - ntp // tile) // (t_s // tile))
        return row_of_tile

    zpad = lambda n: jnp.zeros((depth, D_MODEL, n), F32)
    w_in_p = jnp.concatenate([w_in[:, :, 2336:6432], w_in[:, :, 0:800], zpad(224), w_in[:, :, 800:1568], zpad(256),
                              w_in[:, :, 1824:2336], w_in[:, :, 1568:1824], zpad(256)], axis=2).astype(BF16)
    assert w_in_p.shape[2] == Z_W
    w_in_p = w_in_p.reshape(depth, D_MODEL, Z_W // Z_TN, Z_TN).transpose(0, 2, 1, 3)

    hk, hv = GLA_HEADS * GLA_DK, GLA_HEADS * GLA_DV
    e_mat = (jnp.arange(hk)[:, None] // GLA_DK == jnp.arange(hv)[None, :] // GLA_DV).astype(BF16)
    ind = ((jnp.arange(hv)[:, None] // GLA_DV == jnp.arange(hv)[None, :] // GLA_DV).astype(F32) / GLA_DV).astype(BF16)
    swap = _swap_matrix()
    cos_s, sin_s = _rope_tables(t_s)
    cos_p, sin_p = jnp.ones((t_p, hk), F32), jnp.zeros((t_p, hk), F32)

    zeros_bd = jnp.zeros((n_bp, 2, hv, hk), F32)
    zeros_s5 = jnp.zeros((n_bp, 2, 2 * S5_CH), F32)
    zeros_lru = jnp.zeros((n_bp, 2, BR_W), F32)

    gla_l, ret_l, s5_l, lru_l = [], [], [], []
    for l in range(depth):
        mod3 = mods[l].reshape(n_rows, 1, N_MOD * D_MODEL)
        zg, z = _premix(x, mod3, norm_mix[l].reshape(1, D_MODEL), w_in_p[l], make_row_of_tile(PRE_TM), PRE_TM)

        wd = jnp.zeros((128, 256), F32)
        wd = wd.at[0:GLA_RANK, 0:hk].set(gla_w_decay[l, 0]).at[GLA_RANK:2 * GLA_RANK, hk:].set(gla_w_decay[l, 1])
        bd = gla_b_decay[l].reshape(1, 2 * hk)
        lg = jax.nn.log_sigmoid(ret_decay_logit[l].astype(F32))
        lgl = jnp.repeat(lg, RET_DK, axis=1)
        bre = _block_diag(jnp.swapaxes(s5_b_re[l], 1, 2))
        bim = _block_diag(jnp.swapaxes(s5_b_im[l], 1, 2))
        bbd = jnp.concatenate([bre, bim], axis=1).astype(BF16)
        cre = _block_diag(jnp.swapaxes(s5_c_re[l], 1, 2)).astype(BF16)
        cim = _block_diag(jnp.swapaxes(s5_c_im[l], 1, 2)).astype(BF16)
        coef, pf, pb = _s5_disc(s5_a_re[l].reshape(2, S5_CH), s5_a_im[l].reshape(2, S5_CH),
                                jnp.repeat(s5_log_dt[l], S5_STATE, axis=1), F32_ROWS)
        dsk = s5_d[l].reshape(1, BR_W)
        wglu = s5_w_glu[l].astype(BF16)
        wg = jnp.concatenate([_block_diag(lru_w_a[l, 0]), _block_diag(lru_w_x[l, 0]),
                              _block_diag(lru_w_a[l, 1]), _block_diag(lru_w_x[l, 1])], axis=1).astype(BF16)
        bg = jnp.concatenate([lru_b_a[l, 0], lru_b_x[l, 0], lru_b_a[l, 1], lru_b_x[l, 1]]).reshape(1, 4 * BR_W)
        cw = lru_conv_w[l]
        cb = lru_conv_b[l].reshape(1, BR_W)
        lam = lru_lambda[l]

        s5_h0 = state_s5[:, l].reshape(n_bs, 2, 2 * S5_CH)

        outs = []
        for (row_off, n_b, t_len, sg, sr, ss, sl, cs, sn, rope) in (
                (0, n_bp, t_p, zeros_bd, zeros_bd, zeros_s5, zeros_lru, cos_p, sin_p, False),
                (ntp // t_s, n_bs, t_s, _state_to_bd_t(state_gla[:, l]), _state_to_bd_t(state_ret[:, l]),
                 s5_h0, state_lru[:, l], cos_s, sin_s, True)):
            m_a, f_gla = _gla(z, row_off, n_b, t_len, wd, bd, sg, e_mat, ind)
            m_b, f_ret = _ret(z, row_off, n_b, t_len, cs, sn, swap, lgl, lg, sr, ind, rope)
            m_c, f_s5 = _s5(z, row_off, n_b, t_len, bbd, cre, cim, coef, pf, pb, dsk, wglu, ss)
            m_d, f_lru = _lru(z, row_off, n_b, t_len, cw, cb, wg, bg, lam, sl)
            outs.append(((m_a, m_b, m_c, m_d), (f_gla, f_ret, f_s5, f_lru)))

        ms = [jnp.concatenate([outs[0][0][n], outs[1][0][n]], axis=0) for n in range(4)]
        f_gla, f_ret, f_s5, f_lru = outs[0][1]
        gla_l.append(_bd_t_to_state(f_gla, GLA_HEADS, GLA_DK, GLA_DV))
        ret_l.append(_bd_t_to_state(f_ret, RET_HEADS, RET_DK, RET_DV))
        s5_l.append(f_s5.reshape(n_bp, 2, 2, S5_GROUPS, S5_STATE))
        lru_l.append(f_lru)

        x, h2t = _merge(ms, zg, x, mod3, norm_ffn[l].reshape(1, D_MODEL), w_branch[l].astype(BF16),
                        w_out[l].astype(BF16), make_row_of_tile(tm), tm)

        wqt = peer_w_q[l].T.astype(BF16)
        vt_l = peer_v[l].reshape(PEER_EXPERTS // PEER_TS, PEER_TS, D_MODEL).transpose(0, 2, 1).astype(BF16)
        kbt = peer_keys[l].reshape(2 * PEER_HEADS, PEER_KEYS, PEER_QDIM // 2).astype(BF16)
        x = _peer(h2t, x, mod3, wqt, kbt, peer_u[l].astype(BF16), vt_l,
                  make_row_of_tile(PEER_TM), PEER_TM, PEER_TE, PEER_TS)

    y = _final_norm(x, norm_final.reshape(1, D_MODEL), tm)
    y_p = y[:ntp].reshape(n_bp, t_p, D_MODEL)
    y_s = y[ntp:].reshape(n_bs, t_s, D_MODEL)
    return (y_p, y_s, jnp.stack(gla_l, axis=1), jnp.stack(ret_l, axis=1),
            jnp.stack(s5_l, axis=1), jnp.stack(lru_l, axis=1))
```

```python
import functools
import math

import jax
import jax.numpy as jnp
from jax import lax
from jax.experimental import pallas as pl
from jax.experimental.pallas import tpu as pltpu

F32 = jnp.float32
BF16 = jnp.bfloat16
HI = lax.Precision.HIGHEST

D_MODEL = 1024
N_MOD = 6
EPS = 1e-6
BR_W = 256
GLA_HEADS, GLA_DK, GLA_DV, GLA_RANK, GLA_TAU, GLA_CHUNK = 4, 32, 64, 16, 16.0, 32
RET_HEADS, RET_DK, RET_DV, RET_CHUNK = 4, 32, 64, 64
ROPE_BASE = 10000.0
GRID_W = 64
S5_GROUP, S5_GROUPS, S5_STATE, S5_RE_MAX = 16, 16, 64, -1e-4
S5_CH = S5_GROUPS * S5_STATE
LRU_BLOCKS, LRU_BW, LRU_C = 4, 64, 8.0
PEER_HEADS, PEER_KEYS, PEER_TOPK, PEER_QDIM = 8, 128, 16, 128
PEER_EXPERTS = PEER_KEYS * PEER_KEYS

LANES = 128
VMEM_LIMIT = 58 * 1024 * 1024

Z_TN = 1024
Z_GATE_W = 4 * D_MODEL
Z_GLA, Z_RET, Z_LRU, Z_S5 = 0, 1024, 2048, 2560
Z_MIX_W = 3072
Z_W = Z_GATE_W + Z_MIX_W

SCAN_TB = 128
TOK_TM = 512
PRE_TM = 1024
PEER_TM, PEER_TE, PEER_TS = 512, 2048, 256
NEG_INF = float("-inf")


def _cparams(sem):
    return pltpu.CompilerParams(dimension_semantics=sem, vmem_limit_bytes=VMEM_LIMIT)


def _nt(a, b):
    return lax.dot_general(a, b, (((1,), (1,)), ((), ())), preferred_element_type=F32)


def _tn(a, b):
    return lax.dot_general(a, b, (((0,), (0,)), ((), ())), preferred_element_type=F32)


def _mm(a, b):
    return jnp.dot(a, b, preferred_element_type=F32)


def _mm_hi(a, b):
    return jnp.dot(a, b, preferred_element_type=F32, precision=HI)


def _split_bf16(x):
    hi = x.astype(BF16)
    return hi, (x - hi.astype(F32)).astype(BF16)


def _mm_exact_rhs(a, b_exact):
    hi, lo = _split_bf16(a)
    return _mm(hi, b_exact) + _mm(lo, b_exact)


def _mm_exact_lhs(a_exact, b):
    hi, lo = _split_bf16(b)
    return _mm(a_exact, hi) + _mm(a_exact, lo)


def _sigmoid(x):
    return jax.nn.sigmoid(x)


def _silu(x):
    return x * jax.nn.sigmoid(x)


def _gelu(x):
    return jax.nn.gelu(x)


GELU_C0 = math.sqrt(2.0 / math.pi)
GELU_C1 = GELU_C0 * 0.044715
BF16_ROWS = 16
F32_ROWS = 8


def _gelu_tanh(x):
    k = -2.0 / math.log(2.0)
    return x / (1.0 + jnp.exp2(x * (k * GELU_C0 + (k * GELU_C1) * (x * x))))


def _log_sigmoid(x):
    return jnp.minimum(x, 0.0) - jnp.log(1.0 + jnp.exp(-jnp.abs(x)))


def _softplus(x):
    return jnp.maximum(x, 0.0) + jnp.log(1.0 + jnp.exp(-jnp.abs(x)))


def _rms_mod(x, gain, sc, sh):
    ms = jnp.mean(x * x, axis=-1, keepdims=True)
    return x * lax.rsqrt(ms + EPS) * gain * (1.0 + sc) + sh


def _adaln_kernel(c_ref, w_ref, b_ref, o_ref):
    o_ref[...] = _mm_hi(_silu(c_ref[...]), w_ref[...]) + b_ref[...]


def _adaln(cond, w_mod, b_mod):
    n_l = w_mod.shape[0]
    rows = cond.shape[0]
    tn = 1536
    return pl.pallas_call(
        _adaln_kernel,
        grid=(n_l, N_MOD * D_MODEL // tn),
        in_specs=[
            pl.BlockSpec((rows, D_MODEL), lambda l, j: (0, 0)),
            pl.BlockSpec((None, D_MODEL, tn), lambda l, j: (l, 0, j)),
            pl.BlockSpec((None, 1, tn), lambda l, j: (l, 0, j)),
        ],
        out_specs=pl.BlockSpec((None, rows, tn), lambda l, j: (l, 0, j)),
        out_shape=jax.ShapeDtypeStruct((n_l, rows, N_MOD * D_MODEL), F32),
        compiler_params=_cparams(("parallel", "parallel")),
        name="adaln",
    )(cond, w_mod, b_mod.reshape(n_l, 1, N_MOD * D_MODEL))


def _premix_kernel(x_ref, sh_ref, sc_ref, g_ref, w_ref, zg_ref, zm_ref, h_scr):
    j = pl.program_id(1)

    @pl.when(j == 0)
    def _():
        h_scr[...] = _rms_mod(x_ref[...], g_ref[...], sc_ref[...], sh_ref[...]).astype(BF16)

    z = _mm(h_scr[...], w_ref[j])

    @pl.when(j < Z_GATE_W // Z_TN)
    def _():
        zg_ref[...] = z.astype(BF16)

    @pl.when(j >= Z_GATE_W // Z_TN)
    def _():
        zm_ref[...] = z


def _premix(x, mod3, gain, w, row_of_tile, tm):
    nt = x.shape[0]
    n_gate = Z_GATE_W // Z_TN
    return pl.pallas_call(
        _premix_kernel,
        grid=(nt // tm, Z_W // Z_TN),
        in_specs=[
            pl.BlockSpec((tm, D_MODEL), lambda i, j: (i, 0)),
            pl.BlockSpec((None, 1, D_MODEL), lambda i, j: (row_of_tile(i), 0, 0)),
            pl.BlockSpec((None, 1, D_MODEL), lambda i, j: (row_of_tile(i), 0, 1)),
            pl.BlockSpec((1, D_MODEL), lambda i, j: (0, 0)),
            pl.BlockSpec((Z_W // Z_TN, D_MODEL, Z_TN), lambda i, j: (0, 0, 0), pipeline_mode=pl.Buffered(1)),
        ],
        out_specs=[
            pl.BlockSpec((tm, Z_TN), lambda i, j: (i, jnp.minimum(j, n_gate - 1))),
            pl.BlockSpec((tm, Z_TN), lambda i, j: (i, jnp.maximum(j - n_gate, 0))),
        ],
        out_shape=[
            jax.ShapeDtypeStruct((nt, Z_GATE_W), BF16),
            jax.ShapeDtypeStruct((nt, Z_MIX_W), F32),
        ],
        scratch_shapes=[pltpu.VMEM((tm, D_MODEL), BF16)],
        compiler_params=_cparams(("parallel", "arbitrary")),
        name="premix",
    )(x, mod3, mod3, gain, w)


def _gla_kernel(z_ref, wd_ref, bd_ref, s0_ref, e_ref, ind_ref, m_ref, sfin_ref,
                la_scr, of_scr, st_scr, p_scr, cum_scr, k_scr, v_scr):
    t_len = z_ref.shape[0]
    c = GLA_CHUNK
    n_chunks = t_len // c
    hk = GLA_HEADS * GLA_DK
    hv = GLA_HEADS * GLA_DV
    scale = GLA_DK ** -0.5

    pre = _mm_hi(z_ref[:, 768:896], wd_ref[...]) + bd_ref[...]
    la_scr[...] = _log_sigmoid(pre) * (1.0 / GLA_TAU)

    ri = lax.broadcasted_iota(jnp.int32, (c, c), 0)
    ci = lax.broadcasted_iota(jnp.int32, (c, c), 1)
    tri_lo = (ri >= ci).astype(BF16)
    tri_up = (ri <= ci).astype(BF16)
    row = lax.broadcasted_iota(jnp.int32, (c, hk), 0)
    bd_mask = (lax.broadcasted_iota(jnp.int32, (hv, hk), 0) // GLA_DV
               == lax.broadcasted_iota(jnp.int32, (hv, hk), 1) // GLA_DK).astype(F32)

    def chunk(base, la, tri, reverse, slot):
        cum = _mm_exact_lhs(tri, la)
        q = z_ref[pl.ds(base, c), 0:128] * scale
        k = z_ref[pl.ds(base, c), 128:256]
        v = z_ref[pl.ds(base, c), 256:512]
        edge = cum[0:1, :] if reverse else cum[c - 1:c, :]
        st = st_scr[slot]
        o = _nt((q * jnp.exp(cum)).astype(BF16), st.astype(BF16))
        ke = k * jnp.exp(edge - cum)
        cum_scr[slot] = cum
        k_scr[slot] = k
        v_scr[slot] = v
        for j in range(c):
            d = jnp.minimum(cum - cum_scr[slot, j:j + 1, :], 0.0)
            p = q * k_scr[slot, j:j + 1, :] * jnp.exp(d)
            keep = (row <= j) if reverse else (row >= j)
            p_scr[slot, j * c:(j + 1) * c, :] = jnp.where(keep, p, 0.0).astype(BF16)
        pe = _mm(p_scr[slot], e_ref[...])
        for j in range(c):
            o = o + pe[j * c:(j + 1) * c, :] * v_scr[slot, j:j + 1, :]
        st_scr[slot] = st * jnp.exp(edge) + bd_mask * _tn(v.astype(BF16), ke.astype(BF16))
        return o

    st_scr[0] = s0_ref[0]
    st_scr[1] = s0_ref[1]

    def sweep(i, carry):
        bf = pl.multiple_of(i * c, c)
        bb = pl.multiple_of((n_chunks - 1 - i) * c, c)
        of_scr[pl.ds(bf, c), :] = chunk(bf, la_scr[pl.ds(bf, c), 0:128], tri_lo, False, 0)
        m_ref[pl.ds(bb, c), :] = chunk(bb, la_scr[pl.ds(bb, c), 128:256], tri_up, True, 1)
        return carry

    lax.fori_loop(0, n_chunks, sweep, 0)
    sfin_ref[0] = st_scr[0]
    sfin_ref[1] = st_scr[1]

    fb = 8 * c

    def finish(i, carry):
        base = pl.multiple_of(i * fb, fb)
        o = of_scr[pl.ds(base, fb), :] + m_ref[pl.ds(base, fb), :]
        ms = _mm_exact_rhs(o * o, ind_ref[...])
        g = z_ref[pl.ds(base, fb), 512:768]
        m_ref[pl.ds(base, fb), :] = o * lax.rsqrt(ms + EPS) * _silu(g)
        return carry

    lax.fori_loop(0, t_len // fb, finish, 0)


def _gla(z, row_off, n_b, t_len, wd, bd, s0t, e_mat, ind):
    hk, hv = GLA_HEADS * GLA_DK, GLA_HEADS * GLA_DV
    c = GLA_CHUNK
    return pl.pallas_call(
        _gla_kernel,
        grid=(n_b,),
        in_specs=[
            pl.BlockSpec((t_len, 1024), lambda b: (row_off + b, Z_GLA // 1024)),
            pl.BlockSpec((128, 256), lambda b: (0, 0)),
            pl.BlockSpec((1, 256), lambda b: (0, 0)),
            pl.BlockSpec((None, 2, hv, hk), lambda b: (b, 0, 0, 0)),
            pl.BlockSpec((hk, hv), lambda b: (0, 0)),
            pl.BlockSpec((hv, hv), lambda b: (0, 0)),
        ],
        out_specs=[
            pl.BlockSpec((t_len, BR_W), lambda b: (b, 0)),
            pl.BlockSpec((None, 2, hv, hk), lambda b: (b, 0, 0, 0)),
        ],
        out_shape=[
            jax.ShapeDtypeStruct((n_b * t_len, BR_W), F32),
            jax.ShapeDtypeStruct((n_b, 2, hv, hk), F32),
        ],
        scratch_shapes=[
            pltpu.VMEM((t_len, 256), F32),
            pltpu.VMEM((t_len, hv), F32),
            pltpu.VMEM((2, hv, hk), F32),
            pltpu.VMEM((2, c * c, hk), BF16),
            pltpu.VMEM((2, c, hk), F32),
            pltpu.VMEM((2, c, hk), F32),
            pltpu.VMEM((2, c, hv), F32),
        ],
        compiler_params=_cparams(("parallel",)),
        name="gla",
    )(z, wd, bd, s0t, e_mat, ind)


def _ret_kernel(z_ref, cos_ref, sin_ref, swap_ref, lgl_ref, lgs_ref, s0_ref, ind_ref, m_ref, sfin_ref,
                qk_scr, of_scr, st_scr, *, rope):
    t_len = z_ref.shape[0]
    c = RET_CHUNK
    n_chunks = t_len // c
    hk = RET_HEADS * RET_DK
    hv = RET_HEADS * RET_DV
    scale = RET_DK ** -0.5

    lgf = lgl_ref[0:1, :]
    lgb = lgl_ref[1:2, :]
    pos = lax.broadcasted_iota(jnp.int32, (c, hk), 0).astype(F32)
    wq_f = jnp.exp(lgf * (pos + 1.0))
    wk_f = jnp.exp(lgf * (c - 1.0 - pos))
    wq_b = jnp.exp(lgb * (c - pos))
    wk_b = jnp.exp(lgb * pos)
    dec_f = jnp.exp(lgf * float(c))
    dec_b = jnp.exp(lgb * float(c))

    ii = lax.broadcasted_iota(jnp.int32, (c, c), 0)
    jj = lax.broadcasted_iota(jnp.int32, (c, c), 1)
    rel = (ii - jj).astype(F32)
    dms = []
    for h in range(RET_HEADS):
        d_f = jnp.where(ii >= jj, jnp.exp(lgs_ref[0, h] * jnp.maximum(rel, 0.0)), 0.0)
        d_b = jnp.where(jj >= ii, jnp.exp(lgs_ref[1, h] * jnp.maximum(-rel, 0.0)), 0.0)
        dms.append(d_f + d_b)
    dmat = jnp.concatenate(dms, axis=1)

    ek_mask = (lax.broadcasted_iota(jnp.int32, (RET_HEADS * c, hk), 0) // c
               == lax.broadcasted_iota(jnp.int32, (RET_HEADS * c, hk), 1) // RET_DK).astype(F32)
    ev_mask = (lax.broadcasted_iota(jnp.int32, (RET_HEADS * c, hv), 0) // c
               == lax.broadcasted_iota(jnp.int32, (RET_HEADS * c, hv), 1) // RET_DV).astype(F32)
    bd_mask = (lax.broadcasted_iota(jnp.int32, (hv, hk), 0) // RET_DV
               == lax.broadcasted_iota(jnp.int32, (hv, hk), 1) // RET_DK).astype(F32)

    st_scr[...] = s0_ref[0]

    def fwd(i, carry):
        base = pl.multiple_of(i * c, c)
        q = z_ref[pl.ds(base, c), 0:128] * scale
        k = z_ref[pl.ds(base, c), 128:256]
        v = z_ref[pl.ds(base, c), 256:512]
        if rope:
            cs = cos_ref[pl.ds(base, c), :]
            sn = sin_ref[pl.ds(base, c), :]
            q = q * cs + _mm_exact_rhs(q, swap_ref[...]) * sn
            k = k * cs + _mm_exact_rhs(k, swap_ref[...]) * sn
        qk_scr[pl.ds(base, c), 0:128] = q
        qk_scr[pl.ds(base, c), 128:256] = k
        kexp = (jnp.concatenate([k] * RET_HEADS, axis=0) * ek_mask).astype(BF16)
        vexp = (jnp.concatenate([v] * RET_HEADS, axis=0) * ev_mask).astype(BF16)
        sc = _nt(q.astype(BF16), kexp) * dmat
        o = _mm(sc.astype(BF16), vexp)
        st = st_scr[...]
        o = o + _nt((q * wq_f).astype(BF16), st.astype(BF16))
        of_scr[pl.ds(base, c), :] = o
        st_scr[...] = st * dec_f + bd_mask * _tn(v.astype(BF16), (k * wk_f).astype(BF16))
        return carry

    lax.fori_loop(0, n_chunks, fwd, 0)
    sfin_ref[0] = st_scr[...]
    st_scr[...] = s0_ref[1]

    def bwd(i, carry):
        base = pl.multiple_of((n_chunks - 1 - i) * c, c)
        q = qk_scr[pl.ds(base, c), 0:128]
        k = qk_scr[pl.ds(base, c), 128:256]
        v = z_ref[pl.ds(base, c), 256:512]
        st = st_scr[...]
        o = of_scr[pl.ds(base, c), :] + _nt((q * wq_b).astype(BF16), st.astype(BF16))
        st_scr[...] = st * dec_b + bd_mask * _tn(v.astype(BF16), (k * wk_b).astype(BF16))
        ms = _mm_exact_rhs(o * o, ind_ref[...])
        g = z_ref[pl.ds(base, c), 512:768]
        m_ref[pl.ds(base, c), :] = o * lax.rsqrt(ms + EPS) * _silu(g)
        return carry

    lax.fori_loop(0, n_chunks, bwd, 0)
    sfin_ref[1] = st_scr[...]


def _ret(z, row_off, n_b, t_len, cos_t, sin_t, swap, lgl, lgs, s0t, ind, rope):
    hk, hv = RET_HEADS * RET_DK, RET_HEADS * RET_DV
    return pl.pallas_call(
        functools.partial(_ret_kernel, rope=rope),
        grid=(n_b,),
        in_specs=[
            pl.BlockSpec((t_len, 1024), lambda b: (row_off + b, Z_RET // 1024)),
            pl.BlockSpec((t_len, hk), lambda b: (0, 0)),
            pl.BlockSpec((t_len, hk), lambda b: (0, 0)),
            pl.BlockSpec((hk, hk), lambda b: (0, 0)),
            pl.BlockSpec((2, hk), lambda b: (0, 0)),
            pl.BlockSpec(memory_space=pltpu.SMEM),
            pl.BlockSpec((None, 2, hv, hk), lambda b: (b, 0, 0, 0)),
            pl.BlockSpec((hv, hv), lambda b: (0, 0)),
        ],
        out_specs=[
            pl.BlockSpec((t_len, BR_W), lambda b: (b, 0)),
            pl.BlockSpec((None, 2, hv, hk), lambda b: (b, 0, 0, 0)),
        ],
        out_shape=[
            jax.ShapeDtypeStruct((n_b * t_len, BR_W), F32),
            jax.ShapeDtypeStruct((n_b, 2, hv, hk), F32),
        ],
        scratch_shapes=[
            pltpu.VMEM((t_len, 2 * hk), F32),
            pltpu.VMEM((t_len, hv), F32),
            pltpu.VMEM((hv, hk), F32),
        ],
        compiler_params=_cparams(("parallel",)),
        name="ret",
    )(z, cos_t, sin_t, swap, lgl, lgs, s0t, ind)


def _s5_disc_kernel(are_ref, aim_ref, ldt_ref, coef_ref, pf_ref, pb_ref):
    tb = pf_ref.shape[0]
    re = jnp.minimum(are_ref[...], S5_RE_MAX)
    im = aim_ref[...]
    dt = jnp.exp(ldt_ref[...])
    er = jnp.exp(re * dt)
    lbr = er * jnp.cos(im * dt)
    lbi = er * jnp.sin(im * dt)
    den = re * re + im * im
    nr = lbr - 1.0
    coef_ref[:, 0:S5_CH] = (nr * re + lbi * im) / den
    coef_ref[:, S5_CH:] = (lbi * re - nr * im) / den
    t = lax.broadcasted_iota(jnp.int32, (tb, S5_CH), 0).astype(F32)
    nf = t + 1.0
    nb = float(tb) - t
    mf = jnp.exp(nf * (re[0:1] * dt[0:1]))
    pf_ref[:, 0:S5_CH] = mf * jnp.cos(nf * (im[0:1] * dt[0:1]))
    pf_ref[:, S5_CH:] = mf * jnp.sin(nf * (im[0:1] * dt[0:1]))
    mb = jnp.exp(nb * (re[1:2] * dt[1:2]))
    pb_ref[:, 0:S5_CH] = mb * jnp.cos(nb * (im[1:2] * dt[1:2]))
    pb_ref[:, S5_CH:] = mb * jnp.sin(nb * (im[1:2] * dt[1:2]))


def _s5_disc(are, aim, ldt, tb):
    return pl.pallas_call(
        _s5_disc_kernel,
        out_shape=[
            jax.ShapeDtypeStruct((2, 2 * S5_CH), F32),
            jax.ShapeDtypeStruct((tb, 2 * S5_CH), F32),
            jax.ShapeDtypeStruct((tb, 2 * S5_CH), F32),
        ],
        compiler_params=pltpu.CompilerParams(vmem_limit_bytes=VMEM_LIMIT),
        name="s5_disc",
    )(are, aim, ldt)


def _s5_kernel(u_ref, bbd_ref, cre_ref, cim_ref, coef_ref, pf_ref, pb_ref, dsk_ref, wglu_ref, h0_ref,
               m_ref, hfin_ref, y_scr, bu_scr, hr_scr, hi_scr, car_scr):
    t_len = u_ref.shape[0]
    tb = SCAN_TB
    sub = pf_ref.shape[0]
    n_blocks = t_len // tb
    row_in = lax.broadcasted_iota(jnp.int32, (tb, LANES), 0) % sub
    steps = [1 << s for s in range(int(math.log2(sub)))]

    def block(base, dr_i, reverse):
        p_ref = pb_ref if reverse else pf_ref
        u = u_ref[pl.ds(base, tb), :]
        bu_scr[...] = _mm(u.astype(BF16), bbd_ref[...])
        for g in range(S5_CH // LANES):
            lo, hi = g * LANES, (g + 1) * LANES
            br = bu_scr[:, lo:hi]
            bi = bu_scr[:, S5_CH + lo:S5_CH + hi]
            cr = coef_ref[dr_i:dr_i + 1, lo:hi]
            ci = coef_ref[dr_i:dr_i + 1, S5_CH + lo:S5_CH + hi]
            hr = cr * br - ci * bi
            hi_ = cr * bi + ci * br
            for d in steps:
                if reverse:
                    pr = p_ref[sub - d:sub - d + 1, lo:hi]
                    pi = p_ref[sub - d:sub - d + 1, S5_CH + lo:S5_CH + hi]
                    keep = row_in < sub - d
                    sr = jnp.where(keep, pltpu.roll(hr, tb - d, 0), 0.0)
                    si = jnp.where(keep, pltpu.roll(hi_, tb - d, 0), 0.0)
                else:
                    pr = p_ref[d - 1:d, lo:hi]
                    pi = p_ref[d - 1:d, S5_CH + lo:S5_CH + hi]
                    keep = row_in >= d
                    sr = jnp.where(keep, pltpu.roll(hr, d, 0), 0.0)
                    si = jnp.where(keep, pltpu.roll(hi_, d, 0), 0.0)
                hr, hi_ = hr + pr * sr - pi * si, hi_ + pr * si + pi * sr
            car = car_scr[0:1, lo:hi]
            cai = car_scr[0:1, S5_CH + lo:S5_CH + hi]
            pwr = p_ref[:, lo:hi]
            pwi = p_ref[:, S5_CH + lo:S5_CH + hi]
            n_grp = tb // sub
            for v in (range(n_grp - 1, -1, -1) if reverse else range(n_grp)):
                gr = hr[v * sub:(v + 1) * sub, :] + pwr * car - pwi * cai
                gi = hi_[v * sub:(v + 1) * sub, :] + pwr * cai + pwi * car
                hr_scr[v * sub:(v + 1) * sub, lo:hi] = gr
                hi_scr[v * sub:(v + 1) * sub, lo:hi] = gi
                edge = 0 if reverse else sub - 1
                car, cai = gr[edge:edge + 1, :], gi[edge:edge + 1, :]
            car_scr[0:1, lo:hi] = car
            car_scr[0:1, S5_CH + lo:S5_CH + hi] = cai
        y = _mm(hr_scr[...].astype(BF16), cre_ref[...]) - _mm(hi_scr[...].astype(BF16), cim_ref[...])
        return u, y

    car_scr[0:1, :] = h0_ref[0:1, :]

    def fwd(i, carry):
        base = pl.multiple_of(i * tb, tb)
        _, y = block(base, 0, False)
        y_scr[pl.ds(base, tb), :] = y
        return carry

    lax.fori_loop(0, n_blocks, fwd, 0)
    hfin_ref[0:1, :] = car_scr[0:1, :]
    car_scr[0:1, :] = h0_ref[1:2, :]

    def bwd(i, carry):
        base = pl.multiple_of((n_blocks - 1 - i) * tb, tb)
        u, y = block(base, 1, True)
        y = _gelu(y_scr[pl.ds(base, tb), :] + y + dsk_ref[...] * u)
        gg = _mm(y.astype(BF16), wglu_ref[...])
        m_ref[pl.ds(base, tb), :] = gg[:, 0:BR_W] * _sigmoid(gg[:, BR_W:])
        return carry

    lax.fori_loop(0, n_blocks, bwd, 0)
    hfin_ref[1:2, :] = car_scr[0:1, :]


def _s5(z, row_off, n_b, t_len, bbd, cre, cim, coef, pf, pb, dsk, wglu, h0):
    tb = SCAN_TB
    sub = pf.shape[0]
    full = lambda shape: pl.BlockSpec(shape, lambda b: (0,) * len(shape))
    return pl.pallas_call(
        _s5_kernel,
        grid=(n_b,),
        in_specs=[
            pl.BlockSpec((t_len, BR_W), lambda b: (row_off + b, Z_S5 // BR_W)),
            full((BR_W, 2 * S5_CH)),
            full((S5_CH, BR_W)),
            full((S5_CH, BR_W)),
            full((2, 2 * S5_CH)),
            full((sub, 2 * S5_CH)),
            full((sub, 2 * S5_CH)),
            full((1, BR_W)),
            full((BR_W, 2 * BR_W)),
            pl.BlockSpec((None, 2, 2 * S5_CH), lambda b: (b, 0, 0)),
        ],
        out_specs=[
            pl.BlockSpec((t_len, BR_W), lambda b: (b, 0)),
            pl.BlockSpec((None, 2, 2 * S5_CH), lambda b: (b, 0, 0)),
        ],
        out_shape=[
            jax.ShapeDtypeStruct((n_b * t_len, BR_W), F32),
            jax.ShapeDtypeStruct((n_b, 2, 2 * S5_CH), F32),
        ],
        scratch_shapes=[
            pltpu.VMEM((t_len, BR_W), F32),
            pltpu.VMEM((tb, 2 * S5_CH), F32),
            pltpu.VMEM((tb, S5_CH), F32),
            pltpu.VMEM((tb, S5_CH), F32),
            pltpu.VMEM((8, 2 * S5_CH), F32),
        ],
        compiler_params=_cparams(("parallel",)),
        name="s5",
    )(z, bbd, cre, cim, coef, pf, pb, dsk, wglu, h0)


def _lru_kernel(z_ref, cw_ref, cb_ref, wg_ref, bg_ref, lam_ref, h0_ref, m_ref, hfin_ref,
                xc_scr, hf_scr, car_scr):
    t_len = z_ref.shape[0]
    tb = SCAN_TB
    n_blocks = t_len // tb
    steps = [1 << s for s in range(int(math.log2(tb)))]

    x = z_ref[:, 0:BR_W]
    trow = lax.broadcasted_iota(jnp.int32, (t_len, BR_W), 0)
    xm1 = jnp.where(trow >= 1, pltpu.roll(x, 1, 0), 0.0)
    xp1 = jnp.where(trow < t_len - 1, pltpu.roll(x, t_len - 1, 0), 0.0)
    xp2 = jnp.where(trow < t_len - 2, pltpu.roll(x, t_len - 2, 0), 0.0)
    xc_scr[...] = (cw_ref[0:1, :] * xm1 + cw_ref[1:2, :] * x + cw_ref[2:3, :] * xp1
                   + cw_ref[3:4, :] * xp2 + cb_ref[...])

    row = lax.broadcasted_iota(jnp.int32, (tb, BR_W), 0)
    sp = _softplus(-lam_ref[...])

    def block(base, dr_i, reverse):
        xc = xc_scr[pl.ds(base, tb), :]
        off = dr_i * 2 * BR_W
        gates = _mm(xc.astype(BF16), wg_ref[:, off:off + 2 * BR_W]) + bg_ref[:, off:off + 2 * BR_W]
        r = _sigmoid(gates[:, 0:BR_W])
        ig = _sigmoid(gates[:, BR_W:])
        log_a = -LRU_C * r * sp[dr_i:dr_i + 1, :]
        a = jnp.exp(log_a)
        th = jnp.tanh(log_a)
        b = jnp.sqrt(-2.0 * th / (1.0 - th)) * (ig * xc)
        for d in steps:
            if reverse:
                keep = row < tb - d
                a_s = jnp.where(keep, pltpu.roll(a, tb - d, 0), 1.0)
                b_s = jnp.where(keep, pltpu.roll(b, tb - d, 0), 0.0)
            else:
                keep = row >= d
                a_s = jnp.where(keep, pltpu.roll(a, d, 0), 1.0)
                b_s = jnp.where(keep, pltpu.roll(b, d, 0), 0.0)
            b = b + a * b_s
            a = a * a_s
        h = b + a * car_scr[dr_i:dr_i + 1, :]
        edge = 0 if reverse else tb - 1
        car_scr[dr_i:dr_i + 1, :] = h[edge:edge + 1, :]
        return h

    car_scr[0:2, :] = h0_ref[...]

    def fwd(i, carry):
        base = pl.multiple_of(i * tb, tb)
        hf_scr[pl.ds(base, tb), :] = block(base, 0, False)
        return carry

    lax.fori_loop(0, n_blocks, fwd, 0)

    def bwd(i, carry):
        base = pl.multiple_of((n_blocks - 1 - i) * tb, tb)
        h = block(base, 1, True) + hf_scr[pl.ds(base, tb), :]
        m_ref[pl.ds(base, tb), :] = h * _gelu(z_ref[pl.ds(base, tb), BR_W:2 * BR_W])
        return carry

    lax.fori_loop(0, n_blocks, bwd, 0)
    hfin_ref[...] = car_scr[0:2, :]


def _lru(z, row_off, n_b, t_len, cw, cb, wg, bg, lam, h0):
    full = lambda shape: pl.BlockSpec(shape, lambda b: (0,) * len(shape))
    return pl.pallas_call(
        _lru_kernel,
        grid=(n_b,),
        in_specs=[
            pl.BlockSpec((t_len, 2 * BR_W), lambda b: (row_off + b, Z_LRU // (2 * BR_W))),
            full((4, BR_W)),
            full((1, BR_W)),
            full((BR_W, 4 * BR_W)),
            full((1, 4 * BR_W)),
            full((2, BR_W)),
            pl.BlockSpec((None, 2, BR_W), lambda b: (b, 0, 0)),
        ],
        out_specs=[
            pl.BlockSpec((t_len, BR_W), lambda b: (b, 0)),
            pl.BlockSpec((None, 2, BR_W), lambda b: (b, 0, 0)),
        ],
        out_shape=[
            jax.ShapeDtypeStruct((n_b * t_len, BR_W), F32),
            jax.ShapeDtypeStruct((n_b, 2, BR_W), F32),
        ],
        scratch_shapes=[
            pltpu.VMEM((t_len, BR_W), F32),
            pltpu.VMEM((t_len, BR_W), F32),
            pltpu.VMEM((8, BR_W), F32),
        ],
        compiler_params=_cparams(("parallel",)),
        name="lru",
    )(z, cw, cb, wg, bg, lam, h0)


def _merge_kernel(ma_ref, mb_ref, mc_ref, md_ref, zg_ref, x_ref, g1_ref, sh2_ref, sc2_ref, gain_ref,
                  wb_ref, wo_ref, xo_ref, h2t_ref):
    acc = None
    for n, m_ref in enumerate((ma_ref, mb_ref, mc_ref, md_ref)):
        proj = _mm(m_ref[...].astype(BF16), wb_ref[n])
        term = _sigmoid(zg_ref[:, n * D_MODEL:(n + 1) * D_MODEL].astype(F32)) * proj
        acc = term if acc is None else acc + term
    xn = x_ref[...] + g1_ref[...] * _mm(acc.astype(BF16), wo_ref[...])
    xo_ref[...] = xn
    h2t_ref[...] = _rms_mod(xn, gain_ref[...], sc2_ref[...], sh2_ref[...]).T.astype(BF16)


def _merge(ms, z, x, mod3, gain, wb, wo, row_of_tile, tm):
    nt = x.shape[0]
    modspec = lambda k: pl.BlockSpec((None, 1, D_MODEL), lambda i: (row_of_tile(i), 0, k))
    return pl.pallas_call(
        _merge_kernel,
        grid=(nt // tm,),
        in_specs=[pl.BlockSpec((tm, BR_W), lambda i: (i, 0))] * 4 + [
            pl.BlockSpec((tm, 4 * D_MODEL), lambda i: (i, 0)),
            pl.BlockSpec((tm, D_MODEL), lambda i: (i, 0)),
            modspec(2), modspec(3), modspec(4),
            pl.BlockSpec((1, D_MODEL), lambda i: (0, 0)),
            pl.BlockSpec((4, BR_W, D_MODEL), lambda i: (0, 0, 0)),
            pl.BlockSpec((D_MODEL, D_MODEL), lambda i: (0, 0)),
        ],
        out_specs=[
            pl.BlockSpec((tm, D_MODEL), lambda i: (i, 0)),
            pl.BlockSpec((D_MODEL, tm), lambda i: (0, i)),
        ],
        out_shape=[
            jax.ShapeDtypeStruct((nt, D_MODEL), F32),
            jax.ShapeDtypeStruct((D_MODEL, nt), BF16),
        ],
        compiler_params=_cparams(("parallel",)),
        name="merge",
    )(*ms, z, x, mod3, mod3, mod3, gain, wb, wo)


def _oddeven_merge_sort_pairs(n):
    pairs = []
    p = 1
    while p < n:
        k = p
        while k >= 1:
            for j in range(k % p, n - k, 2 * k):
                for i in range(min(k, n - j - k)):
                    if (i + j) // (2 * p) == (i + j + k) // (2 * p):
                        pairs.append((i + j, i + j + k))
            k //= 2
        p *= 2
    return pairs


_SORT16 = _oddeven_merge_sort_pairs(PEER_KEYS // F32_ROWS)


def _peer_kernel(h2t_ref, x_ref, g2_ref, wqt_ref, kbt_ref, *rest, te, ts):
    n_sub = te // ts
    u_refs, vt_refs = rest[:n_sub], rest[n_sub:2 * n_sub]
    xo_ref, n_scr, r1_scr, a_scr, b_scr, sc_scr, v_scr, at_scr, wa_scr, yt_scr = rest[2 * n_sub:]
    tm = h2t_ref.shape[1]
    n_lt = tm // LANES
    ic = te // PEER_KEYS
    c_idx = pl.program_id(1)
    nk = PEER_KEYS

    @pl.when(c_idx == 0)
    def _route():
        qt = _mm(wqt_ref[...], h2t_ref[...])
        qtb = qt.astype(BF16)
        qd = PEER_QDIM // 2
        for hs in range(2 * PEER_HEADS):
            sct = _mm(kbt_ref[hs], qtb[hs * qd:(hs + 1) * qd, :])
            for lt in range(n_lt):
                sc_scr[lt, hs * nk:(hs + 1) * nk, :] = sct[:, lt * LANES:(lt + 1) * LANES]
        r8 = lax.broadcasted_iota(jnp.int32, (8, LANES), 0)

        def per_tile(lt, carry):
            def per_head(h, vo):
                o0 = pl.multiple_of(h * (2 * nk), nk)
                o1 = pl.multiple_of(h * (2 * nk) + nk, nk)
                cols = [sc_scr[lt, pl.ds(pl.multiple_of(o0 + F32_ROWS * v, F32_ROWS), F32_ROWS), :]
                        for v in range(nk // F32_ROWS)]
                for ca, cb in _SORT16:
                    cols[ca], cols[cb] = jnp.maximum(cols[ca], cols[cb]), jnp.minimum(cols[ca], cols[cb])
                for r in range(PEER_TOPK):
                    m = jnp.max(cols[0], axis=0, keepdims=True)
                    v_scr[vo + r:vo + r + 1, :] = m
                    hit = cols[0] == m
                    for kk in range(PEER_TOPK - 1 - r):
                        cols[kk] = jnp.where(hit, cols[kk + 1], cols[kk])
                rank1 = jnp.full((nk, LANES), float(PEER_TOPK), F32)
                xs = sc_scr[lt, pl.ds(o1, nk), :]
                for r in range(PEER_TOPK):
                    m = jnp.max(xs, axis=0, keepdims=True)
                    v_scr[vo + PEER_TOPK + r:vo + PEER_TOPK + r + 1, :] = m
                    hit = xs == m
                    rank1 = jnp.where(hit, float(r), rank1)
                    xs = jnp.where(hit, NEG_INF, xs)
                v0 = v_scr[vo:vo + PEER_TOPK, :]
                v1 = v_scr[vo + PEER_TOPK:vo + 2 * PEER_TOPK, :]
                pieces = [v0[0:1, :] + v1]
                for r0 in range(1, 8):
                    pieces.append(jnp.where(r8 < PEER_TOPK // (r0 + 1), v0[r0:r0 + 1, :] + v1[0:8, :], NEG_INF))
                pieces.append(v0[8:16, :] + v1[0:1, :])
                cand = jnp.concatenate(pieces, axis=0)
                top = v0[0:1, :] + v1[0:1, :]
                zsum = jnp.zeros_like(top)
                tau = top
                for r in range(PEER_TOPK):
                    tau = jnp.max(cand, axis=0, keepdims=True)
                    zsum = zsum + jnp.exp(tau - top)
                    cand = jnp.where(cand == tau, NEG_INF, cand)
                ho = pl.multiple_of(h * nk, nk)
                s0 = sc_scr[lt, pl.ds(o0, nk), :]
                s1 = sc_scr[lt, pl.ds(o1, nk), :]
                crank = jnp.zeros((PEER_TOPK, LANES), F32)
                for r1 in range(PEER_TOPK):
                    crank = crank + jnp.where(v0 + v1[r1:r1 + 1, :] >= tau, 1.0, 0.0)
                cnt = jnp.zeros((nk, LANES), F32)
                half = PEER_TOPK // 2
                for m in range(1, half + 1):
                    u_m = jnp.min(jnp.where(crank >= float(m), v0, jnp.inf), axis=0, keepdims=True)
                    cnt = jnp.where(s0 >= u_m, float(m), cnt)
                cnt = jnp.where(s0 >= v0[0:1, :], crank[0:1, :], cnt)
                n_scr[lt, pl.ds(ho, nk), :] = cnt
                r1_scr[lt, pl.ds(ho, nk), :] = rank1
                a_scr[lt, pl.ds(ho, nk), :] = jnp.exp(s0 - v0[0:1, :])
                b_scr[lt, pl.ds(ho, nk), :] = jnp.exp(s1 - v1[0:1, :]) / zsum

            def per_pair(hp, carry2):
                per_head(2 * hp, 0)
                per_head(2 * hp + 1, 2 * PEER_TOPK)
                return carry2

            return lax.fori_loop(0, PEER_HEADS // 2, per_pair, carry)

        lax.fori_loop(0, n_lt, per_tile, 0)
        yt_scr[...] = jnp.zeros_like(yt_scr)

    groups = ts // nk

    def a_stage(k):
        at_scr[k % 2] = _mm(u_refs[k][...], h2t_ref[...])

    def y_stage(k):
        yt_scr[...] += _mm(vt_refs[k][...], wa_scr[k % 2])

    def w_block(k, lt, ii, dep):
        pk = BF16_ROWS
        nv = nk // pk
        cols = slice(lt * LANES, (lt + 1) * LANES)
        acc = [None] * nv
        for h in range(PEER_HEADS):
            row = h * nk + c_idx * ic + k * groups + ii
            n_t = jnp.broadcast_to(n_scr[lt, pl.ds(row, 1), :] + dep, (pk, LANES))
            a_t = jnp.broadcast_to(a_scr[lt, pl.ds(row, 1), :] + dep, (pk, LANES))
            for jv in range(nv):
                r1 = r1_scr[lt, h * nk + jv * pk:h * nk + (jv + 1) * pk, :]
                b1 = b_scr[lt, h * nk + jv * pk:h * nk + (jv + 1) * pk, :]
                term = jnp.where(r1 < n_t, b1, 0.0) * a_t
                acc[jv] = term if acc[jv] is None else acc[jv] + term
        out = None
        for jv in range(nv):
            rows = slice(ii * nk + jv * pk, ii * nk + (jv + 1) * pk)
            out = acc[jv] * _gelu_tanh(at_scr[k % 2, rows, cols])
            wa_scr[k % 2, rows, cols] = out.astype(BF16)
        last = out[0:1, :]
        return jnp.where((last < 2.0) & (last > -2.0), last, 1.0) * 0.0

    dep = jnp.zeros((1, LANES), F32)
    a_stage(0)
    for k in range(n_sub):
        if k + 1 < n_sub:
            a_stage(k + 1)
        if k >= 1:
            y_stage(k - 1)
        for lt in range(n_lt):
            for ii in range(groups):
                dep = w_block(k, lt, ii, dep)
    y_stage(n_sub - 1)

    @pl.when(c_idx == pl.num_programs(1) - 1)
    def _fin():
        xo_ref[...] = x_ref[...] + g2_ref[...] * yt_scr[...].T


def _peer(h2t, x, mod3, wqt, kbt, u, vt, row_of_tile, tm, te, ts):
    nt = x.shape[0]
    n_lt = tm // LANES
    n_rt = PEER_HEADS * PEER_KEYS
    n_sub = te // ts
    u_specs = [pl.BlockSpec((ts, D_MODEL), lambda i, c, k=k: (c * n_sub + k, 0)) for k in range(n_sub)]
    vt_specs = [pl.BlockSpec((None, D_MODEL, ts), lambda i, c, k=k: (c * n_sub + k, 0, 0)) for k in range(n_sub)]
    return pl.pallas_call(
        functools.partial(_peer_kernel, te=te, ts=ts),
        grid=(nt // tm, PEER_EXPERTS // te),
        in_specs=[
            pl.BlockSpec((D_MODEL, tm), lambda i, c: (0, i), pipeline_mode=pl.Buffered(1)),
            pl.BlockSpec((tm, D_MODEL), lambda i, c: (i, 0), pipeline_mode=pl.Buffered(1)),
            pl.BlockSpec((None, 1, D_MODEL), lambda i, c: (row_of_tile(i), 0, 5)),
            pl.BlockSpec((D_MODEL, D_MODEL), lambda i, c: (0, 0), pipeline_mode=pl.Buffered(1)),
            pl.BlockSpec((2 * PEER_HEADS, PEER_KEYS, PEER_QDIM // 2), lambda i, c: (0, 0, 0),
                         pipeline_mode=pl.Buffered(1)),
        ] + u_specs + vt_specs,
        out_specs=pl.BlockSpec((tm, D_MODEL), lambda i, c: (i, 0), pipeline_mode=pl.Buffered(1)),
        out_shape=jax.ShapeDtypeStruct((nt, D_MODEL), F32),
        scratch_shapes=[
            pltpu.VMEM((n_lt, n_rt, LANES), F32),
            pltpu.VMEM((n_lt, n_rt, LANES), F32),
            pltpu.VMEM((n_lt, n_rt, LANES), F32),
            pltpu.VMEM((n_lt, n_rt, LANES), F32),
            pltpu.VMEM((n_lt, 2 * n_rt, LANES), F32),
            pltpu.VMEM((4 * PEER_TOPK, LANES), F32),
            pltpu.VMEM((2, ts, tm), F32),
            pltpu.VMEM((2, ts, tm), BF16),
            pltpu.VMEM((D_MODEL, tm), F32),
        ],
        compiler_params=_cparams(("parallel", "arbitrary")),
        name="peer",
    )(h2t, x, mod3, wqt, kbt, *([u] * n_sub), *([vt] * n_sub))


def _final_kernel(x_ref, g_ref, o_ref):
    x = x_ref[...]
    o_ref[...] = x * lax.rsqrt(jnp.mean(x * x, axis=-1, keepdims=True) + EPS) * g_ref[...]


def _final_norm(x, gain, tm):
    nt = x.shape[0]
    return pl.pallas_call(
        _final_kernel,
        grid=(nt // tm,),
        in_specs=[pl.BlockSpec((tm, D_MODEL), lambda i: (i, 0)), pl.BlockSpec((1, D_MODEL), lambda i: (0, 0))],
        out_specs=pl.BlockSpec((tm, D_MODEL), lambda i: (i, 0)),
        out_shape=jax.ShapeDtypeStruct((nt, D_MODEL), F32),
        compiler_params=_cparams(("parallel",)),
        name="final_norm",
    )(x, gain)


def _block_diag(blocks):
    n, r, c = blocks.shape
    eye = jnp.eye(n, dtype=blocks.dtype)
    return jnp.einsum('nrc,nm->nrmc', blocks, eye).reshape(n * r, n * c)


def _state_to_bd_t(s):
    b, two, h, k, v = s.shape
    eye = jnp.eye(h, dtype=s.dtype)
    return jnp.einsum('bdhkv,hg->bdhvgk', s, eye).reshape(b, two, h * v, h * k)


def _bd_t_to_state(st, h, k, v):
    b = st.shape[0]
    return jnp.einsum('bdhvhk->bdhkv', st.reshape(b, 2, h, v, h, k))


def _rope_tables(t_len):
    rows = t_len // GRID_W
    row = jnp.repeat(jnp.arange(rows), GRID_W).astype(F32)
    col = jnp.tile(jnp.arange(GRID_W), rows).astype(F32)
    n_freq = RET_DK // 4
    inv_freq = ROPE_BASE ** (-jnp.arange(n_freq, dtype=F32) / n_freq)
    ang = jnp.concatenate([row[:, None] * inv_freq, col[:, None] * inv_freq], axis=-1)
    cos, sin = jnp.cos(ang), jnp.sin(ang)
    cos_h = jnp.concatenate([cos, cos], axis=-1)
    sin_h = jnp.concatenate([-sin, sin], axis=-1)
    return jnp.tile(cos_h, (1, RET_HEADS)), jnp.tile(sin_h, (1, RET_HEADS))


def _swap_matrix():
    lane = jnp.arange(RET_HEADS * RET_DK)
    half = RET_DK // 2
    src = jnp.where(lane % RET_DK < half, lane + half, lane - half)
    return (lane[:, None] == src[None, :]).astype(BF16)


def kernel(x_prompt, x_sample, c, state_gla, state_ret, state_s5, state_lru, c_ctx, w_mod, b_mod, norm_mix, norm_ffn, norm_final, w_in, gla_w_decay, gla_b_decay, ret_decay_logit, s5_a_re, s5_a_im, s5_log_dt, s5_b_re, s5_b_im, s5_c_re, s5_c_im, s5_d, s5_w_glu, lru_conv_w, lru_conv_b, lru_w_a, lru_b_a, lru_w_x, lru_b_x, lru_lambda, w_branch, w_out, peer_w_q, peer_keys, peer_u, peer_v):
    n_bp, t_p, _ = x_prompt.shape
    n_bs, t_s, _ = x_sample.shape
    depth = w_in.shape[0]
    ntp, nts = n_bp * t_p, n_bs * t_s
    tm = TOK_TM
    assert ntp % t_s == 0 and all(ntp % t == 0 and t_s % t == 0 for t in (TOK_TM, PRE_TM, PEER_TM))

    x = jnp.concatenate([x_prompt.reshape(ntp, D_MODEL), x_sample.reshape(nts, D_MODEL)], axis=0)

    n_rows = 8 * ((1 + n_bs + 7) // 8)
    cond = jnp.zeros((n_rows, D_MODEL), F32).at[0].set(c_ctx).at[1:1 + n_bs].set(c)
    mods = _adaln(cond, w_mod, b_mod)

    def make_row_of_tile(tile):
        def row_of_tile(i):
            return jnp.where(i < ntp // tile, 0, 1 + (i - ntp // tile) // (t_s // tile))
        return row_of_tile

    zpad = lambda n: jnp.zeros((depth, D_MODEL, n), F32)
    w_in_p = jnp.concatenate([w_in[:, :, 2336:6432], w_in[:, :, 0:800], zpad(224), w_in[:, :, 800:1568], zpad(256),
                              w_in[:, :, 1824:2336], w_in[:, :, 1568:1824], zpad(256)], axis=2).astype(BF16)
    assert w_in_p.shape[2] == Z_W
    w_in_p = w_in_p.reshape(depth, D_MODEL, Z_W // Z_TN, Z_TN).transpose(0, 2, 1, 3)

    hk, hv = GLA_HEADS * GLA_DK, GLA_HEADS * GLA_DV
    e_mat = (jnp.arange(hk)[:, None] // GLA_DK == jnp.arange(hv)[None, :] // GLA_DV).astype(BF16)
    ind = ((jnp.arange(hv)[:, None] // GLA_DV == jnp.arange(hv)[None, :] // GLA_DV).astype(F32) / GLA_DV).astype(BF16)
    swap = _swap_matrix()
    cos_s, sin_s = _rope_tables(t_s)
    cos_p, sin_p = jnp.ones((t_p, hk), F32), jnp.zeros((t_p, hk), F32)

    zeros_bd = jnp.zeros((n_bp, 2, hv, hk), F32)
    zeros_s5 = jnp.zeros((n_bp, 2, 2 * S5_CH), F32)
    zeros_lru = jnp.zeros((n_bp, 2, BR_W), F32)

    gla_l, ret_l, s5_l, lru_l = [], [], [], []
    for l in range(depth):
        mod3 = mods[l].reshape(n_rows, 1, N_MOD * D_MODEL)
        zg, z = _premix(x, mod3, norm_mix[l].reshape(1, D_MODEL), w_in_p[l], make_row_of_tile(PRE_TM), PRE_TM)

        wd = jnp.zeros((128, 256), F32)
        wd = wd.at[0:GLA_RANK, 0:hk].set(gla_w_decay[l, 0]).at[GLA_RANK:2 * GLA_RANK, hk:].set(gla_w_decay[l, 1])
        bd = gla_b_decay[l].reshape(1, 2 * hk)
        lg = jax.nn.log_sigmoid(ret_decay_logit[l].astype(F32))
        lgl = jnp.repeat(lg, RET_DK, axis=1)
        bre = _block_diag(jnp.swapaxes(s5_b_re[l], 1, 2))
        bim = _block_diag(jnp.swapaxes(s5_b_im[l], 1, 2))
        bbd = jnp.concatenate([bre, bim], axis=1).astype(BF16)
        cre = _block_diag(jnp.swapaxes(s5_c_re[l], 1, 2)).astype(BF16)
        cim = _block_diag(jnp.swapaxes(s5_c_im[l], 1, 2)).astype(BF16)
        coef, pf, pb = _s5_disc(s5_a_re[l].reshape(2, S5_CH), s5_a_im[l].reshape(2, S5_CH),
                                jnp.repeat(s5_log_dt[l], S5_STATE, axis=1), F32_ROWS)
        dsk = s5_d[l].reshape(1, BR_W)
        wglu = s5_w_glu[l].astype(BF16)
        wg = jnp.concatenate([_block_diag(lru_w_a[l, 0]), _block_diag(lru_w_x[l, 0]),
                              _block_diag(lru_w_a[l, 1]), _block_diag(lru_w_x[l, 1])], axis=1).astype(BF16)
        bg = jnp.concatenate([lru_b_a[l, 0], lru_b_x[l, 0], lru_b_a[l, 1], lru_b_x[l, 1]]).reshape(1, 4 * BR_W)
        cw = lru_conv_w[l]
        cb = lru_conv_b[l].reshape(1, BR_W)
        lam = lru_lambda[l]

        s5_h0 = state_s5[:, l].reshape(n_bs, 2, 2 * S5_CH)

        outs = []
        for (row_off, n_b, t_len, sg, sr, ss, sl, cs, sn, rope) in (
                (0, n_bp, t_p, zeros_bd, zeros_bd, zeros_s5, zeros_lru, cos_p, sin_p, False),
                (ntp // t_s, n_bs, t_s, _state_to_bd_t(state_gla[:, l]), _state_to_bd_t(state_ret[:, l]),
                 s5_h0, state_lru[:, l], cos_s, sin_s, True)):
            m_a, f_gla = _gla(z, row_off, n_b, t_len, wd, bd, sg, e_mat, ind)
            m_b, f_ret = _ret(z, row_off, n_b, t_len, cs, sn, swap, lgl, lg, sr, ind, rope)
            m_c, f_s5 = _s5(z, row_off, n_b, t_len, bbd, cre, cim, coef, pf, pb, dsk, wglu, ss)
            m_d, f_lru = _lru(z, row_off, n_b, t_len, cw, cb, wg, bg, lam, sl)
            outs.append(((m_a, m_b, m_c, m_d), (f_gla, f_ret, f_s5, f_lru)))

        ms = [jnp.concatenate([outs[0][0][n], outs[1][0][n]], axis=0) for n in range(4)]
        f_gla, f_ret, f_s5, f_lru = outs[0][1]
        gla_l.append(_bd_t_to_state(f_gla, GLA_HEADS, GLA_DK, GLA_DV))
        ret_l.append(_bd_t_to_state(f_ret, RET_HEADS, RET_DK, RET_DV))
        s5_l.append(f_s5.reshape(n_bp, 2, 2, S5_GROUPS, S5_STATE))
        lru_l.append(f_lru)

        x, h2t = _merge(ms, zg, x, mod3, norm_ffn[l].reshape(1, D_MODEL), w_branch[l].astype(BF16),
                        w_out[l].astype(BF16), make_row_of_tile(tm), tm)

        wqt = peer_w_q[l].T.astype(BF16)
        vt_l = peer_v[l].reshape(PEER_EXPERTS // PEER_TS, PEER_TS, D_MODEL).transpose(0, 2, 1).astype(BF16)
        kbt = peer_keys[l].reshape(2 * PEER_HEADS, PEER_KEYS, PEER_QDIM // 2).astype(BF16)
        x = _peer(h2t, x, mod3, wqt, kbt, peer_u[l].astype(BF16), vt_l,
                  make_row_of_tile(PEER_TM), PEER_TM, PEER_TE, PEER_TS)

    y = _final_norm(x, norm_final.reshape(1, D_MODEL), tm)
    y_p = y[:ntp].reshape(n_bp, t_p, D_MODEL)
    y_s = y[ntp:].reshape(n_bs, t_s, D_MODEL)
    return (y_p, y_s, jnp.stack(gla_l, axis=1), jnp.stack(ret_l, axis=1),
            jnp.stack(s5_l, axis=1), jnp.stack(lru_l, axis=1))
```
